```python
import math
import jax
import jax.numpy as jnp
from jax import lax
import numpy as np

D_MODEL = 2048
BATCH = 8
SEQ = 2048
DEPTH = 1

HG_WIDTH = D_MODEL // 2
HG_HEAD_DIM = 128
HG_HEADS = HG_WIDTH // HG_HEAD_DIM
HG_CHUNK = 64
DA_WIDTH = D_MODEL - HG_WIDTH
DA_HEAD_DIM = 64
DA_HEADS = DA_WIDTH // (2 * DA_HEAD_DIM)
DA_Q_BLOCK = 128
N_BUCKETS = 32
MAX_DISTANCE = 128
D_FF = -(-(8 * D_MODEL) // (3 * 256)) * 256
IN_COLS = 5 * HG_WIDTH + 3 * DA_WIDTH
EPS = 1e-6

kernel_name = 'hybrid_hgrn2_diffattn_block'


def rms_norm(x, w):
    xf = x.astype(jnp.float32)
    y = xf * lax.rsqrt(jnp.mean(xf * xf, axis=-1, keepdims=True) + EPS)
    return (y * w.astype(jnp.float32)).astype(x.dtype)


def rel_bucket(rel):
    nb = N_BUCKETS // 2
    max_exact = nb // 2
    ret = jnp.where(rel > 0, nb, 0)
    n = jnp.abs(rel)
    nf = jnp.maximum(n, 1).astype(jnp.float32)
    large = max_exact + (jnp.log(nf / max_exact) / math.log(MAX_DISTANCE / max_exact)
                         * (nb - max_exact)).astype(jnp.int32)
    large = jnp.minimum(large, nb - 1)
    return ret + jnp.where(n < max_exact, n, large)


def hgrn2_scan(q, k, v, logf):
    B, H, S, dk = q.shape
    dv = v.shape[-1]
    C = HG_CHUNK
    N = S // C

    def to_chunks(t):
        return jnp.moveaxis(t.astype(jnp.float32).reshape(B, H, N, C, t.shape[-1]), 2, 0)

    qc, kc, vc, gc = to_chunks(q), to_chunks(k), to_chunks(v), to_chunks(logf)
    mask = jnp.tril(jnp.ones((C, C), dtype=bool))[:, :, None]

    def step(s_prev, inp):
        qi, ki, vi, gi = inp
        b = jnp.cumsum(gi, axis=2)
        inter = jnp.einsum('bhtk,bhkv->bhtv', qi * jnp.exp(b), s_prev)
        diff = b[:, :, :, None, :] - b[:, :, None, :, :]
        decay = jnp.exp(jnp.where(mask, diff, -jnp.inf))
        scores = jnp.einsum('bhtk,bhsk,bhtsk->bhts', qi, ki, decay)
        o = inter + jnp.einsum('bhts,bhsv->bhtv', scores, vi)
        b_last = b[:, :, -1:, :]
        s_new = (jnp.exp(b_last[:, :, 0, :])[..., None] * s_prev
                 + jnp.einsum('bhsk,bhsv->bhkv', ki * jnp.exp(b_last - b), vi))
        return s_new, o

    s0 = jnp.zeros((B, H, dk, dv), jnp.float32)
    _, oc = lax.scan(step, s0, (qc, kc, vc, gc))
    return jnp.moveaxis(oc, 0, 2).reshape(B, H, S, dv)


def diff_attention(q1, q2, k1, k2, v, lam, bias_table):
    B, H, S, d = q1.shape
    Q = DA_Q_BLOCK
    N = S // Q
    scale = d ** -0.5
    kpos = jnp.arange(S, dtype=jnp.int32)

    def blk(inp):
        qb1, qb2, qpos = inp
        bias = bias_table[rel_bucket(kpos[None, :] - qpos[:, None])]
        bias = jnp.moveaxis(bias.astype(jnp.float32), -1, 0)[None]
        s1 = jnp.einsum('bhqd,bhkd->bhqk', qb1, k1).astype(jnp.float32) * scale + bias
        s2 = jnp.einsum('bhqd,bhkd->bhqk', qb2, k2).astype(jnp.float32) * scale + bias
        p = jax.nn.softmax(s1, axis=-1) - lam * jax.nn.softmax(s2, axis=-1)
        return jnp.einsum('bhqk,bhkv->bhqv', p.astype(v.dtype), v)

    qb1 = jnp.moveaxis(q1.reshape(B, H, N, Q, d), 2, 0)
    qb2 = jnp.moveaxis(q2.reshape(B, H, N, Q, d), 2, 0)
    qpos = jnp.arange(S, dtype=jnp.int32).reshape(N, Q)
    o = lax.map(blk, (qb1, qb2, qpos))
    return jnp.moveaxis(o, 0, 2).reshape(B, H, S, v.shape[-1])


def setup_inputs(seed: int = 0):
    key = jax.random.key(seed)
    ks = jax.random.split(key, 17)
    f32 = jnp.float32

    def nrm(k, shape, scale):
        return jax.random.normal(k, shape, f32) * scale

    def gain(k, shape):
        return 1.0 + 0.02 * jax.random.normal(k, shape, f32)

    return {
        'x': nrm(ks[0], (BATCH, SEQ, D_MODEL), 1.0),
        'norm1_w': gain(ks[1], (DEPTH, D_MODEL)),
        'w_in': nrm(ks[2], (DEPTH, D_MODEL, IN_COLS), D_MODEL ** -0.5),
        'hg_lb_logits': nrm(ks[3], (2, DEPTH + 1, HG_WIDTH), 0.5),
        'hg_onorm_w': gain(ks[4], (DEPTH, HG_HEAD_DIM)),
        'lambda_q1': nrm(ks[5], (DEPTH, DA_HEAD_DIM), 0.1),
        'lambda_k1': nrm(ks[6], (DEPTH, DA_HEAD_DIM), 0.1),
        'lambda_q2': nrm(ks[7], (DEPTH, DA_HEAD_DIM), 0.1),
        'lambda_k2': nrm(ks[8], (DEPTH, DA_HEAD_DIM), 0.1),
        'da_subln_w': gain(ks[9], (DEPTH, 2 * DA_HEAD_DIM)),
        'rel_bias': nrm(ks[10], (N_BUCKETS, DA_HEADS), 0.5),
        'w_out': nrm(ks[11], (DEPTH, D_MODEL, D_MODEL), D_MODEL ** -0.5),
        'norm2_w': gain(ks[12], (DEPTH, D_MODEL)),
        'w_gate': nrm(ks[13], (DEPTH, D_MODEL, D_FF), D_MODEL ** -0.5),
        'w_up': nrm(ks[14], (DEPTH, D_MODEL, D_FF), D_MODEL ** -0.5),
        'w_down': nrm(ks[15], (DEPTH, D_FF, D_MODEL), D_FF ** -0.5),
        'final_norm_w': gain(ks[16], (D_MODEL,)),
    }


def reference(x, norm1_w, w_in, hg_lb_logits, hg_onorm_w, lambda_q1, lambda_k1,
              lambda_q2, lambda_k2, da_subln_w, rel_bias, w_out, norm2_w,
              w_gate, w_up, w_down, final_norm_w):
    B, S, _ = x.shape
    splits = [HG_WIDTH, 2 * HG_WIDTH, 3 * HG_WIDTH, 4 * HG_WIDTH, 5 * HG_WIDTH,
              5 * HG_WIDTH + DA_WIDTH, 5 * HG_WIDTH + 2 * DA_WIDTH]
    lower_bounds = jnp.cumsum(jax.nn.softmax(hg_lb_logits.astype(jnp.float32), axis=1), axis=1)

    def heads(t, dh):
        return t.reshape(B, S, -1, dh).transpose(0, 2, 1, 3)

    h = x
    for l in range(DEPTH):
        u = rms_norm(h, norm1_w[l])
        proj = u @ w_in[l]
        hq, hi, hf_fwd, hf_bwd, hg, dq, dk, dv = jnp.split(proj, splits, axis=-1)

        q = heads(jax.nn.silu(hq), HG_HEAD_DIM)
        vin = heads(hi, HG_HEAD_DIM)
        lb_f = lower_bounds[0, l]
        lb_b = lower_bounds[1, l]
        f_f = lb_f + (1.0 - lb_f) * jax.nn.sigmoid(hf_fwd.astype(jnp.float32))
        f_b = lb_b + (1.0 - lb_b) * jax.nn.sigmoid(hf_bwd.astype(jnp.float32))
        f_f = heads(f_f, HG_HEAD_DIM)
        f_b = heads(f_b, HG_HEAD_DIM)
        o_fwd = hgrn2_scan(q, 1.0 - f_f, vin, jnp.log(f_f))
        o_bwd = jnp.flip(hgrn2_scan(jnp.flip(q, 2), jnp.flip(1.0 - f_b, 2),
                                    jnp.flip(vin, 2), jnp.flip(jnp.log(f_b), 2)), 2)
        o_hg = (o_fwd + o_bwd).astype(x.dtype).transpose(0, 2, 1, 3)
        o_hg = rms_norm(o_hg, hg_onorm_w[l]) * jax.nn.silu(hg.reshape(B, S, HG_HEADS, HG_HEAD_DIM))
        o_hg = o_hg.reshape(B, S, HG_WIDTH)

        dq5 = dq.reshape(B, S, DA_HEADS, 2, DA_HEAD_DIM).transpose(0, 2, 3, 1, 4)
        dk5 = dk.reshape(B, S, DA_HEADS, 2, DA_HEAD_DIM).transpose(0, 2, 3, 1, 4)
        dvh = heads(dv, 2 * DA_HEAD_DIM)
        lam_init = 0.8 - 0.6 * math.exp(-0.3 * l)
        lam = (jnp.exp(jnp.sum(lambda_q1[l].astype(jnp.float32) * lambda_k1[l].astype(jnp.float32)))
               - jnp.exp(jnp.sum(lambda_q2[l].astype(jnp.float32) * lambda_k2[l].astype(jnp.float32)))
               + lam_init)
        o_da = diff_attention(dq5[:, :, 0], dq5[:, :, 1], dk5[:, :, 0], dk5[:, :, 1],
                              dvh, lam, rel_bias)
        o_da = rms_norm(o_da.transpose(0, 2, 1, 3), da_subln_w[l]) * (1.0 - lam_init)
        o_da = o_da.reshape(B, S, DA_WIDTH)

        h = h + jnp.concatenate([o_hg, o_da], axis=-1) @ w_out[l]

        u2 = rms_norm(h, norm2_w[l])
        h = h + (jax.nn.silu(u2 @ w_gate[l]) * (u2 @ w_up[l])) @ w_down[l]

    return rms_norm(h, final_norm_w)
```

```python
import functools
import math

import numpy as np
import jax
import jax.numpy as jnp
from jax import lax
from jax.experimental import pallas as pl
from jax.experimental.pallas import tpu as pltpu

F32 = jnp.float32
BF16 = jnp.bfloat16

EPS = 1e-6
HG_HEAD_DIM = 128
DA_HEAD_DIM = 64
N_BUCKETS = 32
MAX_DISTANCE = 128

LANES = 128
HG_CHUNK = 64
HG_LEVELS = 6
BIAS_TILE = 128
VMEM_LIMIT = 56 * 1024 * 1024


def _cparams(sem):
    return pltpu.CompilerParams(dimension_semantics=sem, vmem_limit_bytes=VMEM_LIMIT)


def _dot(a, b):
    return jnp.dot(a, b, preferred_element_type=F32)


def _dot_nt(a, b):
    return lax.dot_general(a, b, (((1,), (1,)), ((), ())), preferred_element_type=F32)


def _dot_tn(a, b):
    return lax.dot_general(a, b, (((0,), (0,)), ((), ())), preferred_element_type=F32)


def _sigmoid(x):
    return 1.0 / (1.0 + jnp.exp(-x))


def _silu(x):
    return x * _sigmoid(x)


def _norm_matmul_kernel(x_ref, nw_ref, w_ref, o_ref, u_ref):
    @pl.when(pl.program_id(1) == 0)
    def _():
        x = x_ref[...].astype(F32)
        ms = jnp.mean(x * x, axis=-1, keepdims=True)
        u_ref[...] = (x * lax.rsqrt(ms + EPS) * nw_ref[...]).astype(BF16)

    o_ref[...] = _dot(u_ref[...], w_ref[...]).astype(o_ref.dtype)


def _norm_matmul(x, nw, w, out_dtype, tm, tn):
    m, d = x.shape
    n = w.shape[1]
    return pl.pallas_call(
        _norm_matmul_kernel,
        grid=(m // tm, n // tn),
        in_specs=[
            pl.BlockSpec((tm, d), lambda i, j: (i, 0)),
            pl.BlockSpec((1, d), lambda i, j: (0, 0)),
            pl.BlockSpec((d, tn), lambda i, j: (0, j)),
        ],
        out_specs=pl.BlockSpec((tm, tn), lambda i, j: (i, j)),
        out_shape=jax.ShapeDtypeStruct((m, n), out_dtype),
        scratch_shapes=[pltpu.VMEM((tm, d), BF16)],
        compiler_params=_cparams(("parallel", "arbitrary")),
        name="norm_matmul",
    )(x, nw.reshape(1, d), w)


def _hgrn_constants():
    c, nl = HG_CHUNK, HG_LEVELS
    df = np.zeros((nl + 2, c, c), np.float32)
    db = np.zeros((nl + 2, c, c), np.float32)
    masks = np.zeros((nl + 1, c, c), np.float32)
    masks[0] = np.eye(c, dtype=np.float32)
    for lvl in range(nl):
        m = 1 << lvl
        for t in range(c):
            p = (t // (2 * m)) * (2 * m)
            if t >= p + m:
                df[lvl, t, p + m:t + 1] = 1.0
                db[lvl, t, p + m:t] = 1.0
            else:
                df[lvl, t, t + 1:p + m] = 1.0
                db[lvl, t, t:p + m] = 1.0
            for s in range(c):
                if s // (2 * m) == t // (2 * m) and s // m != t // m:
                    masks[lvl + 1, t, s] = 1.0
    for t in range(c):
        df[nl, t, :t + 1] = 1.0
        df[nl + 1, t, t + 1:] = 1.0
        db[nl, t, t:] = 1.0
        db[nl + 1, t, :t] = 1.0
    return (df.reshape(-1, c), db.reshape(-1, c), masks)


def _hgrn_kernel(lg_ref, hq_ref, hi_ref, hg_ref, gf_ref, gb_ref, ow_ref, df_ref, db_ref, mk_ref,
                 out_ref, o_scr, qi_scr, ut_scr, dec_scr, *, layer, n_slots):
    c, nl, dh = HG_CHUNK, HG_LEVELS, HG_HEAD_DIM
    seq = hq_ref.shape[0]
    nc = seq // c

    lg = lg_ref[...].astype(F32)
    lbs = []
    for d in range(2):
        rows = lg[d * n_slots:(d + 1) * n_slots]
        e = jnp.exp(rows - jnp.max(rows, axis=0, keepdims=True))
        lbs.append(jnp.sum(e[:layer + 1], axis=0, keepdims=True) / jnp.sum(e, axis=0, keepdims=True))
    lb_f, lb_b = lbs

    row = lax.broadcasted_iota(jnp.int32, (c, dh), 0)

    def exponents(sel_ref, lf):
        hi = lf.astype(BF16)
        lo = (lf - hi.astype(F32)).astype(BF16)
        x = _dot(sel_ref[...], jnp.concatenate([hi, lo], axis=1))
        return jnp.exp(x[:, :dh] + x[:, dh:])

    def intra(ci, carry):
        r0 = pl.multiple_of(ci * c, c)
        q = _silu(hq_ref[pl.ds(r0, c), :].astype(F32))
        v = hi_ref[pl.ds(r0, c), :]
        f_f = lb_f + (1.0 - lb_f) * _sigmoid(gf_ref[pl.ds(r0, c), :])
        f_b = lb_b + (1.0 - lb_b) * _sigmoid(gb_ref[pl.ds(r0, c), :])
        k_f = 1.0 - f_f
        k_b = 1.0 - f_b
        e_f = exponents(df_ref, jnp.log(f_f))
        e_b = exponents(db_ref, jnp.log(f_b))

        a = mk_ref[0] * _dot_nt(q.astype(BF16), (k_f + k_b).astype(BF16))
        for lvl in range(nl):
            ef = e_f[lvl * c:(lvl + 1) * c]
            eb = e_b[lvl * c:(lvl + 1) * c]
            right = ((row >> lvl) & 1) == 1
            qs = q * jnp.where(right, ef, eb)
            ks = jnp.where(right, k_b * eb, k_f * ef)
            a = a + mk_ref[lvl + 1] * _dot_nt(qs.astype(BF16), ks.astype(BF16))
        o_scr[pl.ds(r0, c), :] = _dot(a.astype(BF16), v)

        ei_f = e_f[nl * c:(nl + 1) * c]
        ei_b = e_b[nl * c:(nl + 1) * c]
        qi_scr[pl.ds(r0, c), :] = jnp.concatenate([q * ei_f, q * ei_b], axis=1).astype(BF16)
        ks = jnp.concatenate([k_f * e_f[(nl + 1) * c:], k_b * e_b[(nl + 1) * c:]], axis=1).astype(BF16)
        ut_scr[ci] = _dot_tn(v, ks)
        dec = jnp.concatenate([ei_f[c - 1:c], ei_b[0:1]], axis=1)
        dec_scr[ci] = jnp.broadcast_to(dec, (8, 2 * dh))
        return carry

    lax.fori_loop(0, nc, intra, 0)

    def inter(i, carry):
        st_f, st_b = carry
        cf = i
        cb = nc - 1 - i
        rf = pl.multiple_of(cf * c, c)
        rb = pl.multiple_of(cb * c, c)
        o_scr[pl.ds(rf, c), :] += _dot_nt(qi_scr[pl.ds(rf, c), :dh], st_f.astype(BF16))
        st_f = st_f * dec_scr[cf][0:1, :dh] + ut_scr[cf][:, :dh]
        o_scr[pl.ds(rb, c), :] += _dot_nt(qi_scr[pl.ds(rb, c), dh:], st_b.astype(BF16))
        st_b = st_b * dec_scr[cb][0:1, dh:] + ut_scr[cb][:, dh:]
        return st_f, st_b

    zero = jnp.zeros((dh, dh), F32)
    lax.fori_loop(0, nc, inter, (zero, zero))

    ow = ow_ref[...].astype(F32)

    def finish(ci, carry):
        r0 = pl.multiple_of(ci * c, c)
        o = o_scr[pl.ds(r0, c), :]
        y = o * lax.rsqrt(jnp.mean(o * o, axis=-1, keepdims=True) + EPS) * ow
        out_ref[pl.ds(r0, c), :] = (y * _silu(hg_ref[pl.ds(r0, c), :].astype(F32))).astype(out_ref.dtype)
        return carry

    lax.fori_loop(0, nc, finish, 0)


def _hgrn(main, gates, lb_logits, onorm_w, *, batch, seq, heads, layer, col_q, col_i, col_g):
    dh, c, nl = HG_HEAD_DIM, HG_CHUNK, HG_LEVELS
    n_slots = lb_logits.shape[1]
    lg = lb_logits.reshape(2 * n_slots, heads * dh)
    df, db, masks = _hgrn_constants()
    nc = seq // c
    kern = functools.partial(_hgrn_kernel, layer=layer, n_slots=n_slots)
    return pl.pallas_call(
        kern,
        grid=(batch, heads),
        in_specs=[
            pl.BlockSpec((2 * n_slots, dh), lambda b, h: (0, h)),
            pl.BlockSpec((seq, dh), lambda b, h: (b, col_q + h)),
            pl.BlockSpec((seq, dh), lambda b, h: (b, col_i + h)),
            pl.BlockSpec((seq, dh), lambda b, h: (b, col_g + h)),
            pl.BlockSpec((seq, dh), lambda b, h: (b, h)),
            pl.BlockSpec((seq, dh), lambda b, h: (b, heads + h)),
            pl.BlockSpec((1, dh), lambda b, h: (0, 0)),
            pl.BlockSpec(df.shape, lambda b, h: (0, 0)),
            pl.BlockSpec(db.shape, lambda b, h: (0, 0)),
            pl.BlockSpec(masks.shape, lambda b, h: (0, 0, 0)),
        ],
        out_specs=pl.BlockSpec((seq, dh), lambda b, h: (b, h)),
        out_shape=jax.ShapeDtypeStruct((batch * seq, heads * dh), BF16),
        scratch_shapes=[
            pltpu.VMEM((seq, dh), F32),
            pltpu.VMEM((seq, 2 * dh), BF16),
            pltpu.VMEM((nc, dh, 2 * dh), F32),
            pltpu.VMEM((nc, 8, 2 * dh), F32),
        ],
        compiler_params=_cparams(("parallel", "parallel")),
        name="hgrn2",
    )(lg, main, main, main, gates, gates, onorm_w.reshape(1, dh),
      jnp.asarray(df, BF16), jnp.asarray(db, BF16), jnp.asarray(masks, F32))


def _rel_bucket_index(rel):
    nb = N_BUCKETS // 2
    max_exact = nb // 2
    ret = jnp.where(rel > 0, nb, 0)
    n = jnp.abs(rel)
    nf = jnp.maximum(n, 1).astype(jnp.float32)
    large = max_exact + (jnp.log(nf / max_exact) / math.log(MAX_DISTANCE / max_exact)
                         * (nb - max_exact)).astype(jnp.int32)
    large = jnp.minimum(large, nb - 1)
    return ret + jnp.where(n < max_exact, n, large)


def _bias_tiles_kernel(tbl_ref, bkt_ref, out_ref):
    h = pl.program_id(0)
    bkt = bkt_ref[...]
    acc = jnp.zeros(bkt.shape, F32)
    for cidx in range(N_BUCKETS):
        acc = jnp.where(bkt == cidx, tbl_ref[cidx, h], acc)
    out_ref[0] = acc


def _bias_tiles(rel_bias):
    t = BIAS_TILE
    assert t >= MAX_DISTANCE
    heads = rel_bias.shape[1]
    ii = jnp.arange(t, dtype=jnp.int32)[:, None]
    jj = jnp.arange(t, dtype=jnp.int32)[None, :]
    rel = jnp.stack([t * d + jj - ii for d in range(-2, 3)])
    bkt = _rel_bucket_index(rel).astype(jnp.int32)
    return pl.pallas_call(
        _bias_tiles_kernel,
        grid=(heads,),
        in_specs=[
            pl.BlockSpec(memory_space=pltpu.SMEM),
            pl.BlockSpec((5, t, t), lambda h: (0, 0, 0)),
        ],
        out_specs=pl.BlockSpec((1, 5, t, t), lambda h: (h, 0, 0, 0)),
        out_shape=jax.ShapeDtypeStruct((heads, 5, t, t), F32),
        compiler_params=_cparams(("arbitrary",)),
        name="bias_tiles",
    )(rel_bias.astype(F32), bkt)


def _diff_attn_kernel(lam_ref, q_ref, k_ref, v_ref, tiles_ref, sw_ref, out_ref, bias_scr, *, lam_init):
    tq, d2 = q_ref.shape
    seq = k_ref.shape[0]
    t = BIAS_TILE
    n = pl.program_id(1)

    @pl.when(pl.program_id(2) == 0)
    def _():
        for ib in range(tq // t):
            for jb in range(seq // t):
                d = jnp.clip(jb - (n * (tq // t) + ib), -2, 2) + 2
                bias_scr[ib * t:(ib + 1) * t, jb * t:(jb + 1) * t] = tiles_ref[0, d]

    lv = lam_ref[...].astype(F32)
    lam = (jnp.exp(jnp.sum(lv[0:1] * lv[1:2], axis=-1, keepdims=True))
           - jnp.exp(jnp.sum(lv[2:3] * lv[3:4], axis=-1, keepdims=True)) + lam_init)

    scale = DA_HEAD_DIM ** -0.5
    q = q_ref[...]
    lane = lax.broadcasted_iota(jnp.int32, q.shape, 1)
    zero = jnp.zeros_like(q)
    q1 = jnp.where(lane < DA_HEAD_DIM, q, zero)
    q2 = jnp.where(lane >= DA_HEAD_DIM, q, zero)
    k = k_ref[...]
    bias = bias_scr[...]

    def softmax_parts(qm):
        s = _dot_nt(qm, k) * scale + bias
        e = jnp.exp(s - jnp.max(s, axis=-1, keepdims=True))
        return e, jnp.sum(e, axis=-1, keepdims=True)

    e1, l1 = softmax_parts(q1)
    e2, l2 = softmax_parts(q2)
    p = e1 * (1.0 / l1) - e2 * (lam / l2)
    o = _dot(p.astype(BF16), v_ref[...])
    y = o * lax.rsqrt(jnp.mean(o * o, axis=-1, keepdims=True) + EPS) * sw_ref[...].astype(F32)
    out_ref[...] = (y * (1.0 - lam_init)).astype(out_ref.dtype)


def _diff_attn(main, lam_vecs, tiles, subln_w, *, batch, seq, heads, lam_init, col_q, col_k, col_v, tq):
    d2 = 2 * DA_HEAD_DIM
    nq = seq // tq
    t = BIAS_TILE
    kern = functools.partial(_diff_attn_kernel, lam_init=lam_init)
    return pl.pallas_call(
        kern,
        grid=(heads, nq, batch),
        in_specs=[
            pl.BlockSpec(lam_vecs.shape, lambda h, n, b: (0, 0)),
            pl.BlockSpec((tq, d2), lambda h, n, b: (b * nq + n, col_q + h)),
            pl.BlockSpec((seq, d2), lambda h, n, b: (b, col_k + h)),
            pl.BlockSpec((seq, d2), lambda h, n, b: (b, col_v + h)),
            pl.BlockSpec((1, 5, t, t), lambda h, n, b: (h, 0, 0, 0)),
            pl.BlockSpec((1, d2), lambda h, n, b: (0, 0)),
        ],
        out_specs=pl.BlockSpec((tq, d2), lambda h, n, b: (b * nq + n, h)),
        out_shape=jax.ShapeDtypeStruct((batch * seq, heads * d2), BF16),
        scratch_shapes=[pltpu.VMEM((tq, seq), F32)],
        compiler_params=_cparams(("parallel", "parallel", "arbitrary")),
        name="diff_attn",
    )(lam_vecs, main, main, main, tiles, subln_w.reshape(1, d2))


def _out_proj_kernel(x_ref, a_ref, b_ref, wa_ref, wb_ref, o_ref):
    o_ref[...] = x_ref[...] + _dot(a_ref[...], wa_ref[...]) + _dot(b_ref[...], wb_ref[...])


def _out_proj(x, a, b, wa, wb, tm, tn):
    m, d = x.shape
    ka, kb = a.shape[1], b.shape[1]
    return pl.pallas_call(
        _out_proj_kernel,
        grid=(m // tm, d // tn),
        in_specs=[
            pl.BlockSpec((tm, tn), lambda i, j: (i, j)),
            pl.BlockSpec((tm, ka), lambda i, j: (i, 0)),
            pl.BlockSpec((tm, kb), lambda i, j: (i, 0)),
            pl.BlockSpec((ka, tn), lambda i, j: (0, j)),
            pl.BlockSpec((kb, tn), lambda i, j: (0, j)),
        ],
        out_specs=pl.BlockSpec((tm, tn), lambda i, j: (i, j)),
        out_shape=jax.ShapeDtypeStruct((m, d), F32),
        compiler_params=_cparams(("parallel", "parallel")),
        name="out_proj",
    )(x, a, b, wa, wb)


def _ffn_up_kernel(h_ref, nw_ref, wg_ref, wu_ref, o_ref, u_ref):
    @pl.when(pl.program_id(1) == 0)
    def _():
        x = h_ref[...]
        ms = jnp.mean(x * x, axis=-1, keepdims=True)
        u_ref[...] = (x * lax.rsqrt(ms + EPS) * nw_ref[...]).astype(BF16)

    u = u_ref[...]
    g = _dot(u, wg_ref[...])
    up = _dot(u, wu_ref[...])
    o_ref[...] = (_silu(g) * up).astype(o_ref.dtype)


def _ffn_up(h, nw, wg, wu, tm, tn):
    m, d = h.shape
    n = wg.shape[1]
    return pl.pallas_call(
        _ffn_up_kernel,
        grid=(m // tm, n // tn),
        in_specs=[
            pl.BlockSpec((tm, d), lambda i, j: (i, 0)),
            pl.BlockSpec((1, d), lambda i, j: (0, 0)),
            pl.BlockSpec((d, tn), lambda i, j: (0, j)),
            pl.BlockSpec((d, tn), lambda i, j: (0, j)),
        ],
        out_specs=pl.BlockSpec((tm, tn), lambda i, j: (i, j)),
        out_shape=jax.ShapeDtypeStruct((m, n), BF16),
        scratch_shapes=[pltpu.VMEM((tm, d), BF16)],
        compiler_params=_cparams(("parallel", "arbitrary")),
        name="ffn_up",
    )(h, nw.reshape(1, d), wg, wu)


def _ffn_down_kernel(a_ref, w_ref, h_ref, fw_ref, o_ref, acc_ref):
    kk = pl.program_id(1)

    @pl.when(kk == 0)
    def _():
        acc_ref[...] = h_ref[...]

    acc_ref[...] += _dot(a_ref[...], w_ref[...])

    @pl.when(kk == pl.num_programs(1) - 1)
    def _():
        y = acc_ref[...]
        ms = jnp.mean(y * y, axis=-1, keepdims=True)
        o_ref[...] = y * lax.rsqrt(ms + EPS) * fw_ref[...]


def _ffn_down(a, w, h, fw, tm, tk):
    m, f = a.shape
    d = w.shape[1]
    return pl.pallas_call(
        _ffn_down_kernel,
        grid=(m // tm, f // tk),
        in_specs=[
            pl.BlockSpec((tm, tk), lambda i, k: (i, k)),
            pl.BlockSpec((tk, d), lambda i, k: (k, 0)),
            pl.BlockSpec((tm, d), lambda i, k: (i, 0)),
            pl.BlockSpec((1, d), lambda i, k: (0, 0)),
        ],
        out_specs=pl.BlockSpec((tm, d), lambda i, k: (i, 0)),
        out_shape=jax.ShapeDtypeStruct((m, d), F32),
        scratch_shapes=[pltpu.VMEM((tm, d), F32)],
        compiler_params=_cparams(("parallel", "arbitrary")),
        name="ffn_down",
    )(a, w, h, fw.reshape(1, d))


def kernel(x, norm1_w, w_in, hg_lb_logits, hg_onorm_w, lambda_q1, lambda_k1, lambda_q2, lambda_k2,
           da_subln_w, rel_bias, w_out, norm2_w, w_gate, w_up, w_down, final_norm_w):
    batch, seq, d_model = x.shape
    depth = w_in.shape[0]
    hg_width = hg_lb_logits.shape[-1]
    da_width = d_model - hg_width
    hg_heads = hg_width // HG_HEAD_DIM
    da_heads = da_width // (2 * DA_HEAD_DIM)
    assert w_in.shape[2] == 5 * hg_width + 3 * da_width
    assert seq % HG_CHUNK == 0 and seq % BIAS_TILE == 0
    m = batch * seq
    blk = LANES

    tiles = _bias_tiles(rel_bias)
    h = x.reshape(m, d_model)
    for l in range(depth):
        wl = w_in[l]
        w_main = jnp.concatenate([wl[:, :2 * hg_width], wl[:, 4 * hg_width:]], axis=1).astype(BF16)
        w_gates = wl[:, 2 * hg_width:4 * hg_width].astype(BF16)
        main = _norm_matmul(h, norm1_w[l], w_main, BF16, 1024, 1024)
        gates = _norm_matmul(h, norm1_w[l], w_gates, F32, 1024, 1024)

        o_hg = _hgrn(main, gates, hg_lb_logits, hg_onorm_w[l], batch=batch, seq=seq, heads=hg_heads,
                     layer=l, col_q=0, col_i=hg_width // blk, col_g=2 * hg_width // blk)

        lam_init = 0.8 - 0.6 * math.exp(-0.3 * l)
        lam_vecs = jnp.stack([lambda_q1[l], lambda_k1[l], lambda_q2[l], lambda_k2[l]]).astype(F32)
        base = 3 * hg_width // blk
        o_da = _diff_attn(main, lam_vecs, tiles, da_subln_w[l], batch=batch, seq=seq, heads=da_heads,
                          lam_init=lam_init, col_q=base, col_k=base + da_width // blk,
                          col_v=base + 2 * da_width // blk, tq=256)

        wo = w_out[l].astype(BF16)
        h = _out_proj(h, o_hg, o_da, wo[:hg_width], wo[hg_width:], 1024, 1024)

        act = _ffn_up(h, norm2_w[l], w_gate[l].astype(BF16), w_up[l].astype(BF16), 1024, 512)
        last = l == depth - 1
        assert last, "final norm is fused into the last layer's down projection"
        h = _ffn_down(act, w_down[l].astype(BF16), h, final_norm_w, 512, 512)
    return h.reshape(batch, seq, d_model)
```

```python
import functools
import math

import numpy as np
import jax
import jax.numpy as jnp
from jax import lax
from jax.experimental import pallas as pl
from jax.experimental.pallas import tpu as pltpu

F32 = jnp.float32
BF16 = jnp.bfloat16

EPS = 1e-6
HG_HEAD_DIM = 128
DA_HEAD_DIM = 64
N_BUCKETS = 32
MAX_DISTANCE = 128

LANES = 128
SUBLANES = 8
HG_CHUNK = 64
HG_LEVELS = 6
BIAS_TILE = 128
VMEM_LIMIT = 56 * 1024 * 1024


def _cparams(sem):
    return pltpu.CompilerParams(dimension_semantics=sem, vmem_limit_bytes=VMEM_LIMIT)


def _dot(a, b):
    return jnp.dot(a, b, preferred_element_type=F32)


def _dot_nt(a, b):
    return lax.dot_general(a, b, (((1,), (1,)), ((), ())), preferred_element_type=F32)


def _dot_tn(a, b):
    return lax.dot_general(a, b, (((0,), (0,)), ((), ())), preferred_element_type=F32)


def _sigmoid(x):
    return 1.0 / (1.0 + jnp.exp(-x))


def _silu(x):
    return x * _sigmoid(x)


def _norm_matmul_kernel(x_ref, nw_ref, w_ref, o_ref, u_ref):
    @pl.when(pl.program_id(1) == 0)
    def _():
        x = x_ref[...].astype(F32)
        ms = jnp.mean(x * x, axis=-1, keepdims=True)
        u_ref[...] = (x * lax.rsqrt(ms + EPS) * nw_ref[...]).astype(BF16)

    o_ref[...] = _dot(u_ref[...], w_ref[...]).astype(o_ref.dtype)


def _norm_matmul(x, nw, w, out_dtype, tm, tn):
    m, d = x.shape
    n = w.shape[1]
    return pl.pallas_call(
        _norm_matmul_kernel,
        grid=(m // tm, n // tn),
        in_specs=[
            pl.BlockSpec((tm, d), lambda i, j: (i, 0)),
            pl.BlockSpec((1, d), lambda i, j: (0, 0)),
            pl.BlockSpec((d, tn), lambda i, j: (0, j)),
        ],
        out_specs=pl.BlockSpec((tm, tn), lambda i, j: (i, j)),
        out_shape=jax.ShapeDtypeStruct((m, n), out_dtype),
        scratch_shapes=[pltpu.VMEM((tm, d), BF16)],
        compiler_params=_cparams(("parallel", "arbitrary")),
        name="norm_matmul",
    )(x, nw.reshape(1, d), w)


def _hgrn_constants():
    c, nl = HG_CHUNK, HG_LEVELS
    tri = np.tril(np.ones((c, c), np.float32))
    idx = np.arange(c)
    masks = np.zeros((nl + 1, c, c), np.float32)
    masks[0] = np.eye(c, dtype=np.float32)
    for lvl in range(nl):
        m = 1 << lvl
        same_pair = (idx[:, None] // (2 * m)) == (idx[None, :] // (2 * m))
        other_sibling = (idx[:, None] // m) != (idx[None, :] // m)
        masks[lvl + 1] = (same_pair & other_sibling).astype(np.float32)
    return tri, masks


def _level_operands(lvl, j, q, f_f, f_b, k_f, k_b, beta_f, bx_b, odd, hi2, hi4):
    sl = slice(j * SUBLANES, (j + 1) * SUBLANES)
    qj, ffj, fbj, kfj, kbj = q[sl], f_f[sl], f_b[sl], k_f[sl], k_b[sl]
    if lvl == 0:
        return qj * jnp.where(odd, ffj, fbj), jnp.where(odd, kbj, kfj)
    if lvl == 1:
        up_f, dn_f = pltpu.roll(ffj, SUBLANES - 1, axis=0), pltpu.roll(ffj, 1, axis=0)
        up_b, dn_b = pltpu.roll(fbj, SUBLANES - 1, axis=0), pltpu.roll(fbj, 1, axis=0)
        fq = jnp.where(hi2, jnp.where(odd, ffj * dn_f, ffj), jnp.where(odd, fbj, fbj * up_b))
        ks = jnp.where(hi2, jnp.where(odd, kbj * dn_b, kbj), jnp.where(odd, kfj, kfj * up_f))
        return qj * fq, ks
    bfj, bbj = beta_f[sl], bx_b[sl]
    if lvl == 2:
        r = j * SUBLANES + SUBLANES // 2
        ef = jnp.exp2(-jnp.abs(bfj - beta_f[r - 1:r]))
        eb = jnp.exp2(-jnp.abs(bbj - bx_b[r:r + 1]))
        return qj * jnp.where(hi4, ef, eb), jnp.where(hi4, kbj * eb, kfj * ef)
    mb = (1 << lvl) // SUBLANES
    r = ((j // (2 * mb)) * 2 * mb + mb) * SUBLANES
    ref_f, ref_b = beta_f[r - 1:r], bx_b[r:r + 1]
    if (j // mb) & 1:
        return qj * jnp.exp2(bfj - ref_f), kbj * jnp.exp2(bbj - ref_b)
    return qj * jnp.exp2(ref_b - bbj), kfj * jnp.exp2(ref_f - bfj)


def _hgrn_kernel(lg_ref, hq_ref, hi_ref, hg_ref, gf_ref, gb_ref, ow_ref, tri_ref, mk_ref,
                 out_ref, o_scr, qi_scr, ut_scr, dec_scr, st_scr, *, layer, n_slots):
    c, nl, dh = HG_CHUNK, HG_LEVELS, HG_HEAD_DIM
    seq = hq_ref.shape[0]
    nc = seq // c

    lg = lg_ref[...].astype(F32)
    lbs = []
    for d in range(2):
        rows = lg[d * n_slots:(d + 1) * n_slots]
        e = jnp.exp(rows - jnp.max(rows, axis=0, keepdims=True))
        lbs.append(jnp.sum(e[:layer + 1], axis=0, keepdims=True) / jnp.sum(e, axis=0, keepdims=True))
    lb_f, lb_b = lbs

    pos = lax.broadcasted_iota(jnp.int32, (SUBLANES, dh), 0)
    odd = (pos & 1) != 0
    hi2 = (pos & 2) != 0
    hi4 = (pos & 4) != 0

    def split(x):
        hi = x.astype(BF16)
        return hi, (x - hi.astype(F32)).astype(BF16)

    def intra(ci, carry):
        r0 = pl.multiple_of(ci * c, c)
        q = _silu(hq_ref[pl.ds(r0, c), :].astype(F32))
        v = hi_ref[pl.ds(r0, c), :]
        f_f = lb_f + (1.0 - lb_f) * _sigmoid(gf_ref[pl.ds(r0, c), :])
        f_b = lb_b + (1.0 - lb_b) * _sigmoid(gb_ref[pl.ds(r0, c), :])
        k_f = 1.0 - f_f
        k_b = 1.0 - f_b
        lf_f = jnp.log2(f_f)
        lf_b = jnp.log2(f_b)
        pre = _dot(tri_ref[...], jnp.concatenate(split(lf_f) + split(lf_b), axis=1))
        beta_f = pre[:, :dh] + pre[:, dh:2 * dh]
        beta_b = pre[:, 2 * dh:3 * dh] + pre[:, 3 * dh:]
        bx_b = beta_b - lf_b
        tot_f = beta_f[c - 1:c]
        tot_b = beta_b[c - 1:c]

        a = mk_ref[0] * _dot_nt(q.astype(BF16), (k_f + k_b).astype(BF16))
        for lvl in range(nl):
            ops = [_level_operands(lvl, j, q, f_f, f_b, k_f, k_b, beta_f, bx_b, odd, hi2, hi4)
                   for j in range(c // SUBLANES)]
            qs = jnp.concatenate([o[0] for o in ops], axis=0)
            ks = jnp.concatenate([o[1] for o in ops], axis=0)
            a = a + mk_ref[lvl + 1] * _dot_nt(qs.astype(BF16), ks.astype(BF16))
        o_scr[pl.ds(r0, c), :] = _dot(a.astype(BF16), v)

        qi = jnp.concatenate([q * jnp.exp2(beta_f), q * jnp.exp2(tot_b - bx_b)], axis=1)
        qi_scr[pl.ds(r0, c), :] = qi.astype(BF16)
        ks = jnp.concatenate([k_f * jnp.exp2(tot_f - beta_f), k_b * jnp.exp2(bx_b)], axis=1)
        ut_scr[ci] = _dot_tn(v, ks.astype(BF16))
        dec = jnp.exp2(jnp.concatenate([tot_f, tot_b], axis=1))
        dec_scr[ci] = jnp.broadcast_to(dec, (8, 2 * dh))
        return carry

    lax.fori_loop(0, nc, intra, 0, unroll=8)

    def states(i, carry):
        st_f, st_b = carry
        cf = i
        cb = nc - 1 - i
        st_scr[cf, :, :dh] = st_f.astype(BF16)
        st_scr[cb, :, dh:] = st_b.astype(BF16)
        st_f = st_f * dec_scr[cf, 0:1, :dh] + ut_scr[cf, :, :dh]
        st_b = st_b * dec_scr[cb, 0:1, dh:] + ut_scr[cb, :, dh:]
        return st_f, st_b

    zero = jnp.zeros((dh, dh), F32)
    lax.fori_loop(0, nc, states, (zero, zero))

    ow = ow_ref[...].astype(F32)

    def finish(ci, carry):
        r0 = pl.multiple_of(ci * c, c)
        o = o_scr[pl.ds(r0, c), :] + _dot_nt(qi_scr[pl.ds(r0, c), :], st_scr[ci])
        y = o * lax.rsqrt(jnp.mean(o * o, axis=-1, keepdims=True) + EPS) * ow
        out_ref[pl.ds(r0, c), :] = (y * _silu(hg_ref[pl.ds(r0, c), :].astype(F32))).astype(out_ref.dtype)
        return carry

    lax.fori_loop(0, nc, finish, 0, unroll=8)


def _hgrn(main, gates, lb_logits, onorm_w, *, batch, seq, heads, layer, col_q, col_i, col_g):
    dh, c = HG_HEAD_DIM, HG_CHUNK
    n_slots = lb_logits.shape[1]
    lg = lb_logits.reshape(2 * n_slots, heads * dh)
    tri, masks = _hgrn_constants()
    nc = seq // c
    kern = functools.partial(_hgrn_kernel, layer=layer, n_slots=n_slots)
    return pl.pallas_call(
        kern,
        grid=(batch, heads),
        in_specs=[
            pl.BlockSpec((2 * n_slots, dh), lambda b, h: (0, h)),
            pl.BlockSpec((seq, dh), lambda b, h: (b, col_q + h)),
            pl.BlockSpec((seq, dh), lambda b, h: (b, col_i + h)),
            pl.BlockSpec((seq, dh), lambda b, h: (b, col_g + h)),
            pl.BlockSpec((seq, dh), lambda b, h: (b, h)),
            pl.BlockSpec((seq, dh), lambda b, h: (b, heads + h)),
            pl.BlockSpec((1, dh), lambda b, h: (0, 0)),
            pl.BlockSpec(tri.shape, lambda b, h: (0, 0)),
            pl.BlockSpec(masks.shape, lambda b, h: (0, 0, 0)),
        ],
        out_specs=pl.BlockSpec((seq, dh), lambda b, h: (b, h)),
        out_shape=jax.ShapeDtypeStruct((batch * seq, heads * dh), BF16),
        scratch_shapes=[
            pltpu.VMEM((seq, dh), F32),
            pltpu.VMEM((seq, 2 * dh), BF16),
            pltpu.VMEM((nc, dh, 2 * dh), F32),
            pltpu.VMEM((nc, 8, 2 * dh), F32),
            pltpu.VMEM((nc, dh, 2 * dh), BF16),
        ],
        compiler_params=_cparams(("parallel", "parallel")),
        name="hgrn2",
    )(lg, main, main, main, gates, gates, onorm_w.reshape(1, dh),
      jnp.asarray(tri, BF16), jnp.asarray(masks, F32))


def _rel_bucket_index(rel):
    nb = N_BUCKETS // 2
    max_exact = nb // 2
    ret = jnp.where(rel > 0, nb, 0)
    n = jnp.abs(rel)
    nf = jnp.maximum(n, 1).astype(jnp.float32)
    large = max_exact + (jnp.log(nf / max_exact) / math.log(MAX_DISTANCE / max_exact)
                         * (nb - max_exact)).astype(jnp.int32)
    large = jnp.minimum(large, nb - 1)
    return ret + jnp.where(n < max_exact, n, large)


def _bias_tiles_kernel(tbl_ref, bkt_ref, out_ref):
    h = pl.program_id(0)
    bkt = bkt_ref[...]
    acc = jnp.zeros(bkt.shape, F32)
    for cidx in range(N_BUCKETS):
        acc = jnp.where(bkt == cidx, tbl_ref[cidx, h], acc)
    out_ref[0] = acc


def _bias_tiles(rel_bias):
    t = BIAS_TILE
    assert t >= MAX_DISTANCE
    heads = rel_bias.shape[1]
    ii = jnp.arange(t, dtype=jnp.int32)[:, None]
    jj = jnp.arange(t, dtype=jnp.int32)[None, :]
    rel = jnp.stack([t * d + jj - ii for d in range(-2, 3)])
    bkt = _rel_bucket_index(rel).astype(jnp.int32)
    return pl.pallas_call(
        _bias_tiles_kernel,
        grid=(heads,),
        in_specs=[
            pl.BlockSpec(memory_space=pltpu.SMEM),
            pl.BlockSpec((5, t, t), lambda h: (0, 0, 0)),
        ],
        out_specs=pl.BlockSpec((1, 5, t, t), lambda h: (h, 0, 0, 0)),
        out_shape=jax.ShapeDtypeStruct((heads, 5, t, t), F32),
        compiler_params=_cparams(("arbitrary",)),
        name="bias_tiles",
    )(rel_bias.astype(F32), bkt)


def _diff_attn_kernel(lam_ref, q_ref, k_ref, v_ref, tiles_ref, sw_ref, out_ref, bias_scr, *, lam_init):
    tq, d2 = q_ref.shape
    seq = k_ref.shape[0]
    t = BIAS_TILE
    n = pl.program_id(1)

    @pl.when(pl.program_id(2) == 0)
    def _():
        for ib in range(tq // t):
            for jb in range(seq // t):
                d = jnp.clip(jb - (n * (tq // t) + ib), -2, 2) + 2
                bias_scr[ib * t:(ib + 1) * t, jb * t:(jb + 1) * t] = tiles_ref[0, d]

    lv = lam_ref[...].astype(F32)
    lam = (jnp.exp(jnp.sum(lv[0:1] * lv[1:2], axis=-1, keepdims=True))
           - jnp.exp(jnp.sum(lv[2:3] * lv[3:4], axis=-1, keepdims=True)) + lam_init)

    scale = DA_HEAD_DIM ** -0.5
    q = q_ref[...]
    lane = lax.broadcasted_iota(jnp.int32, q.shape, 1)
    zero = jnp.zeros_like(q)
    q1 = jnp.where(lane < DA_HEAD_DIM, q, zero)
    q2 = jnp.where(lane >= DA_HEAD_DIM, q, zero)
    k = k_ref[...]
    bias = bias_scr[...]

    def softmax_parts(qm):
        s = _dot_nt(qm, k) * scale + bias
        e = jnp.exp(s - jnp.max(s, axis=-1, keepdims=True))
        return e, jnp.sum(e, axis=-1, keepdims=True)

    e1, l1 = softmax_parts(q1)
    e2, l2 = softmax_parts(q2)
    p = e1 * (1.0 / l1) - e2 * (lam / l2)
    o = _dot(p.astype(BF16), v_ref[...])
    y = o * lax.rsqrt(jnp.mean(o * o, axis=-1, keepdims=True) + EPS) * sw_ref[...].astype(F32)
    out_ref[...] = (y * (1.0 - lam_init)).astype(out_ref.dtype)


def _diff_attn(main, lam_vecs, tiles, subln_w, *, batch, seq, heads, lam_init, col_q, col_k, col_v, tq):
    d2 = 2 * DA_HEAD_DIM
    nq = seq // tq
    t = BIAS_TILE
    kern = functools.partial(_diff_attn_kernel, lam_init=lam_init)
    return pl.pallas_call(
        kern,
        grid=(heads, nq, batch),
        in_specs=[
            pl.BlockSpec(lam_vecs.shape, lambda h, n, b: (0, 0)),
            pl.BlockSpec((tq, d2), lambda h, n, b: (b * nq + n, col_q + h)),
            pl.BlockSpec((seq, d2), lambda h, n, b: (b, col_k + h)),
            pl.BlockSpec((seq, d2), lambda h, n, b: (b, col_v + h)),
            pl.BlockSpec((1, 5, t, t), lambda h, n, b: (h, 0, 0, 0)),
            pl.BlockSpec((1, d2), lambda h, n, b: (0, 0)),
        ],
        out_specs=pl.BlockSpec((tq, d2), lambda h, n, b: (b * nq + n, h)),
        out_shape=jax.ShapeDtypeStruct((batch * seq, heads * d2), BF16),
        scratch_shapes=[pltpu.VMEM((tq, seq), F32)],
        compiler_params=_cparams(("parallel", "parallel", "arbitrary")),
        name="diff_attn",
    )(lam_vecs, main, main, main, tiles, subln_w.reshape(1, d2))


def _out_proj_kernel(x_ref, a_ref, b_ref, wa_ref, wb_ref, o_ref):
    o_ref[...] = x_ref[...] + _dot(a_ref[...], wa_ref[...]) + _dot(b_ref[...], wb_ref[...])


def _out_proj(x, a, b, wa, wb, tm, tn):
    m, d = x.shape
    ka, kb = a.shape[1], b.shape[1]
    return pl.pallas_call(
        _out_proj_kernel,
        grid=(m // tm, d // tn),
        in_specs=[
            pl.BlockSpec((tm, tn), lambda i, j: (i, j)),
            pl.BlockSpec((tm, ka), lambda i, j: (i, 0)),
            pl.BlockSpec((tm, kb), lambda i, j: (i, 0)),
            pl.BlockSpec((ka, tn), lambda i, j: (0, j)),
            pl.BlockSpec((kb, tn), lambda i, j: (0, j)),
        ],
        out_specs=pl.BlockSpec((tm, tn), lambda i, j: (i, j)),
        out_shape=jax.ShapeDtypeStruct((m, d), F32),
        compiler_params=_cparams(("parallel", "parallel")),
        name="out_proj",
    )(x, a, b, wa, wb)


def _ffn_up_kernel(h_ref, nw_ref, wg_ref, wu_ref, o_ref, u_ref):
    @pl.when(pl.program_id(1) == 0)
    def _():
        x = h_ref[...]
        ms = jnp.mean(x * x, axis=-1, keepdims=True)
        u_ref[...] = (x * lax.rsqrt(ms + EPS) * nw_ref[...]).astype(BF16)

    u = u_ref[...]
    g = _dot(u, wg_ref[...])
    up = _dot(u, wu_ref[...])
    o_ref[...] = (_silu(g) * up).astype(o_ref.dtype)


def _ffn_up(h, nw, wg, wu, tm, tn):
    m, d = h.shape
    n = wg.shape[1]
    return pl.pallas_call(
        _ffn_up_kernel,
        grid=(m // tm, n // tn),
        in_specs=[
            pl.BlockSpec((tm, d), lambda i, j: (i, 0)),
            pl.BlockSpec((1, d), lambda i, j: (0, 0)),
            pl.BlockSpec((d, tn), lambda i, j: (0, j)),
            pl.BlockSpec((d, tn), lambda i, j: (0, j)),
        ],
        out_specs=pl.BlockSpec((tm, tn), lambda i, j: (i, j)),
        out_shape=jax.ShapeDtypeStruct((m, n), BF16),
        scratch_shapes=[pltpu.VMEM((tm, d), BF16)],
        compiler_params=_cparams(("parallel", "arbitrary")),
        name="ffn_up",
    )(h, nw.reshape(1, d), wg, wu)


def _ffn_down_kernel(a_ref, w_ref, h_ref, fw_ref, o_ref, acc_ref):
    kk = pl.program_id(1)

    @pl.when(kk == 0)
    def _():
        acc_ref[...] = h_ref[...]

    acc_ref[...] += _dot(a_ref[...], w_ref[...])

    @pl.when(kk == pl.num_programs(1) - 1)
    def _():
        y = acc_ref[...]
        ms = jnp.mean(y * y, axis=-1, keepdims=True)
        o_ref[...] = y * lax.rsqrt(ms + EPS) * fw_ref[...]


def _ffn_down(a, w, h, fw, tm, tk):
    m, f = a.shape
    d = w.shape[1]
    return pl.pallas_call(
        _ffn_down_kernel,
        grid=(m // tm, f // tk),
        in_specs=[
            pl.BlockSpec((tm, tk), lambda i, k: (i, k)),
            pl.BlockSpec((tk, d), lambda i, k: (k, 0)),
            pl.BlockSpec((tm, d), lambda i, k: (i, 0)),
            pl.BlockSpec((1, d), lambda i, k: (0, 0)),
        ],
        out_specs=pl.BlockSpec((tm, d), lambda i, k: (i, 0)),
        out_shape=jax.ShapeDtypeStruct((m, d), F32),
        scratch_shapes=[pltpu.VMEM((tm, d), F32)],
        compiler_params=_cparams(("parallel", "arbitrary")),
        name="ffn_down",
    )(a, w, h, fw.reshape(1, d))


def kernel(x, norm1_w, w_in, hg_lb_logits, hg_onorm_w, lambda_q1, lambda_k1, lambda_q2, lambda_k2,
           da_subln_w, rel_bias, w_out, norm2_w, w_gate, w_up, w_down, final_norm_w):
    batch, seq, d_model = x.shape
    depth = w_in.shape[0]
    hg_width = hg_lb_logits.shape[-1]
    da_width = d_model - hg_width
    hg_heads = hg_width // HG_HEAD_DIM
    da_heads = da_width // (2 * DA_HEAD_DIM)
    assert w_in.shape[2] == 5 * hg_width + 3 * da_width
    assert seq % HG_CHUNK == 0 and seq % BIAS_TILE == 0
    m = batch * seq
    blk = LANES

    tiles = _bias_tiles(rel_bias)
    h = x.reshape(m, d_model)
    for l in range(depth):
        wl = w_in[l]
        w_main = jnp.concatenate([wl[:, :2 * hg_width], wl[:, 4 * hg_width:]], axis=1).astype(BF16)
        w_gates = wl[:, 2 * hg_width:4 * hg_width].astype(BF16)
        main = _norm_matmul(h, norm1_w[l], w_main, BF16, 1024, 1024)
        gates = _norm_matmul(h, norm1_w[l], w_gates, F32, 1024, 1024)

        o_hg = _hgrn(main, gates, hg_lb_logits, hg_onorm_w[l], batch=batch, seq=seq, heads=hg_heads,
                     layer=l, col_q=0, col_i=hg_width // blk, col_g=2 * hg_width // blk)

        lam_init = 0.8 - 0.6 * math.exp(-0.3 * l)
        lam_vecs = jnp.stack([lambda_q1[l], lambda_k1[l], lambda_q2[l], lambda_k2[l]]).astype(F32)
        base = 3 * hg_width // blk
        o_da = _diff_attn(main, lam_vecs, tiles, da_subln_w[l], batch=batch, seq=seq, heads=da_heads,
                          lam_init=lam_init, col_q=base, col_k=base + da_width // blk,
                          col_v=base + 2 * da_width // blk, tq=256)

        wo = w_out[l].astype(BF16)
        h = _out_proj(h, o_hg, o_da, wo[:hg_width], wo[hg_width:], 1024, 1024)

        act = _ffn_up(h, norm2_w[l], w_gate[l].astype(BF16), w_up[l].astype(BF16), 1024, 512)
        last = l == depth - 1
        assert last, "final norm is fused into the last layer's down projection"
        h = _ffn_down(act, w_down[l].astype(BF16), h, final_norm_w, 512, 512)
    return h.reshape(batch, seq, d_model)
```

```python
import functools
import math

import numpy as np
import jax
import jax.numpy as jnp
from jax import lax
from jax.experimental import pallas as pl
from jax.experimental.pallas import tpu as pltpu

F32 = jnp.float32
BF16 = jnp.bfloat16

EPS = 1e-6
LOG2E = math.log2(math.e)
HG_HEAD_DIM = 128
DA_HEAD_DIM = 64
N_BUCKETS = 32
MAX_DISTANCE = 128

LANES = 128
SUBLANES = 8
HG_CHUNK = 64
HG_LEVELS = 6
BIAS_TILE = 128
VMEM_LIMIT = 56 * 1024 * 1024

TILES = {
    "in_proj_m": 1024, "in_proj_n": 1024,
    "attn_q": 256,
    "out_proj_m": 512,
    "ffn_up_m": 1024, "ffn_up_n": 512,
    "ffn_down_m": 256,
}


def _cparams(sem):
    return pltpu.CompilerParams(dimension_semantics=sem, vmem_limit_bytes=VMEM_LIMIT)


def _dot(a, b):
    return jnp.dot(a, b, preferred_element_type=F32)


def _dot_nt(a, b):
    return lax.dot_general(a, b, (((1,), (1,)), ((), ())), preferred_element_type=F32)


def _dot_tn(a, b):
    return lax.dot_general(a, b, (((0,), (0,)), ((), ())), preferred_element_type=F32)


def _sigmoid(x):
    return 1.0 / (1.0 + jnp.exp(-x))


def _silu(x):
    return x * _sigmoid(x)


def _in_proj_kernel(x_ref, nw_ref, w_ref, main_ref, gates_ref, u_ref, *, n_main):
    j = pl.program_id(1)

    @pl.when(j == 0)
    def _():
        x = x_ref[...].astype(F32)
        ms = jnp.mean(x * x, axis=-1, keepdims=True)
        u_ref[...] = (x * lax.rsqrt(ms + EPS) * nw_ref[...]).astype(BF16)

    @pl.when(j < n_main)
    def _():
        main_ref[...] = _dot(u_ref[...], w_ref[...]).astype(main_ref.dtype)

    @pl.when(j >= n_main)
    def _():
        gates_ref[...] = _dot(u_ref[...], w_ref[...])


def _in_proj(x, nw, w, n_main_cols, tm, tn):
    m, d = x.shape
    n = w.shape[1]
    n_main = n_main_cols // tn
    kern = functools.partial(_in_proj_kernel, n_main=n_main)
    return pl.pallas_call(
        kern,
        grid=(m // tm, n // tn),
        in_specs=[
            pl.BlockSpec((tm, d), lambda i, j: (i, 0)),
            pl.BlockSpec((1, d), lambda i, j: (0, 0)),
            pl.BlockSpec((d, tn), lambda i, j: (0, j)),
        ],
        out_specs=[
            pl.BlockSpec((tm, tn), lambda i, j: (i, jnp.minimum(j, n_main - 1))),
            pl.BlockSpec((tm, tn), lambda i, j: (i, jnp.maximum(j - n_main, 0))),
        ],
        out_shape=[
            jax.ShapeDtypeStruct((m, n_main_cols), BF16),
            jax.ShapeDtypeStruct((m, n - n_main_cols), F32),
        ],
        scratch_shapes=[pltpu.VMEM((tm, d), BF16)],
        compiler_params=_cparams(("parallel", "arbitrary")),
        name="in_proj",
    )(x, nw.reshape(1, d), w)


def _hgrn_constants():
    c, nl = HG_CHUNK, HG_LEVELS
    tri = np.tril(np.ones((c, c), np.float32))
    idx = np.arange(c)
    masks = np.zeros((nl + 1, c, c), np.float32)
    masks[0] = np.eye(c, dtype=np.float32)
    for lvl in range(nl):
        m = 1 << lvl
        same_pair = (idx[:, None] // (2 * m)) == (idx[None, :] // (2 * m))
        other_sibling = (idx[:, None] // m) != (idx[None, :] // m)
        masks[lvl + 1] = (same_pair & other_sibling).astype(np.float32)
    return tri, masks


def _level_operands(lvl, j, q, f_f, f_b, k_f, k_b, beta_f, bx_b, odd, hi2, hi4):
    sl = slice(j * SUBLANES, (j + 1) * SUBLANES)
    qj, ffj, fbj, kfj, kbj = q[sl], f_f[sl], f_b[sl], k_f[sl], k_b[sl]
    if lvl == 0:
        return qj * jnp.where(odd, ffj, fbj), jnp.where(odd, kbj, kfj)
    if lvl == 1:
        up_f, dn_f = pltpu.roll(ffj, SUBLANES - 1, axis=0), pltpu.roll(ffj, 1, axis=0)
        up_b, dn_b = pltpu.roll(fbj, SUBLANES - 1, axis=0), pltpu.roll(fbj, 1, axis=0)
        fq = jnp.where(hi2, jnp.where(odd, ffj * dn_f, ffj), jnp.where(odd, fbj, fbj * up_b))
        ks = jnp.where(hi2, jnp.where(odd, kbj * dn_b, kbj), jnp.where(odd, kfj, kfj * up_f))
        return qj * fq, ks
    bfj, bbj = beta_f[sl], bx_b[sl]
    if lvl == 2:
        r = j * SUBLANES + SUBLANES // 2
        ef = jnp.exp2(-jnp.abs(bfj - beta_f[r - 1:r]))
        eb = jnp.exp2(-jnp.abs(bbj - bx_b[r:r + 1]))
        return qj * jnp.where(hi4, ef, eb), jnp.where(hi4, kbj * eb, kfj * ef)
    mb = (1 << lvl) // SUBLANES
    r = ((j // (2 * mb)) * 2 * mb + mb) * SUBLANES
    ref_f, ref_b = beta_f[r - 1:r], bx_b[r:r + 1]
    if (j // mb) & 1:
        return qj * jnp.exp2(bfj - ref_f), kbj * jnp.exp2(bbj - ref_b)
    return qj * jnp.exp2(ref_b - bbj), kfj * jnp.exp2(ref_f - bfj)


def _hgrn_kernel(lg_ref, hq_ref, hi_ref, hg_ref, gf_ref, gb_ref, ow_ref, tri_ref, mk_ref,
                 out_ref, o_scr, qi_scr, ut_scr, dec_scr, st_scr, *, layer, n_slots):
    c, nl, dh = HG_CHUNK, HG_LEVELS, HG_HEAD_DIM
    seq = hq_ref.shape[0]
    nc = seq // c

    lg = lg_ref[...].astype(F32)
    lbs = []
    for d in range(2):
        rows = lg[d * n_slots:(d + 1) * n_slots]
        e = jnp.exp(rows - jnp.max(rows, axis=0, keepdims=True))
        lbs.append(jnp.sum(e[:layer + 1], axis=0, keepdims=True) / jnp.sum(e, axis=0, keepdims=True))
    lb_f, lb_b = lbs

    pos = lax.broadcasted_iota(jnp.int32, (SUBLANES, dh), 0)
    odd = (pos & 1) != 0
    hi2 = (pos & 2) != 0
    hi4 = (pos & 4) != 0

    def split(x):
        hi = x.astype(BF16)
        return hi, (x - hi.astype(F32)).astype(BF16)

    def intra(ci, carry):
        r0 = pl.multiple_of(ci * c, c)
        q = _silu(hq_ref[pl.ds(r0, c), :].astype(F32))
        v = hi_ref[pl.ds(r0, c), :]
        f_f = lb_f + (1.0 - lb_f) * _sigmoid(gf_ref[pl.ds(r0, c), :])
        f_b = lb_b + (1.0 - lb_b) * _sigmoid(gb_ref[pl.ds(r0, c), :])
        k_f = 1.0 - f_f
        k_b = 1.0 - f_b
        lf_f = jnp.log2(f_f)
        lf_b = jnp.log2(f_b)
        pre = _dot(tri_ref[...], jnp.concatenate(split(lf_f) + split(lf_b), axis=1))
        beta_f = pre[:, :dh] + pre[:, dh:2 * dh]
        beta_b = pre[:, 2 * dh:3 * dh] + pre[:, 3 * dh:]
        bx_b = beta_b - lf_b
        tot_f = beta_f[c - 1:c]
        tot_b = beta_b[c - 1:c]

        a = mk_ref[0] * _dot_nt(q.astype(BF16), (k_f + k_b).astype(BF16))
        for lvl in range(nl):
            ops = [_level_operands(lvl, j, q, f_f, f_b, k_f, k_b, beta_f, bx_b, odd, hi2, hi4)
                   for j in range(c // SUBLANES)]
            qs = jnp.concatenate([o[0] for o in ops], axis=0)
            ks = jnp.concatenate([o[1] for o in ops], axis=0)
            a = a + mk_ref[lvl + 1] * _dot_nt(qs.astype(BF16), ks.astype(BF16))
        o_scr[pl.ds(r0, c), :] = _dot(a.astype(BF16), v)

        qi = jnp.concatenate([q * jnp.exp2(beta_f), q * jnp.exp2(tot_b - bx_b)], axis=1)
        qi_scr[pl.ds(r0, c), :] = qi.astype(BF16)
        ks = jnp.concatenate([k_f * jnp.exp2(tot_f - beta_f), k_b * jnp.exp2(bx_b)], axis=1)
        ut_scr[ci] = _dot_tn(v, ks.astype(BF16))
        dec = jnp.exp2(jnp.concatenate([tot_f, tot_b], axis=1))
        dec_scr[ci] = jnp.broadcast_to(dec, (8, 2 * dh))
        return carry

    lax.fori_loop(0, nc, intra, 0, unroll=8)

    def states(i, carry):
        st_f, st_b = carry
        cf = i
        cb = nc - 1 - i
        st_scr[cf, :, :dh] = st_f.astype(BF16)
        st_scr[cb, :, dh:] = st_b.astype(BF16)
        st_f = st_f * dec_scr[cf, 0:1, :dh] + ut_scr[cf, :, :dh]
        st_b = st_b * dec_scr[cb, 0:1, dh:] + ut_scr[cb, :, dh:]
        return st_f, st_b

    zero = jnp.zeros((dh, dh), F32)
    lax.fori_loop(0, nc, states, (zero, zero))

    ow = ow_ref[...].astype(F32)

    def finish(ci, carry):
        r0 = pl.multiple_of(ci * c, c)
        o = o_scr[pl.ds(r0, c), :] + _dot_nt(qi_scr[pl.ds(r0, c), :], st_scr[ci])
        y = o * lax.rsqrt(jnp.mean(o * o, axis=-1, keepdims=True) + EPS) * ow
        out_ref[pl.ds(r0, c), :] = (y * _silu(hg_ref[pl.ds(r0, c), :].astype(F32))).astype(out_ref.dtype)
        return carry

    lax.fori_loop(0, nc, finish, 0, unroll=8)


def _hgrn(main, gates, lb_logits, onorm_w, *, batch, seq, heads, layer, col_q, col_i, col_g):
    dh, c = HG_HEAD_DIM, HG_CHUNK
    n_slots = lb_logits.shape[1]
    lg = lb_logits.reshape(2 * n_slots, heads * dh)
    tri, masks = _hgrn_constants()
    nc = seq // c
    kern = functools.partial(_hgrn_kernel, layer=layer, n_slots=n_slots)
    return pl.pallas_call(
        kern,
        grid=(batch, heads),
        in_specs=[
            pl.BlockSpec((2 * n_slots, dh), lambda b, h: (0, h)),
            pl.BlockSpec((seq, dh), lambda b, h: (b, col_q + h)),
            pl.BlockSpec((seq, dh), lambda b, h: (b, col_i + h)),
            pl.BlockSpec((seq, dh), lambda b, h: (b, col_g + h)),
            pl.BlockSpec((seq, dh), lambda b, h: (b, h)),
            pl.BlockSpec((seq, dh), lambda b, h: (b, heads + h)),
            pl.BlockSpec((1, dh), lambda b, h: (0, 0)),
            pl.BlockSpec(tri.shape, lambda b, h: (0, 0)),
            pl.BlockSpec(masks.shape, lambda b, h: (0, 0, 0)),
        ],
        out_specs=pl.BlockSpec((seq, dh), lambda b, h: (b, h)),
        out_shape=jax.ShapeDtypeStruct((batch * seq, heads * dh), BF16),
        scratch_shapes=[
            pltpu.VMEM((seq, dh), F32),
            pltpu.VMEM((seq, 2 * dh), BF16),
            pltpu.VMEM((nc, dh, 2 * dh), F32),
            pltpu.VMEM((nc, 8, 2 * dh), F32),
            pltpu.VMEM((nc, dh, 2 * dh), BF16),
        ],
        compiler_params=_cparams(("parallel", "parallel")),
        name="hgrn2",
    )(lg, main, main, main, gates, gates, onorm_w.reshape(1, dh),
      jnp.asarray(tri, BF16), jnp.asarray(masks, F32))


def _rel_bucket_index(rel):
    nb = N_BUCKETS // 2
    max_exact = nb // 2
    ret = jnp.where(rel > 0, nb, 0)
    n = jnp.abs(rel)
    nf = jnp.maximum(n, 1).astype(jnp.float32)
    large = max_exact + (jnp.log(nf / max_exact) / math.log(MAX_DISTANCE / max_exact)
                         * (nb - max_exact)).astype(jnp.int32)
    large = jnp.minimum(large, nb - 1)
    return ret + jnp.where(n < max_exact, n, large)


def _bias_tiles_kernel(tbl_ref, bkt_ref, out_ref):
    h = pl.program_id(0)
    bkt = bkt_ref[...]
    acc = jnp.zeros(bkt.shape, F32)
    for cidx in range(N_BUCKETS):
        acc = jnp.where(bkt == cidx, tbl_ref[cidx, h], acc)
    out_ref[0] = acc * LOG2E


def _bias_tiles(rel_bias):
    t = BIAS_TILE
    assert t >= MAX_DISTANCE
    heads = rel_bias.shape[1]
    key = jnp.arange(t, dtype=jnp.int32)[:, None]
    qry = jnp.arange(t, dtype=jnp.int32)[None, :]
    rel = jnp.stack([t * d + key - qry for d in range(-2, 3)])
    bkt = _rel_bucket_index(rel).astype(jnp.int32)
    return pl.pallas_call(
        _bias_tiles_kernel,
        grid=(heads,),
        in_specs=[
            pl.BlockSpec(memory_space=pltpu.SMEM),
            pl.BlockSpec((5, t, t), lambda h: (0, 0, 0)),
        ],
        out_specs=pl.BlockSpec((1, 5, t, t), lambda h: (h, 0, 0, 0)),
        out_shape=jax.ShapeDtypeStruct((heads, 5, t, t), F32),
        compiler_params=_cparams(("arbitrary",)),
        name="bias_tiles",
    )(rel_bias.astype(F32), bkt)


def _diff_attn_kernel(lam_ref, q_ref, k_ref, v_ref, tiles_ref, sw_ref, out_ref, vaug_scr,
                      s0_scr, s1_scr, m0_scr, m1_scr, e0_scr, e1_scr, *, lam_init, tq):
    seq, d2 = q_ref.shape
    t = BIAS_TILE
    s_scr, m_scr, e_scr = (s0_scr, s1_scr), (m0_scr, m1_scr), (e0_scr, e1_scr)

    vaug_scr[:, :d2] = v_ref[...]
    vaug_scr[:, d2:] = jnp.ones((seq, d2), BF16)

    lv = lam_ref[...].astype(F32)
    lam = (jnp.exp(jnp.sum(lv[0:1] * lv[1:2], axis=-1, keepdims=True))
           - jnp.exp(jnp.sum(lv[2:3] * lv[3:4], axis=-1, keepdims=True)) + lam_init)

    c = DA_HEAD_DIM ** -0.5 * LOG2E
    lane = lax.broadcasted_iota(jnp.int32, (tq, d2), 1)
    first = lane < DA_HEAD_DIM
    sw = sw_ref[...].astype(F32) * (1.0 - lam_init)

    def scores(n, slot):
        r0 = pl.multiple_of(n * tq, tq)
        q = q_ref[pl.ds(r0, tq), :]
        zero = jnp.zeros_like(q)
        cols = []
        for ib in range(tq // t):
            d = [jnp.clip(jb - (n * (tq // t) + ib), -2, 2) + 2 for jb in range(seq // t)]
            cols.append(jnp.concatenate([tiles_ref[0, dj] for dj in d], axis=0))
        bias = jnp.concatenate(cols, axis=1)
        qq = jnp.concatenate([jnp.where(first, q, zero), jnp.where(first, zero, q)], axis=0)
        st = _dot_nt(k_ref[...], qq) * c
        st = jnp.concatenate([st[:, :tq] + bias, st[:, tq:] + bias], axis=1)
        s_scr[slot][...] = st
        m_scr[slot][...] = jnp.max(st, axis=0, keepdims=True)

    def numerators(slot):
        e_scr[slot][...] = jnp.exp2(s_scr[slot][...] - m_scr[slot][...]).astype(BF16)

    def values(n, slot):
        r0 = pl.multiple_of(n * tq, tq)
        oa = _dot_tn(e_scr[slot][...], vaug_scr[...])
        on = oa[:, :d2] / oa[:, d2:]
        o = on[:tq] - lam * on[tq:]
        y = o * lax.rsqrt(jnp.mean(o * o, axis=-1, keepdims=True) + EPS) * sw
        out_ref[pl.ds(r0, tq), :] = y.astype(out_ref.dtype)

    nq = seq // tq
    assert nq % 2 == 0 and nq >= 4
    scores(0, 0)
    scores(1, 1)
    numerators(0)

    def pair(i, carry):
        scores(2 * i + 2, 0)
        numerators(1)
        values(2 * i, 0)
        scores(2 * i + 3, 1)
        numerators(0)
        values(2 * i + 1, 1)
        return carry

    lax.fori_loop(0, nq // 2 - 1, pair, 0)
    numerators(1)
    values(nq - 2, 0)
    values(nq - 1, 1)


def _diff_attn(main, lam_vecs, tiles, subln_w, *, batch, seq, heads, lam_init, col_q, col_k, col_v, tq):
    d2 = 2 * DA_HEAD_DIM
    t = BIAS_TILE
    kern = functools.partial(_diff_attn_kernel, lam_init=lam_init, tq=tq)
    return pl.pallas_call(
        kern,
        grid=(heads, batch),
        in_specs=[
            pl.BlockSpec(lam_vecs.shape, lambda h, b: (0, 0)),
            pl.BlockSpec((seq, d2), lambda h, b: (b, col_q + h)),
            pl.BlockSpec((seq, d2), lambda h, b: (b, col_k + h)),
            pl.BlockSpec((seq, d2), lambda h, b: (b, col_v + h)),
            pl.BlockSpec((1, 5, t, t), lambda h, b: (h, 0, 0, 0)),
            pl.BlockSpec((1, d2), lambda h, b: (0, 0)),
        ],
        out_specs=pl.BlockSpec((seq, d2), lambda h, b: (b, h)),
        out_shape=jax.ShapeDtypeStruct((batch * seq, heads * d2), BF16),
        scratch_shapes=[pltpu.VMEM((seq, 2 * d2), BF16),
                        pltpu.VMEM((seq, 2 * tq), F32), pltpu.VMEM((seq, 2 * tq), F32),
                        pltpu.VMEM((1, 2 * tq), F32), pltpu.VMEM((1, 2 * tq), F32),
                        pltpu.VMEM((seq, 2 * tq), BF16), pltpu.VMEM((seq, 2 * tq), BF16)],
        compiler_params=_cparams(("parallel", "parallel")),
        name="diff_attn",
    )(lam_vecs, main, main, main, tiles, subln_w.reshape(1, d2))


def _out_proj_kernel(x_ref, a_ref, b_ref, wa_ref, wb_ref, o_ref):
    o_ref[...] = x_ref[...] + _dot(a_ref[...], wa_ref[...]) + _dot(b_ref[...], wb_ref[...])


def _out_proj(x, a, b, wa, wb, tm):
    m, d = x.shape
    ka, kb = a.shape[1], b.shape[1]
    resident = pl.Buffered(1)
    return pl.pallas_call(
        _out_proj_kernel,
        grid=(m // tm,),
        in_specs=[
            pl.BlockSpec((tm, d), lambda i: (i, 0)),
            pl.BlockSpec((tm, ka), lambda i: (i, 0)),
            pl.BlockSpec((tm, kb), lambda i: (i, 0)),
            pl.BlockSpec((ka, d), lambda i: (0, 0), pipeline_mode=resident),
            pl.BlockSpec((kb, d), lambda i: (0, 0), pipeline_mode=resident),
        ],
        out_specs=pl.BlockSpec((tm, d), lambda i: (i, 0)),
        out_shape=jax.ShapeDtypeStruct((m, d), F32),
        compiler_params=_cparams(("parallel",)),
        name="out_proj",
    )(x, a, b, wa, wb)


def _ffn_up_kernel(h_ref, nw_ref, wg_ref, wu_ref, o_ref, u_ref):
    @pl.when(pl.program_id(1) == 0)
    def _():
        x = h_ref[...]
        ms = jnp.mean(x * x, axis=-1, keepdims=True)
        u_ref[...] = (x * lax.rsqrt(ms + EPS) * nw_ref[...]).astype(BF16)

    u = u_ref[...]
    g = _dot(u, wg_ref[...])
    up = _dot(u, wu_ref[...])
    o_ref[...] = (_silu(g) * up).astype(o_ref.dtype)


def _ffn_up(h, nw, wg, wu, tm, tn):
    m, d = h.shape
    n = wg.shape[1]
    return pl.pallas_call(
        _ffn_up_kernel,
        grid=(m // tm, n // tn),
        in_specs=[
            pl.BlockSpec((tm, d), lambda i, j: (i, 0)),
            pl.BlockSpec((1, d), lambda i, j: (0, 0)),
            pl.BlockSpec((d, tn), lambda i, j: (0, j)),
            pl.BlockSpec((d, tn), lambda i, j: (0, j)),
        ],
        out_specs=pl.BlockSpec((tm, tn), lambda i, j: (i, j)),
        out_shape=jax.ShapeDtypeStruct((m, n), BF16),
        scratch_shapes=[pltpu.VMEM((tm, d), BF16)],
        compiler_params=_cparams(("parallel", "arbitrary")),
        name="ffn_up",
    )(h, nw.reshape(1, d), wg, wu)


def _ffn_down_kernel(a_ref, w_ref, h_ref, fw_ref, o_ref):
    y = h_ref[...] + _dot(a_ref[...], w_ref[...])
    ms = jnp.mean(y * y, axis=-1, keepdims=True)
    o_ref[...] = y * lax.rsqrt(ms + EPS) * fw_ref[...]


def _ffn_down(a, w, h, fw, tm):
    m, f = a.shape
    d = w.shape[1]
    return pl.pallas_call(
        _ffn_down_kernel,
        grid=(m // tm,),
        in_specs=[
            pl.BlockSpec((tm, f), lambda i: (i, 0)),
            pl.BlockSpec((f, d), lambda i: (0, 0), pipeline_mode=pl.Buffered(1)),
            pl.BlockSpec((tm, d), lambda i: (i, 0)),
            pl.BlockSpec((1, d), lambda i: (0, 0)),
        ],
        out_specs=pl.BlockSpec((tm, d), lambda i: (i, 0)),
        out_shape=jax.ShapeDtypeStruct((m, d), F32),
        compiler_params=_cparams(("parallel",)),
        name="ffn_down",
    )(a, w, h, fw.reshape(1, d))


def kernel(x, norm1_w, w_in, hg_lb_logits, hg_onorm_w, lambda_q1, lambda_k1, lambda_q2, lambda_k2,
           da_subln_w, rel_bias, w_out, norm2_w, w_gate, w_up, w_down, final_norm_w):
    batch, seq, d_model = x.shape
    depth = w_in.shape[0]
    hg_width = hg_lb_logits.shape[-1]
    da_width = d_model - hg_width
    hg_heads = hg_width // HG_HEAD_DIM
    da_heads = da_width // (2 * DA_HEAD_DIM)
    assert w_in.shape[2] == 5 * hg_width + 3 * da_width
    assert seq % HG_CHUNK == 0 and seq % BIAS_TILE == 0
    m = batch * seq
    blk = LANES

    tiles = _bias_tiles(rel_bias)
    h = x.reshape(m, d_model)
    for l in range(depth):
        wl = w_in[l]
        w_all = jnp.concatenate([wl[:, :2 * hg_width], wl[:, 4 * hg_width:], wl[:, 2 * hg_width:4 * hg_width]],
                                axis=1).astype(BF16)
        main, gates = _in_proj(h, norm1_w[l], w_all, 3 * hg_width + 3 * da_width, TILES["in_proj_m"], TILES["in_proj_n"])

        o_hg = _hgrn(main, gates, hg_lb_logits, hg_onorm_w[l], batch=batch, seq=seq, heads=hg_heads,
                     layer=l, col_q=0, col_i=hg_width // blk, col_g=2 * hg_width // blk)

        lam_init = 0.8 - 0.6 * math.exp(-0.3 * l)
        lam_vecs = jnp.stack([lambda_q1[l], lambda_k1[l], lambda_q2[l], lambda_k2[l]]).astype(F32)
        base = 3 * hg_width // blk
        o_da = _diff_attn(main, lam_vecs, tiles, da_subln_w[l], batch=batch, seq=seq, heads=da_heads,
                          lam_init=lam_init, col_q=base, col_k=base + da_width // blk,
                          col_v=base + 2 * da_width // blk, tq=TILES["attn_q"])

        wo = w_out[l].astype(BF16)
        h = _out_proj(h, o_hg, o_da, wo[:hg_width], wo[hg_width:], TILES["out_proj_m"])

        act = _ffn_up(h, norm2_w[l], w_gate[l].astype(BF16), w_up[l].astype(BF16),
                      TILES["ffn_up_m"], TILES["ffn_up_n"])
        last = l == depth - 1
        assert last, "final norm is fused into the last layer's down projection"
        h = _ffn_down(act, w_down[l].astype(BF16), h, final_norm_w, TILES["ffn_down_m"])
    return h.reshape(batch, seq, d_model)
```

```python
import functools
import math

import numpy as np
import jax
import jax.numpy as jnp
from jax import lax
from jax.experimental import pallas as pl
from jax.experimental.pallas import tpu as pltpu

F32 = jnp.float32
BF16 = jnp.bfloat16

EPS = 1e-6
LOG2E = math.log2(math.e)
HG_HEAD_DIM = 128
DA_HEAD_DIM = 64
N_BUCKETS = 32
MAX_DISTANCE = 128

LANES = 128
SUBLANES = 8
HG_CHUNK = 64
HG_LEVELS = 6
BIAS_TILE = 128
VMEM_LIMIT = 56 * 1024 * 1024

TILES = {
    "in_proj_m": 1024, "in_proj_n": 1024,
    "attn_q": 256,
    "out_proj_m": 512,
    "ffn_up_m": 2048, "ffn_up_n": 512,
    "ffn_down_m": 256,
}


def _cparams(sem, flags=None):
    return pltpu.CompilerParams(dimension_semantics=sem, vmem_limit_bytes=VMEM_LIMIT, flags=flags)


def _dot(a, b):
    return jnp.dot(a, b, preferred_element_type=F32)


def _dot_nt(a, b):
    return lax.dot_general(a, b, (((1,), (1,)), ((), ())), preferred_element_type=F32)


def _dot_tn(a, b):
    return lax.dot_general(a, b, (((0,), (0,)), ((), ())), preferred_element_type=F32)


def _sigmoid(x):
    return 1.0 / (1.0 + jnp.exp(-x))


def _silu(x):
    return x * _sigmoid(x)


def _in_proj_kernel(x_ref, nw_ref, w_ref, main_ref, gates_ref, u_ref, *, n_main):
    j = pl.program_id(1)

    @pl.when(j == 0)
    def _():
        x = x_ref[...].astype(F32)
        ms = jnp.mean(x * x, axis=-1, keepdims=True)
        u_ref[...] = (x * lax.rsqrt(ms + EPS) * nw_ref[...]).astype(BF16)

    @pl.when(j < n_main)
    def _():
        main_ref[...] = _dot(u_ref[...], w_ref[...]).astype(main_ref.dtype)

    @pl.when(j >= n_main)
    def _():
        gates_ref[...] = _dot(u_ref[...], w_ref[...])


def _in_proj(x, nw, w, n_main_cols, tm, tn):
    m, d = x.shape
    n = w.shape[1]
    n_main = n_main_cols // tn
    kern = functools.partial(_in_proj_kernel, n_main=n_main)
    return pl.pallas_call(
        kern,
        grid=(m // tm, n // tn),
        in_specs=[
            pl.BlockSpec((tm, d), lambda i, j: (i, 0)),
            pl.BlockSpec((1, d), lambda i, j: (0, 0)),
            pl.BlockSpec((d, tn), lambda i, j: (0, j)),
        ],
        out_specs=[
            pl.BlockSpec((tm, tn), lambda i, j: (i, jnp.minimum(j, n_main - 1))),
            pl.BlockSpec((tm, tn), lambda i, j: (i, jnp.maximum(j - n_main, 0))),
        ],
        out_shape=[
            jax.ShapeDtypeStruct((m, n_main_cols), BF16),
            jax.ShapeDtypeStruct((m, n - n_main_cols), F32),
        ],
        scratch_shapes=[pltpu.VMEM((tm, d), BF16)],
        compiler_params=_cparams(("parallel", "arbitrary")),
        name="in_proj",
    )(x, nw.reshape(1, d), w)


def _hgrn_constants():
    c, nl = HG_CHUNK, HG_LEVELS
    tri = np.tril(np.ones((c, c), np.float32))
    idx = np.arange(c)
    masks = np.zeros((nl + 1, c, c), np.float32)
    masks[0] = np.eye(c, dtype=np.float32)
    for lvl in range(nl):
        m = 1 << lvl
        same_pair = (idx[:, None] // (2 * m)) == (idx[None, :] // (2 * m))
        other_sibling = (idx[:, None] // m) != (idx[None, :] // m)
        masks[lvl + 1] = (same_pair & other_sibling).astype(np.float32)
    return tri, masks


def _level_operands(lvl, j, q, f_f, f_b, k_f, k_b, beta_f, bx_b, odd, hi2, hi4):
    sl = slice(j * SUBLANES, (j + 1) * SUBLANES)
    qj, ffj, fbj, kfj, kbj = q[sl], f_f[sl], f_b[sl], k_f[sl], k_b[sl]
    if lvl == 0:
        return qj * jnp.where(odd, ffj, fbj), jnp.where(odd, kbj, kfj)
    if lvl == 1:
        up_f, dn_f = pltpu.roll(ffj, SUBLANES - 1, axis=0), pltpu.roll(ffj, 1, axis=0)
        up_b, dn_b = pltpu.roll(fbj, SUBLANES - 1, axis=0), pltpu.roll(fbj, 1, axis=0)
        fq = jnp.where(hi2, jnp.where(odd, ffj * dn_f, ffj), jnp.where(odd, fbj, fbj * up_b))
        ks = jnp.where(hi2, jnp.where(odd, kbj * dn_b, kbj), jnp.where(odd, kfj, kfj * up_f))
        return qj * fq, ks
    bfj, bbj = beta_f[sl], bx_b[sl]
    if lvl == 2:
        r = j * SUBLANES + SUBLANES // 2
        ef = jnp.exp2(-jnp.abs(bfj - beta_f[r - 1:r]))
        eb = jnp.exp2(-jnp.abs(bbj - bx_b[r:r + 1]))
        return qj * jnp.where(hi4, ef, eb), jnp.where(hi4, kbj * eb, kfj * ef)
    mb = (1 << lvl) // SUBLANES
    r = ((j // (2 * mb)) * 2 * mb + mb) * SUBLANES
    ref_f, ref_b = beta_f[r - 1:r], bx_b[r:r + 1]
    if (j // mb) & 1:
        return qj * jnp.exp2(bfj - ref_f), kbj * jnp.exp2(bbj - ref_b)
    return qj * jnp.exp2(ref_b - bbj), kfj * jnp.exp2(ref_f - bfj)


def _hgrn_kernel(lg_ref, hq_ref, hi_ref, hg_ref, gf_ref, gb_ref, ow_ref, tri_ref, mk_ref,
                 out_ref, o_scr, qi_scr, ut_scr, dec_scr, st_scr, *, layer, n_slots):
    c, nl, dh = HG_CHUNK, HG_LEVELS, HG_HEAD_DIM
    seq = hq_ref.shape[0]
    nc = seq // c

    lg = lg_ref[...].astype(F32)
    lbs = []
    for d in range(2):
        rows = lg[d * n_slots:(d + 1) * n_slots]
        e = jnp.exp(rows - jnp.max(rows, axis=0, keepdims=True))
        lbs.append(jnp.sum(e[:layer + 1], axis=0, keepdims=True) / jnp.sum(e, axis=0, keepdims=True))
    lb_f, lb_b = lbs

    pos = lax.broadcasted_iota(jnp.int32, (SUBLANES, dh), 0)
    odd = (pos & 1) != 0
    hi2 = (pos & 2) != 0
    hi4 = (pos & 4) != 0

    def split(x):
        hi = x.astype(BF16)
        return hi, (x - hi.astype(F32)).astype(BF16)

    def intra(ci, carry):
        r0 = pl.multiple_of(ci * c, c)
        q = _silu(hq_ref[pl.ds(r0, c), :].astype(F32))
        v = hi_ref[pl.ds(r0, c), :]
        f_f = lb_f + (1.0 - lb_f) * _sigmoid(gf_ref[pl.ds(r0, c), :])
        f_b = lb_b + (1.0 - lb_b) * _sigmoid(gb_ref[pl.ds(r0, c), :])
        k_f = 1.0 - f_f
        k_b = 1.0 - f_b
        lf_f = jnp.log2(f_f)
        lf_b = jnp.log2(f_b)
        pre = _dot(tri_ref[...], jnp.concatenate(split(lf_f) + split(lf_b), axis=1))
        beta_f = pre[:, :dh] + pre[:, dh:2 * dh]
        beta_b = pre[:, 2 * dh:3 * dh] + pre[:, 3 * dh:]
        bx_b = beta_b - lf_b
        tot_f = beta_f[c - 1:c]
        tot_b = beta_b[c - 1:c]

        a = mk_ref[0] * _dot_nt(q.astype(BF16), (k_f + k_b).astype(BF16))
        for lvl in range(nl):
            ops = [_level_operands(lvl, j, q, f_f, f_b, k_f, k_b, beta_f, bx_b, odd, hi2, hi4)
                   for j in range(c // SUBLANES)]
            qs = jnp.concatenate([o[0] for o in ops], axis=0)
            ks = jnp.concatenate([o[1] for o in ops], axis=0)
            a = a + mk_ref[lvl + 1] * _dot_nt(qs.astype(BF16), ks.astype(BF16))
        o_scr[pl.ds(r0, c), :] = _dot(a.astype(BF16), v)

        qi = jnp.concatenate([q * jnp.exp2(beta_f), q * jnp.exp2(tot_b - bx_b)], axis=1)
        qi_scr[pl.ds(r0, c), :] = qi.astype(BF16)
        ks = jnp.concatenate([k_f * jnp.exp2(tot_f - beta_f), k_b * jnp.exp2(bx_b)], axis=1)
        ut_scr[ci] = _dot_tn(v, ks.astype(BF16))
        dec = jnp.exp2(jnp.concatenate([tot_f, tot_b], axis=1))
        dec_scr[ci] = jnp.broadcast_to(dec, (8, 2 * dh))
        return carry

    lax.fori_loop(0, nc, intra, 0, unroll=16)

    def states(i, carry):
        st_f, st_b = carry
        cf = i
        cb = nc - 1 - i
        st_scr[cf, :, :dh] = st_f.astype(BF16)
        st_scr[cb, :, dh:] = st_b.astype(BF16)
        st_f = st_f * dec_scr[cf, 0:1, :dh] + ut_scr[cf, :, :dh]
        st_b = st_b * dec_scr[cb, 0:1, dh:] + ut_scr[cb, :, dh:]
        return st_f, st_b

    zero = jnp.zeros((dh, dh), F32)
    lax.fori_loop(0, nc, states, (zero, zero))

    ow = ow_ref[...].astype(F32)

    def finish(ci, carry):
        r0 = pl.multiple_of(ci * c, c)
        o = o_scr[pl.ds(r0, c), :] + _dot_nt(qi_scr[pl.ds(r0, c), :], st_scr[ci])
        y = o * lax.rsqrt(jnp.mean(o * o, axis=-1, keepdims=True) + EPS) * ow
        out_ref[pl.ds(r0, c), :] = (y * _silu(hg_ref[pl.ds(r0, c), :].astype(F32))).astype(out_ref.dtype)
        return carry

    lax.fori_loop(0, nc, finish, 0, unroll=8)


def _hgrn(main, gates, lb_logits, onorm_w, *, batch, seq, heads, layer, col_q, col_i, col_g):
    dh, c = HG_HEAD_DIM, HG_CHUNK
    n_slots = lb_logits.shape[1]
    lg = lb_logits.reshape(2 * n_slots, heads * dh)
    tri, masks = _hgrn_constants()
    nc = seq // c
    kern = functools.partial(_hgrn_kernel, layer=layer, n_slots=n_slots)
    return pl.pallas_call(
        kern,
        grid=(batch, heads),
        in_specs=[
            pl.BlockSpec((2 * n_slots, dh), lambda b, h: (0, h)),
            pl.BlockSpec((seq, dh), lambda b, h: (b, col_q + h)),
            pl.BlockSpec((seq, dh), lambda b, h: (b, col_i + h)),
            pl.BlockSpec((seq, dh), lambda b, h: (b, col_g + h)),
            pl.BlockSpec((seq, dh), lambda b, h: (b, h)),
            pl.BlockSpec((seq, dh), lambda b, h: (b, heads + h)),
            pl.BlockSpec((1, dh), lambda b, h: (0, 0)),
            pl.BlockSpec(tri.shape, lambda b, h: (0, 0)),
            pl.BlockSpec(masks.shape, lambda b, h: (0, 0, 0)),
        ],
        out_specs=pl.BlockSpec((seq, dh), lambda b, h: (b, h)),
        out_shape=jax.ShapeDtypeStruct((batch * seq, heads * dh), BF16),
        scratch_shapes=[
            pltpu.VMEM((seq, dh), F32),
            pltpu.VMEM((seq, 2 * dh), BF16),
            pltpu.VMEM((nc, dh, 2 * dh), F32),
            pltpu.VMEM((nc, 8, 2 * dh), F32),
            pltpu.VMEM((nc, dh, 2 * dh), BF16),
        ],
        compiler_params=_cparams(("parallel", "parallel")),
        name="hgrn2",
    )(lg, main, main, main, gates, gates, onorm_w.reshape(1, dh),
      jnp.asarray(tri, BF16), jnp.asarray(masks, F32))


def _rel_bucket_index(rel):
    nb = N_BUCKETS // 2
    max_exact = nb // 2
    ret = jnp.where(rel > 0, nb, 0)
    n = jnp.abs(rel)
    nf = jnp.maximum(n, 1).astype(jnp.float32)
    large = max_exact + (jnp.log(nf / max_exact) / math.log(MAX_DISTANCE / max_exact)
                         * (nb - max_exact)).astype(jnp.int32)
    large = jnp.minimum(large, nb - 1)
    return ret + jnp.where(n < max_exact, n, large)


def _bias_tiles_kernel(tbl_ref, bkt_ref, out_ref):
    h = pl.program_id(0)
    bkt = bkt_ref[...]
    acc = jnp.zeros(bkt.shape, F32)
    for cidx in range(N_BUCKETS):
        acc = jnp.where(bkt == cidx, tbl_ref[cidx, h], acc)
    out_ref[0] = acc * LOG2E


def _bias_tiles(rel_bias):
    t = BIAS_TILE
    assert t >= MAX_DISTANCE
    heads = rel_bias.shape[1]
    key = jnp.arange(t, dtype=jnp.int32)[:, None]
    qry = jnp.arange(t, dtype=jnp.int32)[None, :]
    rel = jnp.stack([t * d + key - qry for d in range(-2, 3)])
    bkt = _rel_bucket_index(rel).astype(jnp.int32)
    return pl.pallas_call(
        _bias_tiles_kernel,
        grid=(heads,),
        in_specs=[
            pl.BlockSpec(memory_space=pltpu.SMEM),
            pl.BlockSpec((5, t, t), lambda h: (0, 0, 0)),
        ],
        out_specs=pl.BlockSpec((1, 5, t, t), lambda h: (h, 0, 0, 0)),
        out_shape=jax.ShapeDtypeStruct((heads, 5, t, t), F32),
        compiler_params=_cparams(("arbitrary",)),
        name="bias_tiles",
    )(rel_bias.astype(F32), bkt)


def _diff_attn_kernel(lam_ref, q_ref, k_ref, v_ref, tiles_ref, sw_ref, out_ref, vaug_scr,
                      s0_scr, s1_scr, m0_scr, m1_scr, e0_scr, e1_scr, *, lam_init, tq):
    seq, d2 = q_ref.shape
    t = BIAS_TILE
    s_scr, m_scr, e_scr = (s0_scr, s1_scr), (m0_scr, m1_scr), (e0_scr, e1_scr)

    vaug_scr[:, :d2] = v_ref[...]
    vaug_scr[:, d2:] = jnp.ones((seq, d2), BF16)

    lv = lam_ref[...].astype(F32)
    lam = (jnp.exp(jnp.sum(lv[0:1] * lv[1:2], axis=-1, keepdims=True))
           - jnp.exp(jnp.sum(lv[2:3] * lv[3:4], axis=-1, keepdims=True)) + lam_init)

    c = DA_HEAD_DIM ** -0.5 * LOG2E
    lane = lax.broadcasted_iota(jnp.int32, (tq, d2), 1)
    first = lane < DA_HEAD_DIM
    sw = sw_ref[...].astype(F32) * (1.0 - lam_init)

    def scores(n, slot):
        r0 = pl.multiple_of(n * tq, tq)
        q = q_ref[pl.ds(r0, tq), :]
        zero = jnp.zeros_like(q)
        cols = []
        for ib in range(tq // t):
            d = [jnp.clip(jb - (n * (tq // t) + ib), -2, 2) + 2 for jb in range(seq // t)]
            cols.append(jnp.concatenate([tiles_ref[0, dj] for dj in d], axis=0))
        bias = jnp.concatenate(cols, axis=1)
        qq = jnp.concatenate([jnp.where(first, q, zero), jnp.where(first, zero, q)], axis=0)
        st = _dot_nt(k_ref[...], qq) * c
        st = jnp.concatenate([st[:, :tq] + bias, st[:, tq:] + bias], axis=1)
        s_scr[slot][...] = st
        m_scr[slot][...] = jnp.max(st, axis=0, keepdims=True)

    def numerators(slot):
        e_scr[slot][...] = jnp.exp2(s_scr[slot][...] - m_scr[slot][...]).astype(BF16)

    def values(n, slot):
        r0 = pl.multiple_of(n * tq, tq)
        oa = _dot_tn(e_scr[slot][...], vaug_scr[...])
        on = oa[:, :d2] / oa[:, d2:]
        o = on[:tq] - lam * on[tq:]
        y = o * lax.rsqrt(jnp.mean(o * o, axis=-1, keepdims=True) + EPS) * sw
        out_ref[pl.ds(r0, tq), :] = y.astype(out_ref.dtype)

    nq = seq // tq
    assert nq % 2 == 0 and nq >= 4
    scores(0, 0)
    scores(1, 1)
    numerators(0)

    def pair(i, carry):
        scores(2 * i + 2, 0)
        numerators(1)
        values(2 * i, 0)
        scores(2 * i + 3, 1)
        numerators(0)
        values(2 * i + 1, 1)
        return carry

    lax.fori_loop(0, nq // 2 - 1, pair, 0)
    numerators(1)
    values(nq - 2, 0)
    values(nq - 1, 1)


def _diff_attn(main, lam_vecs, tiles, subln_w, *, batch, seq, heads, lam_init, col_q, col_k, col_v, tq):
    d2 = 2 * DA_HEAD_DIM
    t = BIAS_TILE
    kern = functools.partial(_diff_attn_kernel, lam_init=lam_init, tq=tq)
    return pl.pallas_call(
        kern,
        grid=(heads, batch),
        in_specs=[
            pl.BlockSpec(lam_vecs.shape, lambda h, b: (0, 0)),
            pl.BlockSpec((seq, d2), lambda h, b: (b, col_q + h)),
            pl.BlockSpec((seq, d2), lambda h, b: (b, col_k + h)),
            pl.BlockSpec((seq, d2), lambda h, b: (b, col_v + h)),
            pl.BlockSpec((1, 5, t, t), lambda h, b: (h, 0, 0, 0)),
            pl.BlockSpec((1, d2), lambda h, b: (0, 0)),
        ],
        out_specs=pl.BlockSpec((seq, d2), lambda h, b: (b, h)),
        out_shape=jax.ShapeDtypeStruct((batch * seq, heads * d2), BF16),
        scratch_shapes=[pltpu.VMEM((seq, 2 * d2), BF16),
                        pltpu.VMEM((seq, 2 * tq), F32), pltpu.VMEM((seq, 2 * tq), F32),
                        pltpu.VMEM((1, 2 * tq), F32), pltpu.VMEM((1, 2 * tq), F32),
                        pltpu.VMEM((seq, 2 * tq), BF16), pltpu.VMEM((seq, 2 * tq), BF16)],
        compiler_params=_cparams(("parallel", "parallel")),
        name="diff_attn",
    )(lam_vecs, main, main, main, tiles, subln_w.reshape(1, d2))


def _out_proj_kernel(x_ref, a_ref, b_ref, wa_ref, wb_ref, nw_ref, h_ref, u_ref):
    h = x_ref[...] + _dot(a_ref[...], wa_ref[...]) + _dot(b_ref[...], wb_ref[...])
    h_ref[...] = h
    ms = jnp.mean(h * h, axis=-1, keepdims=True)
    u_ref[...] = (h * lax.rsqrt(ms + EPS) * nw_ref[...]).astype(u_ref.dtype)


def _out_proj(x, a, b, wa, wb, nw, tm):
    m, d = x.shape
    ka, kb = a.shape[1], b.shape[1]
    resident = pl.Buffered(1)
    return pl.pallas_call(
        _out_proj_kernel,
        grid=(m // tm,),
        in_specs=[
            pl.BlockSpec((tm, d), lambda i: (i, 0)),
            pl.BlockSpec((tm, ka), lambda i: (i, 0)),
            pl.BlockSpec((tm, kb), lambda i: (i, 0)),
            pl.BlockSpec((ka, d), lambda i: (0, 0), pipeline_mode=resident),
            pl.BlockSpec((kb, d), lambda i: (0, 0), pipeline_mode=resident),
            pl.BlockSpec((1, d), lambda i: (0, 0)),
        ],
        out_specs=[pl.BlockSpec((tm, d), lambda i: (i, 0)), pl.BlockSpec((tm, d), lambda i: (i, 0))],
        out_shape=[jax.ShapeDtypeStruct((m, d), F32), jax.ShapeDtypeStruct((m, d), BF16)],
        compiler_params=_cparams(("parallel",)),
        name="out_proj",
    )(x, a, b, wa, wb, nw.reshape(1, d))


def _ffn_up_kernel(u_ref, wg_ref, wu_ref, o_ref):
    u = u_ref[...]
    g = _dot(u, wg_ref[...])
    up = _dot(u, wu_ref[...])
    o_ref[...] = (_silu(g) * up).astype(o_ref.dtype)


def _ffn_up(u, wg, wu, tm, tn):
    m, d = u.shape
    n = wg.shape[1]
    return pl.pallas_call(
        _ffn_up_kernel,
        grid=(m // tm, n // tn),
        in_specs=[
            pl.BlockSpec((tm, d), lambda i, j: (i, 0)),
            pl.BlockSpec((d, tn), lambda i, j: (0, j)),
            pl.BlockSpec((d, tn), lambda i, j: (0, j)),
        ],
        out_specs=pl.BlockSpec((tm, tn), lambda i, j: (i, j)),
        out_shape=jax.ShapeDtypeStruct((m, n), BF16),
        compiler_params=_cparams(("parallel", "parallel")),
        name="ffn_up",
    )(u, wg, wu)


def _ffn_down_kernel(a_ref, w_ref, h_ref, fw_ref, o_ref):
    y = h_ref[...] + _dot(a_ref[...], w_ref[...])
    ms = jnp.mean(y * y, axis=-1, keepdims=True)
    o_ref[...] = y * lax.rsqrt(ms + EPS) * fw_ref[...]


def _ffn_down(a, w, h, fw, tm):
    m, f = a.shape
    d = w.shape[1]
    return pl.pallas_call(
        _ffn_down_kernel,
        grid=(m // tm,),
        in_specs=[
            pl.BlockSpec((tm, f), lambda i: (i, 0)),
            pl.BlockSpec((f, d), lambda i: (0, 0), pipeline_mode=pl.Buffered(1)),
            pl.BlockSpec((tm, d), lambda i: (i, 0)),
            pl.BlockSpec((1, d), lambda i: (0, 0)),
        ],
        out_specs=pl.BlockSpec((tm, d), lambda i: (i, 0)),
        out_shape=jax.ShapeDtypeStruct((m, d), F32),
        compiler_params=_cparams(("parallel",)),
        name="ffn_down",
    )(a, w, h, fw.reshape(1, d))


def kernel(x, norm1_w, w_in, hg_lb_logits, hg_onorm_w, lambda_q1, lambda_k1, lambda_q2, lambda_k2,
           da_subln_w, rel_bias, w_out, norm2_w, w_gate, w_up, w_down, final_norm_w):
    batch, seq, d_model = x.shape
    depth = w_in.shape[0]
    hg_width = hg_lb_logits.shape[-1]
    da_width = d_model - hg_width
    hg_heads = hg_width // HG_HEAD_DIM
    da_heads = da_width // (2 * DA_HEAD_DIM)
    assert w_in.shape[2] == 5 * hg_width + 3 * da_width
    assert seq % HG_CHUNK == 0 and seq % BIAS_TILE == 0
    m = batch * seq
    blk = LANES

    tiles = _bias_tiles(rel_bias)
    h = x.reshape(m, d_model)
    for l in range(depth):
        wl = w_in[l]
        w_all = jnp.concatenate([wl[:, :2 * hg_width], wl[:, 4 * hg_width:], wl[:, 2 * hg_width:4 * hg_width]],
                                axis=1).astype(BF16)
        main, gates = _in_proj(h, norm1_w[l], w_all, 3 * hg_width + 3 * da_width, TILES["in_proj_m"], TILES["in_proj_n"])

        o_hg = _hgrn(main, gates, hg_lb_logits, hg_onorm_w[l], batch=batch, seq=seq, heads=hg_heads,
                     layer=l, col_q=0, col_i=hg_width // blk, col_g=2 * hg_width // blk)

        lam_init = 0.8 - 0.6 * math.exp(-0.3 * l)
        lam_vecs = jnp.stack([lambda_q1[l], lambda_k1[l], lambda_q2[l], lambda_k2[l]]).astype(F32)
        base = 3 * hg_width // blk
        o_da = _diff_attn(main, lam_vecs, tiles, da_subln_w[l], batch=batch, seq=seq, heads=da_heads,
                          lam_init=lam_init, col_q=base, col_k=base + da_width // blk,
                          col_v=base + 2 * da_width // blk, tq=TILES["attn_q"])

        wo = w_out[l].astype(BF16)
        h, u2 = _out_proj(h, o_hg, o_da, wo[:hg_width], wo[hg_width:], norm2_w[l], TILES["out_proj_m"])

        act = _ffn_up(u2, w_gate[l].astype(BF16), w_up[l].astype(BF16), TILES["ffn_up_m"], TILES["ffn_up_n"])
        last = l == depth - 1
        assert last, "final norm is fused into the last layer's down projection"
        h = _ffn_down(act, w_down[l].astype(BF16), h, final_norm_w, TILES["ffn_down_m"])
    return h.reshape(batch, seq, d_model)
```

```python
import functools
import math

import numpy as np
import jax
import jax.numpy as jnp
from jax import lax
from jax.experimental import pallas as pl
from jax.experimental.pallas import tpu as pltpu

F32 = jnp.float32
BF16 = jnp.bfloat16

EPS = 1e-6
LOG2E = math.log2(math.e)
HG_HEAD_DIM = 128
DA_HEAD_DIM = 64
N_BUCKETS = 32
MAX_DISTANCE = 128

LANES = 128
SUBLANES = 8
HG_CHUNK = 64
HG_LEVELS = 6
HG_GROUP = 2
HG_UNROLL = 16
BIAS_TILE = 128
VMEM_LIMIT = 56 * 1024 * 1024

TILES = {
    "in_proj_m": 1024, "in_proj_n": 1024,
    "attn_q": 256,
    "out_proj_m": 512,
    "ffn_up_m": 2048, "ffn_up_n": 512,
    "ffn_down_m": 256,
}


def _cparams(sem, flags=None):
    return pltpu.CompilerParams(dimension_semantics=sem, vmem_limit_bytes=VMEM_LIMIT, flags=flags)


def _dot(a, b):
    return jnp.dot(a, b, preferred_element_type=F32)


def _dot_nt(a, b):
    return lax.dot_general(a, b, (((1,), (1,)), ((), ())), preferred_element_type=F32)


def _dot_tn(a, b):
    return lax.dot_general(a, b, (((0,), (0,)), ((), ())), preferred_element_type=F32)


def _sigmoid(x):
    return 1.0 / (1.0 + jnp.exp(-x))


def _silu(x):
    return x * _sigmoid(x)


def _in_proj_kernel(x_ref, nw_ref, w_ref, main_ref, gates_ref, u_ref, *, n_main):
    j = pl.program_id(1)

    @pl.when(j == 0)
    def _():
        x = x_ref[...].astype(F32)
        ms = jnp.mean(x * x, axis=-1, keepdims=True)
        u_ref[...] = (x * lax.rsqrt(ms + EPS) * nw_ref[...]).astype(BF16)

    @pl.when(j < n_main)
    def _():
        main_ref[...] = _dot(u_ref[...], w_ref[...]).astype(main_ref.dtype)

    @pl.when(j >= n_main)
    def _():
        gates_ref[...] = _dot(u_ref[...], w_ref[...])


def _in_proj(x, nw, w, n_main_cols, tm, tn):
    m, d = x.shape
    n = w.shape[1]
    n_main = n_main_cols // tn
    kern = functools.partial(_in_proj_kernel, n_main=n_main)
    return pl.pallas_call(
        kern,
        grid=(m // tm, n // tn),
        in_specs=[
            pl.BlockSpec((tm, d), lambda i, j: (i, 0)),
            pl.BlockSpec((1, d), lambda i, j: (0, 0)),
            pl.BlockSpec((d, tn), lambda i, j: (0, j)),
        ],
        out_specs=[
            pl.BlockSpec((tm, tn), lambda i, j: (i, jnp.minimum(j, n_main - 1))),
            pl.BlockSpec((tm, tn), lambda i, j: (i, jnp.maximum(j - n_main, 0))),
        ],
        out_shape=[
            jax.ShapeDtypeStruct((m, n_main_cols), BF16),
            jax.ShapeDtypeStruct((m, n - n_main_cols), F32),
        ],
        scratch_shapes=[pltpu.VMEM((tm, d), BF16)],
        compiler_params=_cparams(("parallel", "arbitrary")),
        name="in_proj",
    )(x, nw.reshape(1, d), w)


def _hgrn_constants():
    c, nl, g = HG_CHUNK, HG_LEVELS, HG_GROUP
    idx = np.arange(g * c)
    same_chunk = (idx[:, None] // c) == (idx[None, :] // c)
    tri = (same_chunk & (idx[:, None] >= idx[None, :])).astype(np.float32)
    masks = np.zeros((nl + 1, g * c, g * c), np.float32)
    masks[0] = np.eye(g * c, dtype=np.float32)
    for lvl in range(nl):
        m = 1 << lvl
        same_pair = (idx[:, None] // (2 * m)) == (idx[None, :] // (2 * m))
        other_sibling = (idx[:, None] // m) != (idx[None, :] // m)
        masks[lvl + 1] = (same_pair & other_sibling).astype(np.float32)
    return tri, masks


def _level_operands(lvl, j, q, f_f, f_b, k_f, k_b, beta_f, bx_b, odd, hi2, hi4):
    sl = slice(j * SUBLANES, (j + 1) * SUBLANES)
    qj, ffj, fbj, kfj, kbj = q[sl], f_f[sl], f_b[sl], k_f[sl], k_b[sl]
    if lvl == 0:
        return qj * jnp.where(odd, ffj, fbj), jnp.where(odd, kbj, kfj)
    if lvl == 1:
        up_f, dn_f = pltpu.roll(ffj, SUBLANES - 1, axis=0), pltpu.roll(ffj, 1, axis=0)
        up_b, dn_b = pltpu.roll(fbj, SUBLANES - 1, axis=0), pltpu.roll(fbj, 1, axis=0)
        fq = jnp.where(hi2, jnp.where(odd, ffj * dn_f, ffj), jnp.where(odd, fbj, fbj * up_b))
        ks = jnp.where(hi2, jnp.where(odd, kbj * dn_b, kbj), jnp.where(odd, kfj, kfj * up_f))
        return qj * fq, ks
    bfj, bbj = beta_f[sl], bx_b[sl]
    if lvl == 2:
        r = j * SUBLANES + SUBLANES // 2
        ef = jnp.exp2(-jnp.abs(bfj - beta_f[r - 1:r]))
        eb = jnp.exp2(-jnp.abs(bbj - bx_b[r:r + 1]))
        return qj * jnp.where(hi4, ef, eb), jnp.where(hi4, kbj * eb, kfj * ef)
    mb = (1 << lvl) // SUBLANES
    r = ((j // (2 * mb)) * 2 * mb + mb) * SUBLANES
    ref_f, ref_b = beta_f[r - 1:r], bx_b[r:r + 1]
    if (j // mb) & 1:
        return qj * jnp.exp2(bfj - ref_f), kbj * jnp.exp2(bbj - ref_b)
    return qj * jnp.exp2(ref_b - bbj), kfj * jnp.exp2(ref_f - bfj)


def _hgrn_kernel(lg_ref, hq_ref, hi_ref, hg_ref, gf_ref, gb_ref, ow_ref, tri_ref, mk_ref,
                 out_ref, o_scr, qi_scr, ut_scr, dec_scr, st_scr, g_scr, *, layer, n_slots):
    c, nl, dh = HG_CHUNK, HG_LEVELS, HG_HEAD_DIM
    seq = hq_ref.shape[0]
    nc = seq // c

    lg = lg_ref[...].astype(F32)
    lbs = []
    for d in range(2):
        rows = lg[d * n_slots:(d + 1) * n_slots]
        e = jnp.exp(rows - jnp.max(rows, axis=0, keepdims=True))
        lbs.append(jnp.sum(e[:layer + 1], axis=0, keepdims=True) / jnp.sum(e, axis=0, keepdims=True))
    lb_f, lb_b = lbs

    pos = lax.broadcasted_iota(jnp.int32, (SUBLANES, dh), 0)
    odd = (pos & 1) != 0
    hi2 = (pos & 2) != 0
    hi4 = (pos & 4) != 0

    def split(x):
        hi = x.astype(BF16)
        return hi, (x - hi.astype(F32)).astype(BF16)

    gc = HG_GROUP * c
    ng = nc // HG_GROUP

    def gates(gi, carry):
        r0 = pl.multiple_of(gi * gc, gc)
        f_f = lb_f + (1.0 - lb_f) * _sigmoid(gf_ref[pl.ds(r0, gc), :])
        f_b = lb_b + (1.0 - lb_b) * _sigmoid(gb_ref[pl.ds(r0, gc), :])
        lf_f = jnp.log2(f_f)
        lf_b = jnp.log2(f_b)
        pre = _dot(tri_ref[...], jnp.concatenate(split(lf_f) + split(lf_b), axis=1))
        g_scr[0, pl.ds(r0, gc), :] = f_f
        g_scr[1, pl.ds(r0, gc), :] = f_b
        g_scr[2, pl.ds(r0, gc), :] = pre[:, :dh] + pre[:, dh:2 * dh]
        g_scr[3, pl.ds(r0, gc), :] = pre[:, 2 * dh:3 * dh] + pre[:, 3 * dh:] - lf_b
        return carry

    lax.fori_loop(0, ng, gates, 0, unroll=HG_UNROLL // HG_GROUP)

    def per_chunk_rows(x, row):
        return jnp.concatenate([jnp.broadcast_to(x[g * c + row:g * c + row + 1], (c, dh))
                                for g in range(HG_GROUP)], axis=0)

    def intra(gi, carry):
        r0 = pl.multiple_of(gi * gc, gc)
        q = _silu(hq_ref[pl.ds(r0, gc), :].astype(F32))
        v = hi_ref[pl.ds(r0, gc), :]
        f_f = g_scr[0, pl.ds(r0, gc), :]
        f_b = g_scr[1, pl.ds(r0, gc), :]
        beta_f = g_scr[2, pl.ds(r0, gc), :]
        bx_b = g_scr[3, pl.ds(r0, gc), :]
        k_f = 1.0 - f_f
        k_b = 1.0 - f_b
        tot_f = per_chunk_rows(beta_f, c - 1)
        tot_b = per_chunk_rows(bx_b + jnp.log2(f_b), c - 1)

        a = mk_ref[0] * _dot_nt(q.astype(BF16), (k_f + k_b).astype(BF16))
        for lvl in range(nl):
            ops = [_level_operands(lvl, j, q, f_f, f_b, k_f, k_b, beta_f, bx_b, odd, hi2, hi4)
                   for j in range(gc // SUBLANES)]
            qs = jnp.concatenate([o[0] for o in ops], axis=0)
            ks = jnp.concatenate([o[1] for o in ops], axis=0)
            a = a + mk_ref[lvl + 1] * _dot_nt(qs.astype(BF16), ks.astype(BF16))
        o_scr[pl.ds(r0, gc), :] = _dot(a.astype(BF16), v)

        qi = jnp.concatenate([q * jnp.exp2(beta_f), q * jnp.exp2(tot_b - bx_b)], axis=1)
        qi_scr[pl.ds(r0, gc), :] = qi.astype(BF16)
        ks = jnp.concatenate([k_f * jnp.exp2(tot_f - beta_f), k_b * jnp.exp2(bx_b)], axis=1).astype(BF16)
        zero = jnp.zeros((c, dh), BF16)
        vd = jnp.concatenate([jnp.concatenate([v[g * c:(g + 1) * c] if h == g else zero
                                               for h in range(HG_GROUP)], axis=1)
                              for g in range(HG_GROUP)], axis=0)
        ut = _dot_tn(vd, ks)
        dec = jnp.exp2(jnp.concatenate([tot_f, tot_b], axis=1))
        for g in range(HG_GROUP):
            ut_scr[gi * HG_GROUP + g] = ut[g * dh:(g + 1) * dh]
            dec_scr[gi * HG_GROUP + g] = dec[g * c:g * c + 8]
        return carry

    lax.fori_loop(0, ng, intra, 0, unroll=HG_UNROLL // HG_GROUP)

    def states(i, carry):
        st_f, st_b = carry
        cf = i
        cb = nc - 1 - i
        st_scr[cf, :, :dh] = st_f.astype(BF16)
        st_scr[cb, :, dh:] = st_b.astype(BF16)
        st_f = st_f * dec_scr[cf, 0:1, :dh] + ut_scr[cf, :, :dh]
        st_b = st_b * dec_scr[cb, 0:1, dh:] + ut_scr[cb, :, dh:]
        return st_f, st_b

    zero = jnp.zeros((dh, dh), F32)
    lax.fori_loop(0, nc, states, (zero, zero))

    ow = ow_ref[...].astype(F32)

    def finish(ci, carry):
        r0 = pl.multiple_of(ci * c, c)
        o = o_scr[pl.ds(r0, c), :] + _dot_nt(qi_scr[pl.ds(r0, c), :], st_scr[ci])
        y = o * lax.rsqrt(jnp.mean(o * o, axis=-1, keepdims=True) + EPS) * ow
        out_ref[pl.ds(r0, c), :] = (y * _silu(hg_ref[pl.ds(r0, c), :].astype(F32))).astype(out_ref.dtype)
        return carry

    lax.fori_loop(0, nc, finish, 0, unroll=HG_UNROLL)


def _hgrn(main, gates, lb_logits, onorm_w, *, batch, seq, heads, layer, col_q, col_i, col_g):
    dh, c = HG_HEAD_DIM, HG_CHUNK
    n_slots = lb_logits.shape[1]
    lg = lb_logits.reshape(2 * n_slots, heads * dh)
    tri, masks = _hgrn_constants()
    nc = seq // c
    kern = functools.partial(_hgrn_kernel, layer=layer, n_slots=n_slots)
    return pl.pallas_call(
        kern,
        grid=(batch, heads),
        in_specs=[
            pl.BlockSpec((2 * n_slots, dh), lambda b, h: (0, h)),
            pl.BlockSpec((seq, dh), lambda b, h: (b, col_q + h)),
            pl.BlockSpec((seq, dh), lambda b, h: (b, col_i + h)),
            pl.BlockSpec((seq, dh), lambda b, h: (b, col_g + h)),
            pl.BlockSpec((seq, dh), lambda b, h: (b, h)),
            pl.BlockSpec((seq, dh), lambda b, h: (b, heads + h)),
            pl.BlockSpec((1, dh), lambda b, h: (0, 0)),
            pl.BlockSpec(tri.shape, lambda b, h: (0, 0)),
            pl.BlockSpec(masks.shape, lambda b, h: (0, 0, 0)),
        ],
        out_specs=pl.BlockSpec((seq, dh), lambda b, h: (b, h)),
        out_shape=jax.ShapeDtypeStruct((batch * seq, heads * dh), BF16),
        scratch_shapes=[
            pltpu.VMEM((seq, dh), F32),
            pltpu.VMEM((seq, 2 * dh), BF16),
            pltpu.VMEM((nc, dh, 2 * dh), F32),
            pltpu.VMEM((nc, 8, 2 * dh), F32),
            pltpu.VMEM((nc, dh, 2 * dh), BF16),
            pltpu.VMEM((4, seq, dh), F32),
        ],
        compiler_params=_cparams(("parallel", "parallel")),
        name="hgrn2",
    )(lg, main, main, main, gates, gates, onorm_w.reshape(1, dh),
      jnp.asarray(tri, BF16), jnp.asarray(masks, F32))


def _rel_bucket_index(rel):
    nb = N_BUCKETS // 2
    max_exact = nb // 2
    ret = jnp.where(rel > 0, nb, 0)
    n = jnp.abs(rel)
    nf = jnp.maximum(n, 1).astype(jnp.float32)
    large = max_exact + (jnp.log(nf / max_exact) / math.log(MAX_DISTANCE / max_exact)
                         * (nb - max_exact)).astype(jnp.int32)
    large = jnp.minimum(large, nb - 1)
    return ret + jnp.where(n < max_exact, n, large)


def _bias_tiles_kernel(tbl_ref, bkt_ref, out_ref):
    h = pl.program_id(0)
    bkt = bkt_ref[...]
    acc = jnp.zeros(bkt.shape, F32)
    for cidx in range(N_BUCKETS):
        acc = jnp.where(bkt == cidx, tbl_ref[cidx, h], acc)
    out_ref[0] = acc * LOG2E


def _bias_tiles(rel_bias):
    t = BIAS_TILE
    assert t >= MAX_DISTANCE
    heads = rel_bias.shape[1]
    key = jnp.arange(t, dtype=jnp.int32)[:, None]
    qry = jnp.arange(t, dtype=jnp.int32)[None, :]
    rel = jnp.stack([t * d + key - qry for d in range(-2, 3)])
    bkt = _rel_bucket_index(rel).astype(jnp.int32)
    return pl.pallas_call(
        _bias_tiles_kernel,
        grid=(heads,),
        in_specs=[
            pl.BlockSpec(memory_space=pltpu.SMEM),
            pl.BlockSpec((5, t, t), lambda h: (0, 0, 0)),
        ],
        out_specs=pl.BlockSpec((1, 5, t, t), lambda h: (h, 0, 0, 0)),
        out_shape=jax.ShapeDtypeStruct((heads, 5, t, t), F32),
        compiler_params=_cparams(("arbitrary",)),
        name="bias_tiles",
    )(rel_bias.astype(F32), bkt)


def _diff_attn_kernel(lam_ref, q_ref, k_ref, v_ref, tiles_ref, sw_ref, out_ref, vaug_scr,
                      s0_scr, s1_scr, m0_scr, m1_scr, e0_scr, e1_scr, *, lam_init, tq):
    seq, d2 = q_ref.shape
    t = BIAS_TILE
    s_scr, m_scr, e_scr = (s0_scr, s1_scr), (m0_scr, m1_scr), (e0_scr, e1_scr)

    vaug_scr[:, :d2] = v_ref[...]
    vaug_scr[:, d2:] = jnp.ones((seq, d2), BF16)

    lv = lam_ref[...].astype(F32)
    lam = (jnp.exp(jnp.sum(lv[0:1] * lv[1:2], axis=-1, keepdims=True))
           - jnp.exp(jnp.sum(lv[2:3] * lv[3:4], axis=-1, keepdims=True)) + lam_init)

    c = DA_HEAD_DIM ** -0.5 * LOG2E
    lane = lax.broadcasted_iota(jnp.int32, (tq, d2), 1)
    first = lane < DA_HEAD_DIM
    sw = sw_ref[...].astype(F32) * (1.0 - lam_init)

    def scores(n, slot):
        r0 = n * tq if isinstance(n, int) else pl.multiple_of(n * tq, tq)
        q = q_ref[pl.ds(r0, tq), :]
        zero = jnp.zeros_like(q)
        cols = []
        for ib in range(tq // t):
            d = [jnp.clip(jb - (n * (tq // t) + ib), -2, 2) + 2 for jb in range(seq // t)]
            cols.append(jnp.concatenate([tiles_ref[0, dj] for dj in d], axis=0))
        bias = jnp.concatenate(cols, axis=1)
        qq = jnp.concatenate([jnp.where(first, q, zero), jnp.where(first, zero, q)], axis=0)
        st = _dot_nt(k_ref[...], qq) * c
        st = jnp.concatenate([st[:, :tq] + bias, st[:, tq:] + bias], axis=1)
        s_scr[slot][...] = st
        m_scr[slot][...] = jnp.max(st, axis=0, keepdims=True)

    def numerators(slot):
        e_scr[slot][...] = jnp.exp2(s_scr[slot][...] - m_scr[slot][...]).astype(BF16)

    def values(n, slot):
        r0 = n * tq if isinstance(n, int) else pl.multiple_of(n * tq, tq)
        oa = _dot_tn(e_scr[slot][...], vaug_scr[...])
        on = oa[:, :d2] / oa[:, d2:]
        o = on[:tq] - lam * on[tq:]
        y = o * lax.rsqrt(jnp.mean(o * o, axis=-1, keepdims=True) + EPS) * sw
        out_ref[pl.ds(r0, tq), :] = y.astype(out_ref.dtype)

    nq = seq // tq
    assert nq % 2 == 0 and nq >= 4
    scores(0, 0)
    scores(1, 1)
    numerators(0)

    def pair(i, carry):
        scores(2 * i + 2, 0)
        numerators(1)
        values(2 * i, 0)
        scores(2 * i + 3, 1)
        numerators(0)
        values(2 * i + 1, 1)
        return carry

    lax.fori_loop(0, nq // 2 - 1, pair, 0)
    numerators(1)
    values(nq - 2, 0)
    values(nq - 1, 1)


def _diff_attn(main, lam_vecs, tiles, subln_w, *, batch, seq, heads, lam_init, col_q, col_k, col_v, tq):
    d2 = 2 * DA_HEAD_DIM
    t = BIAS_TILE
    kern = functools.partial(_diff_attn_kernel, lam_init=lam_init, tq=tq)
    return pl.pallas_call(
        kern,
        grid=(heads, batch),
        in_specs=[
            pl.BlockSpec(lam_vecs.shape, lambda h, b: (0, 0)),
            pl.BlockSpec((seq, d2), lambda h, b: (b, col_q + h)),
            pl.BlockSpec((seq, d2), lambda h, b: (b, col_k + h)),
            pl.BlockSpec((seq, d2), lambda h, b: (b, col_v + h)),
            pl.BlockSpec((1, 5, t, t), lambda h, b: (h, 0, 0, 0)),
            pl.BlockSpec((1, d2), lambda h, b: (0, 0)),
        ],
        out_specs=pl.BlockSpec((seq, d2), lambda h, b: (b, h)),
        out_shape=jax.ShapeDtypeStruct((batch * seq, heads * d2), BF16),
        scratch_shapes=[pltpu.VMEM((seq, 2 * d2), BF16),
                        pltpu.VMEM((seq, 2 * tq), F32), pltpu.VMEM((seq, 2 * tq), F32),
                        pltpu.VMEM((1, 2 * tq), F32), pltpu.VMEM((1, 2 * tq), F32),
                        pltpu.VMEM((seq, 2 * tq), BF16), pltpu.VMEM((seq, 2 * tq), BF16)],
        compiler_params=_cparams(("parallel", "parallel")),
        name="diff_attn",
    )(lam_vecs, main, main, main, tiles, subln_w.reshape(1, d2))


def _out_proj_kernel(x_ref, a_ref, b_ref, wa_ref, wb_ref, nw_ref, h_ref, u_ref):
    h = x_ref[...] + _dot(a_ref[...], wa_ref[...]) + _dot(b_ref[...], wb_ref[...])
    h_ref[...] = h
    ms = jnp.mean(h * h, axis=-1, keepdims=True)
    u_ref[...] = (h * lax.rsqrt(ms + EPS) * nw_ref[...]).astype(u_ref.dtype)


def _out_proj(x, a, b, w, nw, tm):
    m, d = x.shape
    ka, kb = a.shape[1], b.shape[1]
    assert ka == kb and w.shape == (ka + kb, d)
    resident = pl.Buffered(1)
    return pl.pallas_call(
        _out_proj_kernel,
        grid=(m // tm,),
        in_specs=[
            pl.BlockSpec((tm, d), lambda i: (i, 0)),
            pl.BlockSpec((tm, ka), lambda i: (i, 0)),
            pl.BlockSpec((tm, kb), lambda i: (i, 0)),
            pl.BlockSpec((ka, d), lambda i: (0, 0), pipeline_mode=resident),
            pl.BlockSpec((kb, d), lambda i: (1, 0), pipeline_mode=resident),
            pl.BlockSpec((1, d), lambda i: (0, 0)),
        ],
        out_specs=[pl.BlockSpec((tm, d), lambda i: (i, 0)), pl.BlockSpec((tm, d), lambda i: (i, 0))],
        out_shape=[jax.ShapeDtypeStruct((m, d), F32), jax.ShapeDtypeStruct((m, d), BF16)],
        compiler_params=_cparams(("parallel",)),
        name="out_proj",
    )(x, a, b, w, w, nw.reshape(1, d))


def _ffn_up_kernel(u_ref, wg_ref, wu_ref, o_ref):
    u = u_ref[...]
    g = _dot(u, wg_ref[...])
    up = _dot(u, wu_ref[...])
    o_ref[...] = (_silu(g) * up).astype(o_ref.dtype)


def _ffn_up(u, wg, wu, tm, tn):
    m, d = u.shape
    n = wg.shape[1]
    return pl.pallas_call(
        _ffn_up_kernel,
        grid=(m // tm, n // tn),
        in_specs=[
            pl.BlockSpec((tm, d), lambda i, j: (i, 0)),
            pl.BlockSpec((d, tn), lambda i, j: (0, j)),
            pl.BlockSpec((d, tn), lambda i, j: (0, j)),
        ],
        out_specs=pl.BlockSpec((tm, tn), lambda i, j: (i, j)),
        out_shape=jax.ShapeDtypeStruct((m, n), BF16),
        compiler_params=_cparams(("parallel", "parallel")),
        name="ffn_up",
    )(u, wg, wu)


def _ffn_down_kernel(a_ref, w_ref, h_ref, fw_ref, o_ref):
    y = h_ref[...] + _dot(a_ref[...], w_ref[...])
    ms = jnp.mean(y * y, axis=-1, keepdims=True)
    o_ref[...] = y * lax.rsqrt(ms + EPS) * fw_ref[...]


def _ffn_down(a, w, h, fw, tm):
    m, f = a.shape
    d = w.shape[1]
    return pl.pallas_call(
        _ffn_down_kernel,
        grid=(m // tm,),
        in_specs=[
            pl.BlockSpec((tm, f), lambda i: (i, 0)),
            pl.BlockSpec((f, d), lambda i: (0, 0), pipeline_mode=pl.Buffered(1)),
            pl.BlockSpec((tm, d), lambda i: (i, 0)),
            pl.BlockSpec((1, d), lambda i: (0, 0)),
        ],
        out_specs=pl.BlockSpec((tm, d), lambda i: (i, 0)),
        out_shape=jax.ShapeDtypeStruct((m, d), F32),
        compiler_params=_cparams(("parallel",)),
        name="ffn_down",
    )(a, w, h, fw.reshape(1, d))


def kernel(x, norm1_w, w_in, hg_lb_logits, hg_onorm_w, lambda_q1, lambda_k1, lambda_q2, lambda_k2,
           da_subln_w, rel_bias, w_out, norm2_w, w_gate, w_up, w_down, final_norm_w):
    batch, seq, d_model = x.shape
    depth = w_in.shape[0]
    hg_width = hg_lb_logits.shape[-1]
    da_width = d_model - hg_width
    hg_heads = hg_width // HG_HEAD_DIM
    da_heads = da_width // (2 * DA_HEAD_DIM)
    assert w_in.shape[2] == 5 * hg_width + 3 * da_width
    assert seq % HG_CHUNK == 0 and seq % BIAS_TILE == 0
    m = batch * seq
    blk = LANES

    tiles = _bias_tiles(rel_bias)
    h = x.reshape(m, d_model)
    for l in range(depth):
        wl = w_in[l]
        w_all = jnp.concatenate([wl[:, :2 * hg_width], wl[:, 4 * hg_width:], wl[:, 2 * hg_width:4 * hg_width]],
                                axis=1).astype(BF16)
        main, gates = _in_proj(h, norm1_w[l], w_all, 3 * hg_width + 3 * da_width, TILES["in_proj_m"], TILES["in_proj_n"])

        o_hg = _hgrn(main, gates, hg_lb_logits, hg_onorm_w[l], batch=batch, seq=seq, heads=hg_heads,
                     layer=l, col_q=0, col_i=hg_width // blk, col_g=2 * hg_width // blk)

        lam_init = 0.8 - 0.6 * math.exp(-0.3 * l)
        lam_vecs = jnp.stack([lambda_q1[l], lambda_k1[l], lambda_q2[l], lambda_k2[l]]).astype(F32)
        base = 3 * hg_width // blk
        o_da = _diff_attn(main, lam_vecs, tiles, da_subln_w[l], batch=batch, seq=seq, heads=da_heads,
                          lam_init=lam_init, col_q=base, col_k=base + da_width // blk,
                          col_v=base + 2 * da_width // blk, tq=TILES["attn_q"])

        h, u2 = _out_proj(h, o_hg, o_da, w_out[l].astype(BF16), norm2_w[l], TILES["out_proj_m"])

        act = _ffn_up(u2, w_gate[l].astype(BF16), w_up[l].astype(BF16), TILES["ffn_up_m"], TILES["ffn_up_n"])
        last = l == depth - 1
        assert last, "final norm is fused into the last layer's down projection"
        h = _ffn_down(act, w_down[l].astype(BF16), h, final_norm_w, TILES["ffn_down_m"])
    return h.reshape(batch, seq, d_model)
```

```python
import functools
import math

import numpy as np
import jax
import jax.numpy as jnp
from jax import lax
from jax.experimental import pallas as pl
from jax.experimental.pallas import tpu as pltpu

F32 = jnp.float32
BF16 = jnp.bfloat16

EPS = 1e-6
LOG2E = math.log2(math.e)
HG_HEAD_DIM = 128
DA_HEAD_DIM = 64
N_BUCKETS = 32
MAX_DISTANCE = 128

LANES = 128
SUBLANES = 8
HG_CHUNK = 64
HG_LEVELS = 6
HG_GROUP = 2
HG_UNROLL = 16
BIAS_TILE = 128
VMEM_LIMIT = 56 * 1024 * 1024

TILES = {
    "in_proj_m": 1024, "in_proj_n": 1024,
    "attn_q": 256,
    "out_proj_m": 512,
    "ffn_up_m": 2048, "ffn_up_n": 512,
    "ffn_down_m": 256,
}


def _cparams(sem, flags=None):
    return pltpu.CompilerParams(dimension_semantics=sem, vmem_limit_bytes=VMEM_LIMIT, flags=flags)


def _dot(a, b):
    return jnp.dot(a, b, preferred_element_type=F32)


def _dot_nt(a, b):
    return lax.dot_general(a, b, (((1,), (1,)), ((), ())), preferred_element_type=F32)


def _dot_tn(a, b):
    return lax.dot_general(a, b, (((0,), (0,)), ((), ())), preferred_element_type=F32)


def _sigmoid(x):
    return 1.0 / (1.0 + jnp.exp(-x))


def _silu(x):
    return x * _sigmoid(x)


def _in_proj_kernel(x_ref, nw_ref, w_ref, main_ref, gates_ref, u_ref, *, n_main):
    j = pl.program_id(1)

    @pl.when(j == 0)
    def _():
        x = x_ref[...].astype(F32)
        ms = jnp.mean(x * x, axis=-1, keepdims=True)
        u_ref[...] = (x * lax.rsqrt(ms + EPS) * nw_ref[...]).astype(BF16)

    @pl.when(j < n_main)
    def _():
        main_ref[...] = _dot(u_ref[...], w_ref[...]).astype(main_ref.dtype)

    @pl.when(j >= n_main)
    def _():
        gates_ref[...] = _dot(u_ref[...], w_ref[...])


def _in_proj(x, nw, w, gate_lo, gate_hi, tm, tn):
    m, d = x.shape
    n = w.shape[1]
    assert gate_lo % tn == 0 and gate_hi % tn == 0
    n_before, n_gate = gate_lo // tn, (gate_hi - gate_lo) // tn
    n_main = n // tn - n_gate
    n_main_cols = n_main * tn

    def w_block(i, j):
        return 0, jnp.where(j < n_before, j, jnp.where(j < n_main, j + n_gate, j - (n_main - n_before)))

    kern = functools.partial(_in_proj_kernel, n_main=n_main)
    return pl.pallas_call(
        kern,
        grid=(m // tm, n // tn),
        in_specs=[
            pl.BlockSpec((tm, d), lambda i, j: (i, 0)),
            pl.BlockSpec((1, d), lambda i, j: (0, 0)),
            pl.BlockSpec((d, tn), w_block),
        ],
        out_specs=[
            pl.BlockSpec((tm, tn), lambda i, j: (i, jnp.minimum(j, n_main - 1))),
            pl.BlockSpec((tm, tn), lambda i, j: (i, jnp.maximum(j - n_main, 0))),
        ],
        out_shape=[
            jax.ShapeDtypeStruct((m, n_main_cols), BF16),
            jax.ShapeDtypeStruct((m, n - n_main_cols), F32),
        ],
        scratch_shapes=[pltpu.VMEM((tm, d), BF16)],
        compiler_params=_cparams(("parallel", "arbitrary")),
        name="in_proj",
    )(x, nw.reshape(1, d), w)


def _hgrn_constants():
    c, nl, g = HG_CHUNK, HG_LEVELS, HG_GROUP
    idx = np.arange(g * c)
    same_chunk = (idx[:, None] // c) == (idx[None, :] // c)
    tri = (same_chunk & (idx[:, None] >= idx[None, :])).astype(np.float32)
    masks = np.zeros((nl + 1, g * c, g * c), np.float32)
    masks[0] = np.eye(g * c, dtype=np.float32)
    for lvl in range(nl):
        m = 1 << lvl
        same_pair = (idx[:, None] // (2 * m)) == (idx[None, :] // (2 * m))
        other_sibling = (idx[:, None] // m) != (idx[None, :] // m)
        masks[lvl + 1] = (same_pair & other_sibling).astype(np.float32)
    return tri, masks


def _level_operands(lvl, j, q, f_f, f_b, k_f, k_b, beta_f, bx_b, odd, hi2, hi4):
    sl = slice(j * SUBLANES, (j + 1) * SUBLANES)
    qj, ffj, fbj, kfj, kbj = q[sl], f_f[sl], f_b[sl], k_f[sl], k_b[sl]
    if lvl == 0:
        return qj * jnp.where(odd, ffj, fbj), jnp.where(odd, kbj, kfj)
    if lvl == 1:
        up_f, dn_f = pltpu.roll(ffj, SUBLANES - 1, axis=0), pltpu.roll(ffj, 1, axis=0)
        up_b, dn_b = pltpu.roll(fbj, SUBLANES - 1, axis=0), pltpu.roll(fbj, 1, axis=0)
        fq = jnp.where(hi2, jnp.where(odd, ffj * dn_f, ffj), jnp.where(odd, fbj, fbj * up_b))
        ks = jnp.where(hi2, jnp.where(odd, kbj * dn_b, kbj), jnp.where(odd, kfj, kfj * up_f))
        return qj * fq, ks
    bfj, bbj = beta_f[sl], bx_b[sl]
    if lvl == 2:
        r = j * SUBLANES + SUBLANES // 2
        ef = jnp.exp2(-jnp.abs(bfj - beta_f[r - 1:r]))
        eb = jnp.exp2(-jnp.abs(bbj - bx_b[r:r + 1]))
        return qj * jnp.where(hi4, ef, eb), jnp.where(hi4, kbj * eb, kfj * ef)
    mb = (1 << lvl) // SUBLANES
    r = ((j // (2 * mb)) * 2 * mb + mb) * SUBLANES
    ref_f, ref_b = beta_f[r - 1:r], bx_b[r:r + 1]
    if (j // mb) & 1:
        return qj * jnp.exp2(bfj - ref_f), kbj * jnp.exp2(bbj - ref_b)
    return qj * jnp.exp2(ref_b - bbj), kfj * jnp.exp2(ref_f - bfj)


def _hgrn_kernel(lg_ref, hq_ref, hi_ref, hg_ref, gf_ref, gb_ref, ow_ref, tri_ref, mk_ref,
                 out_ref, o_scr, qi_scr, ut_scr, dec_scr, st_scr, g_scr, *, layer, n_slots):
    c, nl, dh = HG_CHUNK, HG_LEVELS, HG_HEAD_DIM
    seq = hq_ref.shape[0]
    nc = seq // c

    lg = lg_ref[...].astype(F32)
    lbs = []
    for d in range(2):
        rows = lg[d * n_slots:(d + 1) * n_slots]
        e = jnp.exp(rows - jnp.max(rows, axis=0, keepdims=True))
        lbs.append(jnp.sum(e[:layer + 1], axis=0, keepdims=True) / jnp.sum(e, axis=0, keepdims=True))
    lb_f, lb_b = lbs

    pos = lax.broadcasted_iota(jnp.int32, (SUBLANES, dh), 0)
    odd = (pos & 1) != 0
    hi2 = (pos & 2) != 0
    hi4 = (pos & 4) != 0

    def split(x):
        hi = x.astype(BF16)
        return hi, (x - hi.astype(F32)).astype(BF16)

    gc = HG_GROUP * c
    ng = nc // HG_GROUP

    def gates(gi, carry):
        r0 = pl.multiple_of(gi * gc, gc)
        f_f = lb_f + (1.0 - lb_f) * _sigmoid(gf_ref[pl.ds(r0, gc), :])
        f_b = lb_b + (1.0 - lb_b) * _sigmoid(gb_ref[pl.ds(r0, gc), :])
        lf_f = jnp.log2(f_f)
        lf_b = jnp.log2(f_b)
        pre = _dot(tri_ref[...], jnp.concatenate(split(lf_f) + split(lf_b), axis=1))
        g_scr[0, pl.ds(r0, gc), :] = f_f
        g_scr[1, pl.ds(r0, gc), :] = f_b
        g_scr[2, pl.ds(r0, gc), :] = pre[:, :dh] + pre[:, dh:2 * dh]
        g_scr[3, pl.ds(r0, gc), :] = pre[:, 2 * dh:3 * dh] + pre[:, 3 * dh:] - lf_b
        return carry

    lax.fori_loop(0, ng, gates, 0, unroll=HG_UNROLL // HG_GROUP)

    def per_chunk_rows(x, row):
        return jnp.concatenate([jnp.broadcast_to(x[g * c + row:g * c + row + 1], (c, dh))
                                for g in range(HG_GROUP)], axis=0)

    def intra(gi, carry):
        r0 = pl.multiple_of(gi * gc, gc)
        q = _silu(hq_ref[pl.ds(r0, gc), :].astype(F32))
        v = hi_ref[pl.ds(r0, gc), :]
        f_f = g_scr[0, pl.ds(r0, gc), :]
        f_b = g_scr[1, pl.ds(r0, gc), :]
        beta_f = g_scr[2, pl.ds(r0, gc), :]
        bx_b = g_scr[3, pl.ds(r0, gc), :]
        k_f = 1.0 - f_f
        k_b = 1.0 - f_b
        tot_f = per_chunk_rows(beta_f, c - 1)
        tot_b = per_chunk_rows(bx_b + jnp.log2(f_b), c - 1)

        a = mk_ref[0] * _dot_nt(q.astype(BF16), (k_f + k_b).astype(BF16))
        for lvl in range(nl):
            ops = [_level_operands(lvl, j, q, f_f, f_b, k_f, k_b, beta_f, bx_b, odd, hi2, hi4)
                   for j in range(gc // SUBLANES)]
            qs = jnp.concatenate([o[0] for o in ops], axis=0)
            ks = jnp.concatenate([o[1] for o in ops], axis=0)
            a = a + mk_ref[lvl + 1] * _dot_nt(qs.astype(BF16), ks.astype(BF16))
        o_scr[pl.ds(r0, gc), :] = _dot(a.astype(BF16), v)

        qi = jnp.concatenate([q * jnp.exp2(beta_f), q * jnp.exp2(tot_b - bx_b)], axis=1)
        qi_scr[pl.ds(r0, gc), :] = qi.astype(BF16)
        ks = jnp.concatenate([k_f * jnp.exp2(tot_f - beta_f), k_b * jnp.exp2(bx_b)], axis=1).astype(BF16)
        zero = jnp.zeros((c, dh), BF16)
        vd = jnp.concatenate([jnp.concatenate([v[g * c:(g + 1) * c] if h == g else zero
                                               for h in range(HG_GROUP)], axis=1)
                              for g in range(HG_GROUP)], axis=0)
        ut = _dot_tn(vd, ks)
        dec = jnp.exp2(jnp.concatenate([tot_f, tot_b], axis=1))
        for g in range(HG_GROUP):
            ut_scr[gi * HG_GROUP + g] = ut[g * dh:(g + 1) * dh]
            dec_scr[gi * HG_GROUP + g] = dec[g * c:g * c + 8]
        return carry

    lax.fori_loop(0, ng, intra, 0, unroll=HG_UNROLL // HG_GROUP)

    def states(i, carry):
        st_f, st_b = carry
        cf = i
        cb = nc - 1 - i
        st_scr[cf, :, :dh] = st_f.astype(BF16)
        st_scr[cb, :, dh:] = st_b.astype(BF16)
        st_f = st_f * dec_scr[cf, 0:1, :dh] + ut_scr[cf, :, :dh]
        st_b = st_b * dec_scr[cb, 0:1, dh:] + ut_scr[cb, :, dh:]
        return st_f, st_b

    zero = jnp.zeros((dh, dh), F32)
    lax.fori_loop(0, nc, states, (zero, zero))

    ow = ow_ref[...].astype(F32)

    def finish(ci, carry):
        r0 = pl.multiple_of(ci * c, c)
        o = o_scr[pl.ds(r0, c), :] + _dot_nt(qi_scr[pl.ds(r0, c), :], st_scr[ci])
        y = o * lax.rsqrt(jnp.mean(o * o, axis=-1, keepdims=True) + EPS) * ow
        out_ref[pl.ds(r0, c), :] = (y * _silu(hg_ref[pl.ds(r0, c), :].astype(F32))).astype(out_ref.dtype)
        return carry

    lax.fori_loop(0, nc, finish, 0, unroll=HG_UNROLL)


def _hgrn(main, gates, lb_logits, onorm_w, *, batch, seq, heads, layer, col_q, col_i, col_g):
    dh, c = HG_HEAD_DIM, HG_CHUNK
    n_slots = lb_logits.shape[1]
    lg = lb_logits.reshape(2 * n_slots, heads * dh)
    tri, masks = _hgrn_constants()
    nc = seq // c
    kern = functools.partial(_hgrn_kernel, layer=layer, n_slots=n_slots)
    return pl.pallas_call(
        kern,
        grid=(batch, heads),
        in_specs=[
            pl.BlockSpec((2 * n_slots, dh), lambda b, h: (0, h)),
            pl.BlockSpec((seq, dh), lambda b, h: (b, col_q + h)),
            pl.BlockSpec((seq, dh), lambda b, h: (b, col_i + h)),
            pl.BlockSpec((seq, dh), lambda b, h: (b, col_g + h)),
            pl.BlockSpec((seq, dh), lambda b, h: (b, h)),
            pl.BlockSpec((seq, dh), lambda b, h: (b, heads + h)),
            pl.BlockSpec((1, dh), lambda b, h: (0, 0)),
            pl.BlockSpec(tri.shape, lambda b, h: (0, 0)),
            pl.BlockSpec(masks.shape, lambda b, h: (0, 0, 0)),
        ],
        out_specs=pl.BlockSpec((seq, dh), lambda b, h: (b, h)),
        out_shape=jax.ShapeDtypeStruct((batch * seq, heads * dh), BF16),
        scratch_shapes=[
            pltpu.VMEM((seq, dh), F32),
            pltpu.VMEM((seq, 2 * dh), BF16),
            pltpu.VMEM((nc, dh, 2 * dh), F32),
            pltpu.VMEM((nc, 8, 2 * dh), F32),
            pltpu.VMEM((nc, dh, 2 * dh), BF16),
            pltpu.VMEM((4, seq, dh), F32),
        ],
        compiler_params=_cparams(("parallel", "parallel")),
        name="hgrn2",
    )(lg, main, main, main, gates, gates, onorm_w.reshape(1, dh),
      jnp.asarray(tri, BF16), jnp.asarray(masks, F32))


def _rel_bucket_index(rel):
    nb = N_BUCKETS // 2
    max_exact = nb // 2
    ret = jnp.where(rel > 0, nb, 0)
    n = jnp.abs(rel)
    nf = jnp.maximum(n, 1).astype(jnp.float32)
    large = max_exact + (jnp.log(nf / max_exact) / math.log(MAX_DISTANCE / max_exact)
                         * (nb - max_exact)).astype(jnp.int32)
    large = jnp.minimum(large, nb - 1)
    return ret + jnp.where(n < max_exact, n, large)


def _bias_tiles_kernel(tbl_ref, bkt_ref, out_ref):
    h = pl.program_id(0)
    bkt = bkt_ref[...]
    acc = jnp.zeros(bkt.shape, F32)
    for cidx in range(N_BUCKETS):
        acc = jnp.where(bkt == cidx, tbl_ref[cidx, h], acc)
    out_ref[0] = acc * LOG2E


def _bias_tiles(rel_bias):
    t = BIAS_TILE
    assert t >= MAX_DISTANCE
    heads = rel_bias.shape[1]
    key = jnp.arange(t, dtype=jnp.int32)[:, None]
    qry = jnp.arange(t, dtype=jnp.int32)[None, :]
    rel = jnp.stack([t * d + key - qry for d in range(-2, 3)])
    bkt = _rel_bucket_index(rel).astype(jnp.int32)
    return pl.pallas_call(
        _bias_tiles_kernel,
        grid=(heads,),
        in_specs=[
            pl.BlockSpec(memory_space=pltpu.SMEM),
            pl.BlockSpec((5, t, t), lambda h: (0, 0, 0)),
        ],
        out_specs=pl.BlockSpec((1, 5, t, t), lambda h: (h, 0, 0, 0)),
        out_shape=jax.ShapeDtypeStruct((heads, 5, t, t), F32),
        compiler_params=_cparams(("arbitrary",)),
        name="bias_tiles",
    )(rel_bias.astype(F32), bkt)


def _diff_attn_kernel(lam_ref, q_ref, k_ref, v_ref, tiles_ref, sw_ref, out_ref, vaug_scr,
                      s0_scr, s1_scr, m0_scr, m1_scr, e0_scr, e1_scr, *, lam_init, tq):
    seq, d2 = q_ref.shape
    t = BIAS_TILE
    s_scr, m_scr, e_scr = (s0_scr, s1_scr), (m0_scr, m1_scr), (e0_scr, e1_scr)

    vaug_scr[:, :d2] = v_ref[...]
    vaug_scr[:, d2:] = jnp.ones((seq, d2), BF16)

    lv = lam_ref[...].astype(F32)
    lam = (jnp.exp(jnp.sum(lv[0:1] * lv[1:2], axis=-1, keepdims=True))
           - jnp.exp(jnp.sum(lv[2:3] * lv[3:4], axis=-1, keepdims=True)) + lam_init)

    c = DA_HEAD_DIM ** -0.5 * LOG2E
    lane = lax.broadcasted_iota(jnp.int32, (tq, d2), 1)
    first = lane < DA_HEAD_DIM
    sw = sw_ref[...].astype(F32) * (1.0 - lam_init)

    def scores(n, slot):
        r0 = n * tq if isinstance(n, int) else pl.multiple_of(n * tq, tq)
        q = q_ref[pl.ds(r0, tq), :]
        zero = jnp.zeros_like(q)
        cols = []
        for ib in range(tq // t):
            d = [jnp.clip(jb - (n * (tq // t) + ib), -2, 2) + 2 for jb in range(seq // t)]
            cols.append(jnp.concatenate([tiles_ref[0, dj] for dj in d], axis=0))
        bias = jnp.concatenate(cols, axis=1)
        qq = jnp.concatenate([jnp.where(first, q, zero), jnp.where(first, zero, q)], axis=0)
        st = _dot_nt(k_ref[...], qq) * c
        st = jnp.concatenate([st[:, :tq] + bias, st[:, tq:] + bias], axis=1)
        s_scr[slot][...] = st
        m_scr[slot][...] = jnp.max(st, axis=0, keepdims=True)

    def numerators(slot):
        e_scr[slot][...] = jnp.exp2(s_scr[slot][...] - m_scr[slot][...]).astype(BF16)

    def values(n, slot):
        r0 = n * tq if isinstance(n, int) else pl.multiple_of(n * tq, tq)
        oa = _dot_tn(e_scr[slot][...], vaug_scr[...])
        on = oa[:, :d2] / oa[:, d2:]
        o = on[:tq] - lam * on[tq:]
        y = o * lax.rsqrt(jnp.mean(o * o, axis=-1, keepdims=True) + EPS) * sw
        out_ref[pl.ds(r0, tq), :] = y.astype(out_ref.dtype)

    nq = seq // tq
    assert nq % 2 == 0 and nq >= 4
    scores(0, 0)
    scores(1, 1)
    numerators(0)

    def pair(i, carry):
        scores(2 * i + 2, 0)
        numerators(1)
        values(2 * i, 0)
        scores(2 * i + 3, 1)
        numerators(0)
        values(2 * i + 1, 1)
        return carry

    lax.fori_loop(0, nq // 2 - 1, pair, 0)
    numerators(1)
    values(nq - 2, 0)
    values(nq - 1, 1)


def _diff_attn(main, lam_vecs, tiles, subln_w, *, batch, seq, heads, lam_init, col_q, col_k, col_v, tq):
    d2 = 2 * DA_HEAD_DIM
    t = BIAS_TILE
    kern = functools.partial(_diff_attn_kernel, lam_init=lam_init, tq=tq)
    return pl.pallas_call(
        kern,
        grid=(heads, batch),
        in_specs=[
            pl.BlockSpec(lam_vecs.shape, lambda h, b: (0, 0)),
            pl.BlockSpec((seq, d2), lambda h, b: (b, col_q + h)),
            pl.BlockSpec((seq, d2), lambda h, b: (b, col_k + h)),
            pl.BlockSpec((seq, d2), lambda h, b: (b, col_v + h)),
            pl.BlockSpec((1, 5, t, t), lambda h, b: (h, 0, 0, 0)),
            pl.BlockSpec((1, d2), lambda h, b: (0, 0)),
        ],
        out_specs=pl.BlockSpec((seq, d2), lambda h, b: (b, h)),
        out_shape=jax.ShapeDtypeStruct((batch * seq, heads * d2), BF16),
        scratch_shapes=[pltpu.VMEM((seq, 2 * d2), BF16),
                        pltpu.VMEM((seq, 2 * tq), F32), pltpu.VMEM((seq, 2 * tq), F32),
                        pltpu.VMEM((1, 2 * tq), F32), pltpu.VMEM((1, 2 * tq), F32),
                        pltpu.VMEM((seq, 2 * tq), BF16), pltpu.VMEM((seq, 2 * tq), BF16)],
        compiler_params=_cparams(("parallel", "parallel")),
        name="diff_attn",
    )(lam_vecs, main, main, main, tiles, subln_w.reshape(1, d2))


def _out_proj_kernel(x_ref, a_ref, b_ref, wa_ref, wb_ref, nw_ref, h_ref, u_ref):
    h = x_ref[...] + _dot(a_ref[...], wa_ref[...]) + _dot(b_ref[...], wb_ref[...])
    h_ref[...] = h
    ms = jnp.mean(h * h, axis=-1, keepdims=True)
    u_ref[...] = (h * lax.rsqrt(ms + EPS) * nw_ref[...]).astype(u_ref.dtype)


def _out_proj(x, a, b, w, nw, tm):
    m, d = x.shape
    ka, kb = a.shape[1], b.shape[1]
    assert ka == kb and w.shape == (ka + kb, d)
    resident = pl.Buffered(1)
    return pl.pallas_call(
        _out_proj_kernel,
        grid=(m // tm,),
        in_specs=[
            pl.BlockSpec((tm, d), lambda i: (i, 0)),
            pl.BlockSpec((tm, ka), lambda i: (i, 0)),
            pl.BlockSpec((tm, kb), lambda i: (i, 0)),
            pl.BlockSpec((ka, d), lambda i: (0, 0), pipeline_mode=resident),
            pl.BlockSpec((kb, d), lambda i: (1, 0), pipeline_mode=resident),
            pl.BlockSpec((1, d), lambda i: (0, 0)),
        ],
        out_specs=[pl.BlockSpec((tm, d), lambda i: (i, 0)), pl.BlockSpec((tm, d), lambda i: (i, 0))],
        out_shape=[jax.ShapeDtypeStruct((m, d), F32), jax.ShapeDtypeStruct((m, d), BF16)],
        compiler_params=_cparams(("parallel",)),
        name="out_proj",
    )(x, a, b, w, w, nw.reshape(1, d))


def _ffn_up_kernel(u_ref, wg_ref, wu_ref, o_ref, wg_scr, wu_scr):
    @pl.when(pl.program_id(1) == 0)
    def _():
        wg_scr[...] = wg_ref[...].astype(BF16)
        wu_scr[...] = wu_ref[...].astype(BF16)

    u = u_ref[...]
    g = _dot(u, wg_scr[...])
    up = _dot(u, wu_scr[...])
    o_ref[...] = (_silu(g) * up).astype(o_ref.dtype)


def _ffn_up(u, wg, wu, layer, tm, tn):
    m, d = u.shape
    n = wg.shape[2]
    return pl.pallas_call(
        _ffn_up_kernel,
        grid=(n // tn, m // tm),
        in_specs=[
            pl.BlockSpec((tm, d), lambda j, i: (i, 0)),
            pl.BlockSpec((None, d, tn), lambda j, i: (layer, 0, j)),
            pl.BlockSpec((None, d, tn), lambda j, i: (layer, 0, j)),
        ],
        out_specs=pl.BlockSpec((tm, tn), lambda j, i: (i, j)),
        out_shape=jax.ShapeDtypeStruct((m, n), BF16),
        scratch_shapes=[pltpu.VMEM((d, tn), BF16), pltpu.VMEM((d, tn), BF16)],
        compiler_params=_cparams(("parallel", "arbitrary")),
        name="ffn_up",
    )(u, wg, wu)


def _ffn_down_kernel(a_ref, w_ref, h_ref, fw_ref, o_ref):
    y = h_ref[...] + _dot(a_ref[...], w_ref[...])
    ms = jnp.mean(y * y, axis=-1, keepdims=True)
    o_ref[...] = y * lax.rsqrt(ms + EPS) * fw_ref[...]


def _ffn_down(a, w, h, fw, tm):
    m, f = a.shape
    d = w.shape[1]
    return pl.pallas_call(
        _ffn_down_kernel,
        grid=(m // tm,),
        in_specs=[
            pl.BlockSpec((tm, f), lambda i: (i, 0)),
            pl.BlockSpec((f, d), lambda i: (0, 0), pipeline_mode=pl.Buffered(1)),
            pl.BlockSpec((tm, d), lambda i: (i, 0)),
            pl.BlockSpec((1, d), lambda i: (0, 0)),
        ],
        out_specs=pl.BlockSpec((tm, d), lambda i: (i, 0)),
        out_shape=jax.ShapeDtypeStruct((m, d), F32),
        compiler_params=_cparams(("parallel",)),
        name="ffn_down",
    )(a, w, h, fw.reshape(1, d))


def kernel(x, norm1_w, w_in, hg_lb_logits, hg_onorm_w, lambda_q1, lambda_k1, lambda_q2, lambda_k2,
           da_subln_w, rel_bias, w_out, norm2_w, w_gate, w_up, w_down, final_norm_w):
    batch, seq, d_model = x.shape
    depth = w_in.shape[0]
    hg_width = hg_lb_logits.shape[-1]
    da_width = d_model - hg_width
    hg_heads = hg_width // HG_HEAD_DIM
    da_heads = da_width // (2 * DA_HEAD_DIM)
    assert w_in.shape[2] == 5 * hg_width + 3 * da_width
    assert seq % HG_CHUNK == 0 and seq % BIAS_TILE == 0
    m = batch * seq
    blk = LANES

    tiles = _bias_tiles(rel_bias)
    h = x.reshape(m, d_model)
    for l in range(depth):
        main, gates = _in_proj(h, norm1_w[l], w_in[l].astype(BF16), 2 * hg_width, 4 * hg_width,
                               TILES["in_proj_m"], TILES["in_proj_n"])

        o_hg = _hgrn(main, gates, hg_lb_logits, hg_onorm_w[l], batch=batch, seq=seq, heads=hg_heads,
                     layer=l, col_q=0, col_i=hg_width // blk, col_g=2 * hg_width // blk)

        lam_init = 0.8 - 0.6 * math.exp(-0.3 * l)
        lam_vecs = jnp.stack([lambda_q1[l], lambda_k1[l], lambda_q2[l], lambda_k2[l]]).astype(F32)
        base = 3 * hg_width // blk
        o_da = _diff_attn(main, lam_vecs, tiles, da_subln_w[l], batch=batch, seq=seq, heads=da_heads,
                          lam_init=lam_init, col_q=base, col_k=base + da_width // blk,
                          col_v=base + 2 * da_width // blk, tq=TILES["attn_q"])

        h, u2 = _out_proj(h, o_hg, o_da, w_out[l].astype(BF16), norm2_w[l], TILES["out_proj_m"])

        act = _ffn_up(u2, w_gate, w_up, l, TILES["ffn_up_m"], TILES["ffn_up_n"])
        last = l == depth - 1
        assert last, "final norm is fused into the last layer's down projection"
        h = _ffn_down(act, w_down[l].astype(BF16), h, final_norm_w, TILES["ffn_down_m"])
    return h.reshape(batch, seq, d_model)
```

```python
import functools
import math

import numpy as np
import jax
import jax.numpy as jnp
from jax import lax
from jax.experimental import pallas as pl
from jax.experimental.pallas import tpu as pltpu

F32 = jnp.float32
BF16 = jnp.bfloat16

EPS = 1e-6
LOG2E = math.log2(math.e)
HG_HEAD_DIM = 128
DA_HEAD_DIM = 64
N_BUCKETS = 32
MAX_DISTANCE = 128

LANES = 128
SUBLANES = 8
HG_CHUNK = 64
HG_LEVELS = 6
HG_GROUP = 2
HG_UNROLL = 16
BIAS_TILE = 128
VMEM_LIMIT = 56 * 1024 * 1024

TILES = {
    "in_proj_m": 1024, "in_proj_n": 1024,
    "attn_q": 256,
    "out_proj_m": 512,
    "ffn_up_m": 2048, "ffn_up_n": 512,
    "ffn_down_m": 512,
}


def _cparams(sem, flags=None):
    return pltpu.CompilerParams(dimension_semantics=sem, vmem_limit_bytes=VMEM_LIMIT, flags=flags)


def _dot(a, b):
    return jnp.dot(a, b, preferred_element_type=F32)


def _dot_nt(a, b):
    return lax.dot_general(a, b, (((1,), (1,)), ((), ())), preferred_element_type=F32)


def _dot_tn(a, b):
    return lax.dot_general(a, b, (((0,), (0,)), ((), ())), preferred_element_type=F32)


def _sigmoid(x):
    return 1.0 / (1.0 + jnp.exp(-x))


def _silu(x):
    return x * _sigmoid(x)


def _in_proj_kernel(x_ref, nw_ref, w_ref, main_ref, gates_ref, u_ref, *, n_main):
    j = pl.program_id(1)

    @pl.when(j == 0)
    def _():
        x = x_ref[...].astype(F32)
        ms = jnp.mean(x * x, axis=-1, keepdims=True)
        u_ref[...] = (x * lax.rsqrt(ms + EPS) * nw_ref[...]).astype(BF16)

    def emit(o_ref):
        r = _dot(u_ref[...], w_ref[...]).astype(o_ref.dtype)
        for c in range(o_ref.shape[0]):
            o_ref[c] = r[:, c * LANES:(c + 1) * LANES]

    @pl.when(j < n_main)
    def _():
        emit(main_ref)

    @pl.when(j >= n_main)
    def _():
        emit(gates_ref)


def _in_proj(x, nw, w, gate_lo, gate_hi, tm, tn):
    m, d = x.shape
    n = w.shape[1]
    assert gate_lo % tn == 0 and gate_hi % tn == 0
    n_before, n_gate = gate_lo // tn, (gate_hi - gate_lo) // tn
    n_main = n // tn - n_gate
    n_main_cols = n_main * tn

    def w_block(i, j):
        return 0, jnp.where(j < n_before, j, jnp.where(j < n_main, j + n_gate, j - (n_main - n_before)))

    kern = functools.partial(_in_proj_kernel, n_main=n_main)
    return pl.pallas_call(
        kern,
        grid=(m // tm, n // tn),
        in_specs=[
            pl.BlockSpec((tm, d), lambda i, j: (i, 0)),
            pl.BlockSpec((1, d), lambda i, j: (0, 0)),
            pl.BlockSpec((d, tn), w_block),
        ],
        out_specs=[
            pl.BlockSpec((tn // LANES, tm, LANES), lambda i, j: (jnp.minimum(j, n_main - 1), i, 0)),
            pl.BlockSpec((tn // LANES, tm, LANES), lambda i, j: (jnp.maximum(j - n_main, 0), i, 0)),
        ],
        out_shape=[
            jax.ShapeDtypeStruct((n_main_cols // LANES, m, LANES), BF16),
            jax.ShapeDtypeStruct(((n - n_main_cols) // LANES, m, LANES), F32),
        ],
        scratch_shapes=[pltpu.VMEM((tm, d), BF16)],
        compiler_params=_cparams(("parallel", "arbitrary")),
        name="in_proj",
    )(x, nw.reshape(1, d), w)


def _hgrn_constants():
    c, nl, g = HG_CHUNK, HG_LEVELS, HG_GROUP
    idx = np.arange(g * c)
    same_chunk = (idx[:, None] // c) == (idx[None, :] // c)
    tri = (same_chunk & (idx[:, None] >= idx[None, :])).astype(np.float32)
    masks = np.zeros((nl + 1, g * c, g * c), np.float32)
    masks[0] = np.eye(g * c, dtype=np.float32)
    for lvl in range(nl):
        m = 1 << lvl
        same_pair = (idx[:, None] // (2 * m)) == (idx[None, :] // (2 * m))
        other_sibling = (idx[:, None] // m) != (idx[None, :] // m)
        masks[lvl + 1] = (same_pair & other_sibling).astype(np.float32)
    return tri, masks


def _level_operands(lvl, j, q, f_f, f_b, k_f, k_b, beta_f, bx_b, odd, hi2, hi4):
    sl = slice(j * SUBLANES, (j + 1) * SUBLANES)
    qj, ffj, fbj, kfj, kbj = q[sl], f_f[sl], f_b[sl], k_f[sl], k_b[sl]
    if lvl == 0:
        return qj * jnp.where(odd, ffj, fbj), jnp.where(odd, kbj, kfj)
    if lvl == 1:
        up_f, dn_f = pltpu.roll(ffj, SUBLANES - 1, axis=0), pltpu.roll(ffj, 1, axis=0)
        up_b, dn_b = pltpu.roll(fbj, SUBLANES - 1, axis=0), pltpu.roll(fbj, 1, axis=0)
        fq = jnp.where(hi2, jnp.where(odd, ffj * dn_f, ffj), jnp.where(odd, fbj, fbj * up_b))
        ks = jnp.where(hi2, jnp.where(odd, kbj * dn_b, kbj), jnp.where(odd, kfj, kfj * up_f))
        return qj * fq, ks
    bfj, bbj = beta_f[sl], bx_b[sl]
    if lvl == 2:
        r = j * SUBLANES + SUBLANES // 2
        ef = jnp.exp2(-jnp.abs(bfj - beta_f[r - 1:r]))
        eb = jnp.exp2(-jnp.abs(bbj - bx_b[r:r + 1]))
        return qj * jnp.where(hi4, ef, eb), jnp.where(hi4, kbj * eb, kfj * ef)
    mb = (1 << lvl) // SUBLANES
    r = ((j // (2 * mb)) * 2 * mb + mb) * SUBLANES
    ref_f, ref_b = beta_f[r - 1:r], bx_b[r:r + 1]
    if (j // mb) & 1:
        return qj * jnp.exp2(bfj - ref_f), kbj * jnp.exp2(bbj - ref_b)
    return qj * jnp.exp2(ref_b - bbj), kfj * jnp.exp2(ref_f - bfj)


def _hgrn_kernel(lg_ref, hq_ref, hi_ref, hg_ref, gf_ref, gb_ref, ow_ref, tri_ref, mk_ref,
                 out_ref, o_scr, qi_scr, ut_scr, dec_scr, st_scr, g_scr, *, layer, n_slots):
    c, nl, dh = HG_CHUNK, HG_LEVELS, HG_HEAD_DIM
    seq = hq_ref.shape[0]
    nc = seq // c

    lg = lg_ref[...].astype(F32)
    lbs = []
    for d in range(2):
        rows = lg[d * n_slots:(d + 1) * n_slots]
        e = jnp.exp(rows - jnp.max(rows, axis=0, keepdims=True))
        lbs.append(jnp.sum(e[:layer + 1], axis=0, keepdims=True) / jnp.sum(e, axis=0, keepdims=True))
    lb_f, lb_b = lbs

    pos = lax.broadcasted_iota(jnp.int32, (SUBLANES, dh), 0)
    odd = (pos & 1) != 0
    hi2 = (pos & 2) != 0
    hi4 = (pos & 4) != 0

    def split(x):
        hi = x.astype(BF16)
        return hi, (x - hi.astype(F32)).astype(BF16)

    gc = HG_GROUP * c
    ng = nc // HG_GROUP

    def gates(gi, carry):
        r0 = pl.multiple_of(gi * gc, gc)
        f_f = lb_f + (1.0 - lb_f) * _sigmoid(gf_ref[pl.ds(r0, gc), :])
        f_b = lb_b + (1.0 - lb_b) * _sigmoid(gb_ref[pl.ds(r0, gc), :])
        lf_f = jnp.log2(f_f)
        lf_b = jnp.log2(f_b)
        pre = _dot(tri_ref[...], jnp.concatenate(split(lf_f) + split(lf_b), axis=1))
        g_scr[0, pl.ds(r0, gc), :] = f_f
        g_scr[1, pl.ds(r0, gc), :] = f_b
        g_scr[2, pl.ds(r0, gc), :] = pre[:, :dh] + pre[:, dh:2 * dh]
        g_scr[3, pl.ds(r0, gc), :] = pre[:, 2 * dh:3 * dh] + pre[:, 3 * dh:] - lf_b
        return carry

    lax.fori_loop(0, ng, gates, 0, unroll=HG_UNROLL // HG_GROUP)

    def per_chunk_rows(x, row):
        return jnp.concatenate([jnp.broadcast_to(x[g * c + row:g * c + row + 1], (c, dh))
                                for g in range(HG_GROUP)], axis=0)

    def intra(gi, carry):
        r0 = pl.multiple_of(gi * gc, gc)
        q = _silu(hq_ref[pl.ds(r0, gc), :].astype(F32))
        v = hi_ref[pl.ds(r0, gc), :]
        f_f = g_scr[0, pl.ds(r0, gc), :]
        f_b = g_scr[1, pl.ds(r0, gc), :]
        beta_f = g_scr[2, pl.ds(r0, gc), :]
        bx_b = g_scr[3, pl.ds(r0, gc), :]
        k_f = 1.0 - f_f
        k_b = 1.0 - f_b
        tot_f = per_chunk_rows(beta_f, c - 1)
        tot_b = per_chunk_rows(bx_b + jnp.log2(f_b), c - 1)

        a = mk_ref[0] * _dot_nt(q.astype(BF16), (k_f + k_b).astype(BF16))
        for lvl in range(nl):
            ops = [_level_operands(lvl, j, q, f_f, f_b, k_f, k_b, beta_f, bx_b, odd, hi2, hi4)
                   for j in range(gc // SUBLANES)]
            qs = jnp.concatenate([o[0] for o in ops], axis=0)
            ks = jnp.concatenate([o[1] for o in ops], axis=0)
            a = a + mk_ref[lvl + 1] * _dot_nt(qs.astype(BF16), ks.astype(BF16))
        o_scr[pl.ds(r0, gc), :] = _dot(a.astype(BF16), v)

        qi = jnp.concatenate([q * jnp.exp2(beta_f), q * jnp.exp2(tot_b - bx_b)], axis=1)
        qi_scr[pl.ds(r0, gc), :] = qi.astype(BF16)
        ks = jnp.concatenate([k_f * jnp.exp2(tot_f - beta_f), k_b * jnp.exp2(bx_b)], axis=1).astype(BF16)
        zero = jnp.zeros((c, dh), BF16)
        vd = jnp.concatenate([jnp.concatenate([v[g * c:(g + 1) * c] if h == g else zero
                                               for h in range(HG_GROUP)], axis=1)
                              for g in range(HG_GROUP)], axis=0)
        ut = _dot_tn(vd, ks)
        dec = jnp.exp2(jnp.concatenate([tot_f, tot_b], axis=1))
        for g in range(HG_GROUP):
            ut_scr[gi * HG_GROUP + g] = ut[g * dh:(g + 1) * dh]
            dec_scr[gi * HG_GROUP + g] = dec[g * c:g * c + 8]
        return carry

    lax.fori_loop(0, ng, intra, 0, unroll=HG_UNROLL // HG_GROUP)

    def states(i, carry):
        st_f, st_b = carry
        cf = i
        cb = nc - 1 - i
        st_scr[cf, :, :dh] = st_f.astype(BF16)
        st_scr[cb, :, dh:] = st_b.astype(BF16)
        st_f = st_f * dec_scr[cf, 0:1, :dh] + ut_scr[cf, :, :dh]
        st_b = st_b * dec_scr[cb, 0:1, dh:] + ut_scr[cb, :, dh:]
        return st_f, st_b

    zero = jnp.zeros((dh, dh), F32)
    lax.fori_loop(0, nc, states, (zero, zero))

    ow = ow_ref[...].astype(F32)

    def finish(ci, carry):
        r0 = pl.multiple_of(ci * c, c)
        o = o_scr[pl.ds(r0, c), :] + _dot_nt(qi_scr[pl.ds(r0, c), :], st_scr[ci])
        y = o * lax.rsqrt(jnp.mean(o * o, axis=-1, keepdims=True) + EPS) * ow
        out_ref[pl.ds(r0, c), :] = (y * _silu(hg_ref[pl.ds(r0, c), :].astype(F32))).astype(out_ref.dtype)
        return carry

    lax.fori_loop(0, nc, finish, 0, unroll=HG_UNROLL)


def _hgrn(main, gates, lb_logits, onorm_w, *, batch, seq, heads, layer, col_q, col_i, col_g):
    dh, c = HG_HEAD_DIM, HG_CHUNK
    n_slots = lb_logits.shape[1]
    lg = lb_logits.reshape(2 * n_slots, heads * dh)
    tri, masks = _hgrn_constants()
    nc = seq // c
    kern = functools.partial(_hgrn_kernel, layer=layer, n_slots=n_slots)
    return pl.pallas_call(
        kern,
        grid=(batch, heads),
        in_specs=[
            pl.BlockSpec((2 * n_slots, dh), lambda b, h: (0, h)),
            pl.BlockSpec((None, seq, dh), lambda b, h: (col_q + h, b, 0)),
            pl.BlockSpec((None, seq, dh), lambda b, h: (col_i + h, b, 0)),
            pl.BlockSpec((None, seq, dh), lambda b, h: (col_g + h, b, 0)),
            pl.BlockSpec((None, seq, dh), lambda b, h: (h, b, 0)),
            pl.BlockSpec((None, seq, dh), lambda b, h: (heads + h, b, 0)),
            pl.BlockSpec((1, dh), lambda b, h: (0, 0)),
            pl.BlockSpec(tri.shape, lambda b, h: (0, 0)),
            pl.BlockSpec(masks.shape, lambda b, h: (0, 0, 0)),
        ],
        out_specs=pl.BlockSpec((None, seq, dh), lambda b, h: (h, b, 0)),
        out_shape=jax.ShapeDtypeStruct((heads, batch * seq, dh), BF16),
        scratch_shapes=[
            pltpu.VMEM((seq, dh), F32),
            pltpu.VMEM((seq, 2 * dh), BF16),
            pltpu.VMEM((nc, dh, 2 * dh), F32),
            pltpu.VMEM((nc, 8, 2 * dh), F32),
            pltpu.VMEM((nc, dh, 2 * dh), BF16),
            pltpu.VMEM((4, seq, dh), F32),
        ],
        compiler_params=_cparams(("parallel", "parallel")),
        name="hgrn2",
    )(lg, main, main, main, gates, gates, onorm_w.reshape(1, dh),
      jnp.asarray(tri, BF16), jnp.asarray(masks, F32))


def _rel_bucket_index(rel):
    nb = N_BUCKETS // 2
    max_exact = nb // 2
    ret = jnp.where(rel > 0, nb, 0)
    n = jnp.abs(rel)
    nf = jnp.maximum(n, 1).astype(jnp.float32)
    large = max_exact + (jnp.log(nf / max_exact) / math.log(MAX_DISTANCE / max_exact)
                         * (nb - max_exact)).astype(jnp.int32)
    large = jnp.minimum(large, nb - 1)
    return ret + jnp.where(n < max_exact, n, large)


def _bias_tiles_kernel(tbl_ref, bkt_ref, out_ref):
    h = pl.program_id(0)
    bkt = bkt_ref[...]
    acc = jnp.zeros(bkt.shape, F32)
    for cidx in range(N_BUCKETS):
        acc = jnp.where(bkt == cidx, tbl_ref[cidx, h], acc)
    out_ref[0] = acc * LOG2E


def _bias_tiles(rel_bias):
    t = BIAS_TILE
    assert t >= MAX_DISTANCE
    heads = rel_bias.shape[1]
    key = jnp.arange(t, dtype=jnp.int32)[:, None]
    qry = jnp.arange(t, dtype=jnp.int32)[None, :]
    rel = jnp.stack([t * d + key - qry for d in range(-2, 3)])
    bkt = _rel_bucket_index(rel).astype(jnp.int32)
    return pl.pallas_call(
        _bias_tiles_kernel,
        grid=(heads,),
        in_specs=[
            pl.BlockSpec(memory_space=pltpu.SMEM),
            pl.BlockSpec((5, t, t), lambda h: (0, 0, 0)),
        ],
        out_specs=pl.BlockSpec((1, 5, t, t), lambda h: (h, 0, 0, 0)),
        out_shape=jax.ShapeDtypeStruct((heads, 5, t, t), F32),
        compiler_params=_cparams(("arbitrary",)),
        name="bias_tiles",
    )(rel_bias.astype(F32), bkt)


def _diff_attn_kernel(lam_ref, q_ref, k_ref, v_ref, tiles_ref, sw_ref, out_ref, vaug_scr,
                      s0_scr, s1_scr, m0_scr, m1_scr, e0_scr, e1_scr, *, lam_init, tq):
    seq, d2 = q_ref.shape
    t = BIAS_TILE
    s_scr, m_scr, e_scr = (s0_scr, s1_scr), (m0_scr, m1_scr), (e0_scr, e1_scr)

    vaug_scr[:d2, :] = v_ref[...].T
    vaug_scr[d2:, :] = jnp.ones((vaug_scr.shape[0] - d2, seq), BF16)

    lv = lam_ref[...].astype(F32)
    lam = (jnp.exp(jnp.sum(lv[0:1] * lv[1:2], axis=-1, keepdims=True))
           - jnp.exp(jnp.sum(lv[2:3] * lv[3:4], axis=-1, keepdims=True)) + lam_init)

    c = DA_HEAD_DIM ** -0.5 * LOG2E
    lane = lax.broadcasted_iota(jnp.int32, (tq, d2), 1)
    first = lane < DA_HEAD_DIM
    sw = sw_ref[...].astype(F32) * (1.0 - lam_init)

    def scores(n, slot):
        r0 = n * tq if isinstance(n, int) else pl.multiple_of(n * tq, tq)
        q = q_ref[pl.ds(r0, tq), :]
        zero = jnp.zeros_like(q)
        cols = []
        for ib in range(tq // t):
            d = [jnp.clip(jb - (n * (tq // t) + ib), -2, 2) + 2 for jb in range(seq // t)]
            cols.append(jnp.concatenate([tiles_ref[0, dj] for dj in d], axis=0))
        bias = jnp.concatenate(cols, axis=1)
        qq = jnp.concatenate([jnp.where(first, q, zero), jnp.where(first, zero, q)], axis=0)
        st = _dot_nt(k_ref[...], qq) * c
        st = jnp.concatenate([st[:, :tq] + bias, st[:, tq:] + bias], axis=1)
        s_scr[slot][...] = st
        m_scr[slot][...] = jnp.max(st, axis=0, keepdims=True)

    def numerators(slot):
        e_scr[slot][...] = jnp.exp2(s_scr[slot][...] - m_scr[slot][...]).astype(BF16)

    def values(n, slot):
        r0 = n * tq if isinstance(n, int) else pl.multiple_of(n * tq, tq)
        ot = _dot(vaug_scr[...], e_scr[slot][...])
        on = ot[:d2] / ot[d2:d2 + 1]
        o = on[:, :tq] - lam * on[:, tq:]
        yt = o * lax.rsqrt(jnp.mean(o * o, axis=0, keepdims=True) + EPS)
        out_ref[pl.ds(r0, tq), :] = (yt.T * sw).astype(out_ref.dtype)

    nq = seq // tq
    assert nq % 2 == 0 and nq >= 4
    scores(0, 0)
    scores(1, 1)
    numerators(0)

    def pair(i, carry):
        scores(2 * i + 2, 0)
        numerators(1)
        values(2 * i, 0)
        scores(2 * i + 3, 1)
        numerators(0)
        values(2 * i + 1, 1)
        return carry

    lax.fori_loop(0, nq // 2 - 1, pair, 0)
    numerators(1)
    values(nq - 2, 0)
    values(nq - 1, 1)


def _diff_attn(main, lam_vecs, tiles, subln_w, *, batch, seq, heads, lam_init, col_q, col_k, col_v, tq):
    d2 = 2 * DA_HEAD_DIM
    t = BIAS_TILE
    kern = functools.partial(_diff_attn_kernel, lam_init=lam_init, tq=tq)
    return pl.pallas_call(
        kern,
        grid=(heads, batch),
        in_specs=[
            pl.BlockSpec(lam_vecs.shape, lambda h, b: (0, 0)),
            pl.BlockSpec((None, seq, d2), lambda h, b: (col_q + h, b, 0)),
            pl.BlockSpec((None, seq, d2), lambda h, b: (col_k + h, b, 0)),
            pl.BlockSpec((None, seq, d2), lambda h, b: (col_v + h, b, 0)),
            pl.BlockSpec((1, 5, t, t), lambda h, b: (h, 0, 0, 0)),
            pl.BlockSpec((1, d2), lambda h, b: (0, 0)),
        ],
        out_specs=pl.BlockSpec((None, seq, d2), lambda h, b: (h, b, 0)),
        out_shape=jax.ShapeDtypeStruct((heads, batch * seq, d2), BF16),
        scratch_shapes=[pltpu.VMEM((d2 + 2 * SUBLANES, seq), BF16),
                        pltpu.VMEM((seq, 2 * tq), F32), pltpu.VMEM((seq, 2 * tq), F32),
                        pltpu.VMEM((1, 2 * tq), F32), pltpu.VMEM((1, 2 * tq), F32),
                        pltpu.VMEM((seq, 2 * tq), BF16), pltpu.VMEM((seq, 2 * tq), BF16)],
        compiler_params=_cparams(("parallel", "parallel")),
        name="diff_attn",
    )(lam_vecs, main, main, main, tiles, subln_w.reshape(1, d2))


def _out_proj_kernel(x_ref, a_ref, b_ref, wa_ref, wb_ref, nw_ref, h_ref, u_ref):
    def heads_to_lanes(ref):
        return jnp.concatenate([ref[c] for c in range(ref.shape[0])], axis=1)

    h = x_ref[...] + _dot(heads_to_lanes(a_ref), wa_ref[...]) + _dot(heads_to_lanes(b_ref), wb_ref[...])
    h_ref[...] = h
    ms = jnp.mean(h * h, axis=-1, keepdims=True)
    u_ref[...] = (h * lax.rsqrt(ms + EPS) * nw_ref[...]).astype(u_ref.dtype)


def _out_proj(x, a, b, w, nw, tm):
    m, d = x.shape
    ka, kb = a.shape[0] * a.shape[2], b.shape[0] * b.shape[2]
    assert ka == kb and w.shape == (ka + kb, d)
    resident = pl.Buffered(1)
    return pl.pallas_call(
        _out_proj_kernel,
        grid=(m // tm,),
        in_specs=[
            pl.BlockSpec((tm, d), lambda i: (i, 0)),
            pl.BlockSpec((a.shape[0], tm, a.shape[2]), lambda i: (0, i, 0)),
            pl.BlockSpec((b.shape[0], tm, b.shape[2]), lambda i: (0, i, 0)),
            pl.BlockSpec((ka, d), lambda i: (0, 0), pipeline_mode=resident),
            pl.BlockSpec((kb, d), lambda i: (1, 0), pipeline_mode=resident),
            pl.BlockSpec((1, d), lambda i: (0, 0)),
        ],
        out_specs=[pl.BlockSpec((tm, d), lambda i: (i, 0)), pl.BlockSpec((tm, d), lambda i: (i, 0))],
        out_shape=[jax.ShapeDtypeStruct((m, d), F32), jax.ShapeDtypeStruct((m, d), BF16)],
        compiler_params=_cparams(("parallel",)),
        name="out_proj",
    )(x, a, b, w, w, nw.reshape(1, d))


def _ffn_up_kernel(u_ref, wg_ref, wu_ref, o_ref, wg_scr, wu_scr):
    @pl.when(pl.program_id(1) == 0)
    def _():
        wg_scr[...] = wg_ref[...].astype(BF16)
        wu_scr[...] = wu_ref[...].astype(BF16)

    u = u_ref[...]
    g = _dot(u, wg_scr[...])
    up = _dot(u, wu_scr[...])
    o_ref[...] = (_silu(g) * up).astype(o_ref.dtype)


def _ffn_up(u, wg, wu, layer, tm, tn):
    m, d = u.shape
    n = wg.shape[2]
    return pl.pallas_call(
        _ffn_up_kernel,
        grid=(n // tn, m // tm),
        in_specs=[
            pl.BlockSpec((tm, d), lambda j, i: (i, 0)),
            pl.BlockSpec((None, d, tn), lambda j, i: (layer, 0, j)),
            pl.BlockSpec((None, d, tn), lambda j, i: (layer, 0, j)),
        ],
        out_specs=pl.BlockSpec((tm, tn), lambda j, i: (i, j)),
        out_shape=jax.ShapeDtypeStruct((m, n), BF16),
        scratch_shapes=[pltpu.VMEM((d, tn), BF16), pltpu.VMEM((d, tn), BF16)],
        compiler_params=_cparams(("parallel", "arbitrary")),
        name="ffn_up",
    )(u, wg, wu)


def _ffn_down_kernel(a_ref, w_ref, h_ref, fw_ref, o_ref):
    y = h_ref[...] + _dot(a_ref[...], w_ref[...])
    ms = jnp.mean(y * y, axis=-1, keepdims=True)
    o_ref[...] = y * lax.rsqrt(ms + EPS) * fw_ref[...]


def _ffn_down(a, w, h, fw, tm):
    m, f = a.shape
    d = w.shape[1]
    return pl.pallas_call(
        _ffn_down_kernel,
        grid=(m // tm,),
        in_specs=[
            pl.BlockSpec((tm, f), lambda i: (i, 0)),
            pl.BlockSpec((f, d), lambda i: (0, 0), pipeline_mode=pl.Buffered(1)),
            pl.BlockSpec((tm, d), lambda i: (i, 0)),
            pl.BlockSpec((1, d), lambda i: (0, 0)),
        ],
        out_specs=pl.BlockSpec((tm, d), lambda i: (i, 0)),
        out_shape=jax.ShapeDtypeStruct((m, d), F32),
        compiler_params=_cparams(("parallel",)),
        name="ffn_down",
    )(a, w, h, fw.reshape(1, d))


def kernel(x, norm1_w, w_in, hg_lb_logits, hg_onorm_w, lambda_q1, lambda_k1, lambda_q2, lambda_k2,
           da_subln_w, rel_bias, w_out, norm2_w, w_gate, w_up, w_down, final_norm_w):
    batch, seq, d_model = x.shape
    depth = w_in.shape[0]
    hg_width = hg_lb_logits.shape[-1]
    da_width = d_model - hg_width
    hg_heads = hg_width // HG_HEAD_DIM
    da_heads = da_width // (2 * DA_HEAD_DIM)
    assert w_in.shape[2] == 5 * hg_width + 3 * da_width
    assert seq % HG_CHUNK == 0 and seq % BIAS_TILE == 0
    m = batch * seq
    blk = LANES

    tiles = _bias_tiles(rel_bias)
    h = x.reshape(m, d_model)
    for l in range(depth):
        main, gates = _in_proj(h, norm1_w[l], w_in[l].astype(BF16), 2 * hg_width, 4 * hg_width,
                               TILES["in_proj_m"], TILES["in_proj_n"])

        o_hg = _hgrn(main, gates, hg_lb_logits, hg_onorm_w[l], batch=batch, seq=seq, heads=hg_heads,
                     layer=l, col_q=0, col_i=hg_width // blk, col_g=2 * hg_width // blk)

        lam_init = 0.8 - 0.6 * math.exp(-0.3 * l)
        lam_vecs = jnp.stack([lambda_q1[l], lambda_k1[l], lambda_q2[l], lambda_k2[l]]).astype(F32)
        base = 3 * hg_width // blk
        o_da = _diff_attn(main, lam_vecs, tiles, da_subln_w[l], batch=batch, seq=seq, heads=da_heads,
                          lam_init=lam_init, col_q=base, col_k=base + da_width // blk,
                          col_v=base + 2 * da_width // blk, tq=TILES["attn_q"])

        h, u2 = _out_proj(h, o_hg, o_da, w_out[l].astype(BF16), norm2_w[l], TILES["out_proj_m"])

        act = _ffn_up(u2, w_gate, w_up, l, TILES["ffn_up_m"], TILES["ffn_up_n"])
        last = l == depth - 1
        assert last, "final norm is fused into the last layer's down projection"
        h = _ffn_down(act, w_down[l].astype(BF16), h, final_norm_w, TILES["ffn_down_m"])
    return h.reshape(batch, seq, d_model)
```

```python
import functools
import math

import numpy as np
import jax
import jax.numpy as jnp
from jax import lax
from jax.experimental import pallas as pl
from jax.experimental.pallas import tpu as pltpu

F32 = jnp.float32
BF16 = jnp.bfloat16

EPS = 1e-6
LOG2E = math.log2(math.e)
HG_HEAD_DIM = 128
DA_HEAD_DIM = 64
N_BUCKETS = 32
MAX_DISTANCE = 128

LANES = 128
SUBLANES = 8
HG_CHUNK = 64
HG_LEVELS = 6
HG_GROUP = 4
HG_UNROLL = 16
BIAS_TILE = 128
VMEM_LIMIT = 56 * 1024 * 1024

TILES = {
    "in_proj_m": 1024, "in_proj_n": 1024,
    "attn_q": 256,
    "out_proj_m": 512,
    "ffn_up_m": 2048, "ffn_up_n": 512,
    "ffn_down_m": 512,
}


def _cparams(sem, flags=None):
    return pltpu.CompilerParams(dimension_semantics=sem, vmem_limit_bytes=VMEM_LIMIT, flags=flags)


def _dot(a, b):
    return jnp.dot(a, b, preferred_element_type=F32)


def _dot_nt(a, b):
    return lax.dot_general(a, b, (((1,), (1,)), ((), ())), preferred_element_type=F32)


def _dot_tn(a, b):
    return lax.dot_general(a, b, (((0,), (0,)), ((), ())), preferred_element_type=F32)


def _sigmoid(x):
    return 1.0 / (1.0 + jnp.exp(-x))


def _silu(x):
    return x * _sigmoid(x)


def _in_proj_kernel(x_ref, nw_ref, w_ref, main_ref, gates_ref, u_ref, *, n_main):
    j = pl.program_id(1)

    @pl.when(j == 0)
    def _():
        x = x_ref[...].astype(F32)
        ms = jnp.mean(x * x, axis=-1, keepdims=True)
        u_ref[...] = (x * lax.rsqrt(ms + EPS) * nw_ref[...]).astype(BF16)

    def emit(o_ref):
        r = _dot(u_ref[...], w_ref[...]).astype(o_ref.dtype)
        for c in range(o_ref.shape[0]):
            o_ref[c] = r[:, c * LANES:(c + 1) * LANES]

    @pl.when(j < n_main)
    def _():
        emit(main_ref)

    @pl.when(j >= n_main)
    def _():
        emit(gates_ref)


def _in_proj(x, nw, w, gate_lo, gate_hi, tm, tn):
    m, d = x.shape
    n = w.shape[1]
    assert gate_lo % tn == 0 and gate_hi % tn == 0
    n_before, n_gate = gate_lo // tn, (gate_hi - gate_lo) // tn
    n_main = n // tn - n_gate
    n_main_cols = n_main * tn

    def w_block(i, j):
        return 0, jnp.where(j < n_before, j, jnp.where(j < n_main, j + n_gate, j - (n_main - n_before)))

    kern = functools.partial(_in_proj_kernel, n_main=n_main)
    return pl.pallas_call(
        kern,
        grid=(m // tm, n // tn),
        in_specs=[
            pl.BlockSpec((tm, d), lambda i, j: (i, 0)),
            pl.BlockSpec((1, d), lambda i, j: (0, 0)),
            pl.BlockSpec((d, tn), w_block),
        ],
        out_specs=[
            pl.BlockSpec((tn // LANES, tm, LANES), lambda i, j: (jnp.minimum(j, n_main - 1), i, 0)),
            pl.BlockSpec((tn // LANES, tm, LANES), lambda i, j: (jnp.maximum(j - n_main, 0), i, 0)),
        ],
        out_shape=[
            jax.ShapeDtypeStruct((n_main_cols // LANES, m, LANES), BF16),
            jax.ShapeDtypeStruct(((n - n_main_cols) // LANES, m, LANES), F32),
        ],
        scratch_shapes=[pltpu.VMEM((tm, d), BF16)],
        compiler_params=_cparams(("parallel", "arbitrary")),
        name="in_proj",
    )(x, nw.reshape(1, d), w)


def _hgrn_constants():
    c, nl, g = HG_CHUNK, HG_LEVELS, HG_GROUP
    idx = np.arange(g * c)
    same_chunk = (idx[:, None] // c) == (idx[None, :] // c)
    tri = (same_chunk & (idx[:, None] >= idx[None, :])).astype(np.float32)
    masks = np.zeros((nl + 1, g * c, g * c), np.float32)
    masks[0] = np.eye(g * c, dtype=np.float32)
    for lvl in range(nl):
        m = 1 << lvl
        same_pair = (idx[:, None] // (2 * m)) == (idx[None, :] // (2 * m))
        other_sibling = (idx[:, None] // m) != (idx[None, :] // m)
        masks[lvl + 1] = (same_pair & other_sibling).astype(np.float32)
    return tri, masks


def _level_operands(lvl, j, q, f_f, f_b, k_f, k_b, beta_f, bx_b, odd, hi2, hi4):
    sl = slice(j * SUBLANES, (j + 1) * SUBLANES)
    qj, ffj, fbj, kfj, kbj = q[sl], f_f[sl], f_b[sl], k_f[sl], k_b[sl]
    if lvl == 0:
        return qj * jnp.where(odd, ffj, fbj), jnp.where(odd, kbj, kfj)
    if lvl == 1:
        up_f, dn_f = pltpu.roll(ffj, SUBLANES - 1, axis=0), pltpu.roll(ffj, 1, axis=0)
        up_b, dn_b = pltpu.roll(fbj, SUBLANES - 1, axis=0), pltpu.roll(fbj, 1, axis=0)
        fq = jnp.where(hi2, jnp.where(odd, ffj * dn_f, ffj), jnp.where(odd, fbj, fbj * up_b))
        ks = jnp.where(hi2, jnp.where(odd, kbj * dn_b, kbj), jnp.where(odd, kfj, kfj * up_f))
        return qj * fq, ks
    bfj, bbj = beta_f[sl], bx_b[sl]
    if lvl == 2:
        r = j * SUBLANES + SUBLANES // 2
        ef = jnp.exp2(-jnp.abs(bfj - beta_f[r - 1:r]))
        eb = jnp.exp2(-jnp.abs(bbj - bx_b[r:r + 1]))
        return qj * jnp.where(hi4, ef, eb), jnp.where(hi4, kbj * eb, kfj * ef)
    mb = (1 << lvl) // SUBLANES
    r = ((j // (2 * mb)) * 2 * mb + mb) * SUBLANES
    ref_f, ref_b = beta_f[r - 1:r], bx_b[r:r + 1]
    if (j // mb) & 1:
        return qj * jnp.exp2(bfj - ref_f), kbj * jnp.exp2(bbj - ref_b)
    return qj * jnp.exp2(ref_b - bbj), kfj * jnp.exp2(ref_f - bfj)


def _hgrn_kernel(lg_ref, hq_ref, hi_ref, hg_ref, gf_ref, gb_ref, ow_ref, tri_ref, mk_ref,
                 out_ref, o_scr, qi_scr, ut_scr, dec_scr, st_scr, g_scr, *, layer, n_slots):
    c, nl, dh = HG_CHUNK, HG_LEVELS, HG_HEAD_DIM
    seq = hq_ref.shape[0]
    nc = seq // c

    lg = lg_ref[...].astype(F32)
    lbs = []
    for d in range(2):
        rows = lg[d * n_slots:(d + 1) * n_slots]
        e = jnp.exp(rows - jnp.max(rows, axis=0, keepdims=True))
        lbs.append(jnp.sum(e[:layer + 1], axis=0, keepdims=True) / jnp.sum(e, axis=0, keepdims=True))
    lb_f, lb_b = lbs

    pos = lax.broadcasted_iota(jnp.int32, (SUBLANES, dh), 0)
    odd = (pos & 1) != 0
    hi2 = (pos & 2) != 0
    hi4 = (pos & 4) != 0

    def split(x):
        hi = x.astype(BF16)
        return hi, (x - hi.astype(F32)).astype(BF16)

    gc = HG_GROUP * c
    ng = nc // HG_GROUP

    def gates(gi, carry):
        r0 = pl.multiple_of(gi * gc, gc)
        f_f = lb_f + (1.0 - lb_f) * _sigmoid(gf_ref[pl.ds(r0, gc), :])
        f_b = lb_b + (1.0 - lb_b) * _sigmoid(gb_ref[pl.ds(r0, gc), :])
        lf_f = jnp.log2(f_f)
        lf_b = jnp.log2(f_b)
        pre = _dot(tri_ref[...], jnp.concatenate(split(lf_f) + split(lf_b), axis=1))
        g_scr[0, pl.ds(r0, gc), :] = f_f
        g_scr[1, pl.ds(r0, gc), :] = f_b
        g_scr[2, pl.ds(r0, gc), :] = pre[:, :dh] + pre[:, dh:2 * dh]
        g_scr[3, pl.ds(r0, gc), :] = pre[:, 2 * dh:3 * dh] + pre[:, 3 * dh:] - lf_b
        return carry

    lax.fori_loop(0, ng, gates, 0, unroll=HG_UNROLL // HG_GROUP)

    def per_chunk_rows(x, row):
        return jnp.concatenate([jnp.broadcast_to(x[g * c + row:g * c + row + 1], (c, dh))
                                for g in range(HG_GROUP)], axis=0)

    def intra(gi, carry):
        r0 = pl.multiple_of(gi * gc, gc)
        q = _silu(hq_ref[pl.ds(r0, gc), :].astype(F32))
        v = hi_ref[pl.ds(r0, gc), :]
        f_f = g_scr[0, pl.ds(r0, gc), :]
        f_b = g_scr[1, pl.ds(r0, gc), :]
        beta_f = g_scr[2, pl.ds(r0, gc), :]
        bx_b = g_scr[3, pl.ds(r0, gc), :]
        k_f = 1.0 - f_f
        k_b = 1.0 - f_b
        tot_f = per_chunk_rows(beta_f, c - 1)
        tot_b = per_chunk_rows(bx_b + jnp.log2(f_b), c - 1)

        a = mk_ref[0] * _dot_nt(q.astype(BF16), (k_f + k_b).astype(BF16))
        for lvl in range(nl):
            ops = [_level_operands(lvl, j, q, f_f, f_b, k_f, k_b, beta_f, bx_b, odd, hi2, hi4)
                   for j in range(gc // SUBLANES)]
            qs = jnp.concatenate([o[0] for o in ops], axis=0)
            ks = jnp.concatenate([o[1] for o in ops], axis=0)
            a = a + mk_ref[lvl + 1] * _dot_nt(qs.astype(BF16), ks.astype(BF16))
        o_scr[pl.ds(r0, gc), :] = _dot(a.astype(BF16), v)

        qi = jnp.concatenate([q * jnp.exp2(beta_f), q * jnp.exp2(tot_b - bx_b)], axis=1)
        qi_scr[pl.ds(r0, gc), :] = qi.astype(BF16)
        ks = jnp.concatenate([k_f * jnp.exp2(tot_f - beta_f), k_b * jnp.exp2(bx_b)], axis=1).astype(BF16)
        zero = jnp.zeros((c, dh), BF16)
        vd = jnp.concatenate([jnp.concatenate([v[g * c:(g + 1) * c] if h == g else zero
                                               for h in range(HG_GROUP)], axis=1)
                              for g in range(HG_GROUP)], axis=0)
        ut = _dot_tn(vd, ks)
        dec = jnp.exp2(jnp.concatenate([tot_f, tot_b], axis=1))
        for g in range(HG_GROUP):
            ut_scr[gi * HG_GROUP + g] = ut[g * dh:(g + 1) * dh]
            dec_scr[gi * HG_GROUP + g] = dec[g * c:g * c + 8]
        return carry

    lax.fori_loop(0, ng, intra, 0, unroll=HG_UNROLL // HG_GROUP)

    def states(i, carry):
        st_f, st_b = carry
        cf = i
        cb = nc - 1 - i
        st_scr[cf, :, :dh] = st_f.astype(BF16)
        st_scr[cb, :, dh:] = st_b.astype(BF16)
        st_f = st_f * dec_scr[cf, 0:1, :dh] + ut_scr[cf, :, :dh]
        st_b = st_b * dec_scr[cb, 0:1, dh:] + ut_scr[cb, :, dh:]
        return st_f, st_b

    zero = jnp.zeros((dh, dh), F32)
    lax.fori_loop(0, nc, states, (zero, zero))

    ow = ow_ref[...].astype(F32)

    def finish(ci, carry):
        r0 = pl.multiple_of(ci * c, c)
        o = o_scr[pl.ds(r0, c), :] + _dot_nt(qi_scr[pl.ds(r0, c), :], st_scr[ci])
        y = o * lax.rsqrt(jnp.mean(o * o, axis=-1, keepdims=True) + EPS) * ow
        out_ref[pl.ds(r0, c), :] = (y * _silu(hg_ref[pl.ds(r0, c), :].astype(F32))).astype(out_ref.dtype)
        return carry

    lax.fori_loop(0, nc, finish, 0, unroll=HG_UNROLL)


def _hgrn(main, gates, lb_logits, onorm_w, *, batch, seq, heads, layer, col_q, col_i, col_g):
    dh, c = HG_HEAD_DIM, HG_CHUNK
    n_slots = lb_logits.shape[1]
    lg = lb_logits.reshape(2 * n_slots, heads * dh)
    tri, masks = _hgrn_constants()
    nc = seq // c
    kern = functools.partial(_hgrn_kernel, layer=layer, n_slots=n_slots)
    return pl.pallas_call(
        kern,
        grid=(batch, heads),
        in_specs=[
            pl.BlockSpec((2 * n_slots, dh), lambda b, h: (0, h)),
            pl.BlockSpec((None, seq, dh), lambda b, h: (col_q + h, b, 0)),
            pl.BlockSpec((None, seq, dh), lambda b, h: (col_i + h, b, 0)),
            pl.BlockSpec((None, seq, dh), lambda b, h: (col_g + h, b, 0)),
            pl.BlockSpec((None, seq, dh), lambda b, h: (h, b, 0)),
            pl.BlockSpec((None, seq, dh), lambda b, h: (heads + h, b, 0)),
            pl.BlockSpec((1, dh), lambda b, h: (0, 0)),
            pl.BlockSpec(tri.shape, lambda b, h: (0, 0)),
            pl.BlockSpec(masks.shape, lambda b, h: (0, 0, 0)),
        ],
        out_specs=pl.BlockSpec((None, seq, dh), lambda b, h: (h, b, 0)),
        out_shape=jax.ShapeDtypeStruct((heads, batch * seq, dh), BF16),
        scratch_shapes=[
            pltpu.VMEM((seq, dh), F32),
            pltpu.VMEM((seq, 2 * dh), BF16),
            pltpu.VMEM((nc, dh, 2 * dh), F32),
            pltpu.VMEM((nc, 8, 2 * dh), F32),
            pltpu.VMEM((nc, dh, 2 * dh), BF16),
            pltpu.VMEM((4, seq, dh), F32),
        ],
        compiler_params=_cparams(("parallel", "parallel")),
        name="hgrn2",
    )(lg, main, main, main, gates, gates, onorm_w.reshape(1, dh),
      jnp.asarray(tri, BF16), jnp.asarray(masks, F32))


def _rel_bucket_index(rel):
    nb = N_BUCKETS // 2
    max_exact = nb // 2
    ret = jnp.where(rel > 0, nb, 0)
    n = jnp.abs(rel)
    nf = jnp.maximum(n, 1).astype(jnp.float32)
    large = max_exact + (jnp.log(nf / max_exact) / math.log(MAX_DISTANCE / max_exact)
                         * (nb - max_exact)).astype(jnp.int32)
    large = jnp.minimum(large, nb - 1)
    return ret + jnp.where(n < max_exact, n, large)


def _bias_tiles_kernel(tbl_ref, bkt_ref, out_ref):
    h = pl.program_id(0)
    bkt = bkt_ref[...]
    acc = jnp.zeros(bkt.shape, F32)
    for cidx in range(N_BUCKETS):
        acc = jnp.where(bkt == cidx, tbl_ref[cidx, h], acc)
    out_ref[0] = acc * LOG2E


def _bias_tiles(rel_bias):
    t = BIAS_TILE
    assert t >= MAX_DISTANCE
    heads = rel_bias.shape[1]
    key = jnp.arange(t, dtype=jnp.int32)[:, None]
    qry = jnp.arange(t, dtype=jnp.int32)[None, :]
    rel = jnp.stack([t * d + key - qry for d in range(-2, 3)])
    bkt = _rel_bucket_index(rel).astype(jnp.int32)
    return pl.pallas_call(
        _bias_tiles_kernel,
        grid=(heads,),
        in_specs=[
            pl.BlockSpec(memory_space=pltpu.SMEM),
            pl.BlockSpec((5, t, t), lambda h: (0, 0, 0)),
        ],
        out_specs=pl.BlockSpec((1, 5, t, t), lambda h: (h, 0, 0, 0)),
        out_shape=jax.ShapeDtypeStruct((heads, 5, t, t), F32),
        compiler_params=_cparams(("arbitrary",)),
        name="bias_tiles",
    )(rel_bias.astype(F32), bkt)


def _diff_attn_kernel(lam_ref, q_ref, k_ref, v_ref, tiles_ref, sw_ref, out_ref, vaug_scr,
                      s0_scr, s1_scr, m0_scr, m1_scr, e0_scr, e1_scr, *, lam_init, tq):
    seq, d2 = q_ref.shape
    t = BIAS_TILE
    s_scr, m_scr, e_scr = (s0_scr, s1_scr), (m0_scr, m1_scr), (e0_scr, e1_scr)

    vaug_scr[:d2, :] = v_ref[...].T
    vaug_scr[d2:, :] = jnp.ones((vaug_scr.shape[0] - d2, seq), BF16)

    lv = lam_ref[...].astype(F32)
    lam = (jnp.exp(jnp.sum(lv[0:1] * lv[1:2], axis=-1, keepdims=True))
           - jnp.exp(jnp.sum(lv[2:3] * lv[3:4], axis=-1, keepdims=True)) + lam_init)

    c = DA_HEAD_DIM ** -0.5 * LOG2E
    lane = lax.broadcasted_iota(jnp.int32, (tq, d2), 1)
    first = lane < DA_HEAD_DIM
    sw = sw_ref[...].astype(F32) * (1.0 - lam_init)

    def scores(n, slot):
        r0 = n * tq if isinstance(n, int) else pl.multiple_of(n * tq, tq)
        q = q_ref[pl.ds(r0, tq), :]
        zero = jnp.zeros_like(q)
        cols = []
        for ib in range(tq // t):
            d = [jnp.clip(jb - (n * (tq // t) + ib), -2, 2) + 2 for jb in range(seq // t)]
            cols.append(jnp.concatenate([tiles_ref[0, dj] for dj in d], axis=0))
        bias = jnp.concatenate(cols, axis=1)
        qq = jnp.concatenate([jnp.where(first, q, zero), jnp.where(first, zero, q)], axis=0)
        st = _dot_nt(k_ref[...], qq) * c
        st = jnp.concatenate([st[:, :tq] + bias, st[:, tq:] + bias], axis=1)
        s_scr[slot][...] = st
        m_scr[slot][...] = jnp.max(st, axis=0, keepdims=True)

    def numerators(slot):
        e_scr[slot][...] = jnp.exp2(s_scr[slot][...] - m_scr[slot][...]).astype(BF16)

    def values(n, slot):
        r0 = n * tq if isinstance(n, int) else pl.multiple_of(n * tq, tq)
        ot = _dot(vaug_scr[...], e_scr[slot][...])
        on = ot[:d2] / ot[d2:d2 + 1]
        o = on[:, :tq] - lam * on[:, tq:]
        yt = o * lax.rsqrt(jnp.mean(o * o, axis=0, keepdims=True) + EPS)
        out_ref[pl.ds(r0, tq), :] = (yt.T * sw).astype(out_ref.dtype)

    nq = seq // tq
    assert nq % 2 == 0 and nq >= 4
    scores(0, 0)
    scores(1, 1)
    numerators(0)

    def pair(i, carry):
        scores(2 * i + 2, 0)
        numerators(1)
        values(2 * i, 0)
        scores(2 * i + 3, 1)
        numerators(0)
        values(2 * i + 1, 1)
        return carry

    lax.fori_loop(0, nq // 2 - 1, pair, 0)
    numerators(1)
    values(nq - 2, 0)
    values(nq - 1, 1)


def _diff_attn(main, lam_vecs, tiles, subln_w, *, batch, seq, heads, lam_init, col_q, col_k, col_v, tq):
    d2 = 2 * DA_HEAD_DIM
    t = BIAS_TILE
    kern = functools.partial(_diff_attn_kernel, lam_init=lam_init, tq=tq)
    return pl.pallas_call(
        kern,
        grid=(heads, batch),
        in_specs=[
            pl.BlockSpec(lam_vecs.shape, lambda h, b: (0, 0)),
            pl.BlockSpec((None, seq, d2), lambda h, b: (col_q + h, b, 0)),
            pl.BlockSpec((None, seq, d2), lambda h, b: (col_k + h, b, 0)),
            pl.BlockSpec((None, seq, d2), lambda h, b: (col_v + h, b, 0)),
            pl.BlockSpec((1, 5, t, t), lambda h, b: (h, 0, 0, 0)),
            pl.BlockSpec((1, d2), lambda h, b: (0, 0)),
        ],
        out_specs=pl.BlockSpec((None, seq, d2), lambda h, b: (h, b, 0)),
        out_shape=jax.ShapeDtypeStruct((heads, batch * seq, d2), BF16),
        scratch_shapes=[pltpu.VMEM((d2 + 2 * SUBLANES, seq), BF16),
                        pltpu.VMEM((seq, 2 * tq), F32), pltpu.VMEM((seq, 2 * tq), F32),
                        pltpu.VMEM((1, 2 * tq), F32), pltpu.VMEM((1, 2 * tq), F32),
                        pltpu.VMEM((seq, 2 * tq), BF16), pltpu.VMEM((seq, 2 * tq), BF16)],
        compiler_params=_cparams(("parallel", "parallel")),
        name="diff_attn",
    )(lam_vecs, main, main, main, tiles, subln_w.reshape(1, d2))


def _out_proj_kernel(x_ref, a_ref, b_ref, wa_ref, wb_ref, nw_ref, wd_ref, h_ref, u_ref, wd_out_ref):
    def heads_to_lanes(ref):
        return jnp.concatenate([ref[c] for c in range(ref.shape[0])], axis=1)

    h = x_ref[...] + _dot(heads_to_lanes(a_ref), wa_ref[...]) + _dot(heads_to_lanes(b_ref), wb_ref[...])
    h_ref[...] = h
    ms = jnp.mean(h * h, axis=-1, keepdims=True)
    u_ref[...] = (h * lax.rsqrt(ms + EPS) * nw_ref[...]).astype(u_ref.dtype)
    wd_out_ref[...] = wd_ref[...].astype(wd_out_ref.dtype)


def _out_proj(x, a, b, w, nw, wd, layer, tm):
    m, d = x.shape
    ka, kb = a.shape[0] * a.shape[2], b.shape[0] * b.shape[2]
    assert ka == kb and w.shape == (ka + kb, d)
    steps = m // tm
    f = wd.shape[1]
    rows = f // steps
    assert rows * steps == f and rows % (2 * SUBLANES) == 0
    resident = pl.Buffered(1)
    return pl.pallas_call(
        _out_proj_kernel,
        grid=(steps,),
        in_specs=[
            pl.BlockSpec((tm, d), lambda i: (i, 0)),
            pl.BlockSpec((a.shape[0], tm, a.shape[2]), lambda i: (0, i, 0)),
            pl.BlockSpec((b.shape[0], tm, b.shape[2]), lambda i: (0, i, 0)),
            pl.BlockSpec((ka, d), lambda i: (0, 0), pipeline_mode=resident),
            pl.BlockSpec((kb, d), lambda i: (1, 0), pipeline_mode=resident),
            pl.BlockSpec((1, d), lambda i: (0, 0)),
            pl.BlockSpec((None, rows, wd.shape[2]), lambda i: (layer, i, 0)),
        ],
        out_specs=[pl.BlockSpec((tm, d), lambda i: (i, 0)), pl.BlockSpec((tm, d), lambda i: (i, 0)),
                   pl.BlockSpec((rows, wd.shape[2]), lambda i: (i, 0))],
        out_shape=[jax.ShapeDtypeStruct((m, d), F32), jax.ShapeDtypeStruct((m, d), BF16),
                   jax.ShapeDtypeStruct(wd.shape[1:], BF16)],
        compiler_params=_cparams(("parallel",)),
        name="out_proj",
    )(x, a, b, w, w, nw.reshape(1, d), wd)


def _ffn_up_kernel(u_ref, wg_ref, wu_ref, o_ref, wg_scr, wu_scr):
    @pl.when(pl.program_id(1) == 0)
    def _():
        wg_scr[...] = wg_ref[...].astype(BF16)
        wu_scr[...] = wu_ref[...].astype(BF16)

    u = u_ref[...]
    g = _dot(u, wg_scr[...])
    up = _dot(u, wu_scr[...])
    o_ref[...] = (_silu(g) * up).astype(o_ref.dtype)


def _ffn_up(u, wg, wu, layer, tm, tn):
    m, d = u.shape
    n = wg.shape[2]
    return pl.pallas_call(
        _ffn_up_kernel,
        grid=(n // tn, m // tm),
        in_specs=[
            pl.BlockSpec((tm, d), lambda j, i: (i, 0)),
            pl.BlockSpec((None, d, tn), lambda j, i: (layer, 0, j)),
            pl.BlockSpec((None, d, tn), lambda j, i: (layer, 0, j)),
        ],
        out_specs=pl.BlockSpec((tm, tn), lambda j, i: (i, j)),
        out_shape=jax.ShapeDtypeStruct((m, n), BF16),
        scratch_shapes=[pltpu.VMEM((d, tn), BF16), pltpu.VMEM((d, tn), BF16)],
        compiler_params=_cparams(("parallel", "arbitrary")),
        name="ffn_up",
    )(u, wg, wu)


def _ffn_down_kernel(a_ref, w_ref, h_ref, fw_ref, o_ref):
    y = h_ref[...] + _dot(a_ref[...], w_ref[...])
    ms = jnp.mean(y * y, axis=-1, keepdims=True)
    o_ref[...] = y * lax.rsqrt(ms + EPS) * fw_ref[...]


def _ffn_down(a, w, h, fw, tm):
    m, f = a.shape
    d = w.shape[1]
    return pl.pallas_call(
        _ffn_down_kernel,
        grid=(m // tm,),
        in_specs=[
            pl.BlockSpec((tm, f), lambda i: (i, 0)),
            pl.BlockSpec((f, d), lambda i: (0, 0), pipeline_mode=pl.Buffered(1)),
            pl.BlockSpec((tm, d), lambda i: (i, 0)),
            pl.BlockSpec((1, d), lambda i: (0, 0)),
        ],
        out_specs=pl.BlockSpec((tm, d), lambda i: (i, 0)),
        out_shape=jax.ShapeDtypeStruct((m, d), F32),
        compiler_params=_cparams(("parallel",)),
        name="ffn_down",
    )(a, w, h, fw.reshape(1, d))


def kernel(x, norm1_w, w_in, hg_lb_logits, hg_onorm_w, lambda_q1, lambda_k1, lambda_q2, lambda_k2,
           da_subln_w, rel_bias, w_out, norm2_w, w_gate, w_up, w_down, final_norm_w):
    batch, seq, d_model = x.shape
    depth = w_in.shape[0]
    hg_width = hg_lb_logits.shape[-1]
    da_width = d_model - hg_width
    hg_heads = hg_width // HG_HEAD_DIM
    da_heads = da_width // (2 * DA_HEAD_DIM)
    assert w_in.shape[2] == 5 * hg_width + 3 * da_width
    assert seq % HG_CHUNK == 0 and seq % BIAS_TILE == 0
    m = batch * seq
    blk = LANES

    tiles = _bias_tiles(rel_bias)
    h = x.reshape(m, d_model)
    for l in range(depth):
        main, gates = _in_proj(h, norm1_w[l], w_in[l].astype(BF16), 2 * hg_width, 4 * hg_width,
                               TILES["in_proj_m"], TILES["in_proj_n"])

        o_hg = _hgrn(main, gates, hg_lb_logits, hg_onorm_w[l], batch=batch, seq=seq, heads=hg_heads,
                     layer=l, col_q=0, col_i=hg_width // blk, col_g=2 * hg_width // blk)

        lam_init = 0.8 - 0.6 * math.exp(-0.3 * l)
        lam_vecs = jnp.stack([lambda_q1[l], lambda_k1[l], lambda_q2[l], lambda_k2[l]]).astype(F32)
        base = 3 * hg_width // blk
        o_da = _diff_attn(main, lam_vecs, tiles, da_subln_w[l], batch=batch, seq=seq, heads=da_heads,
                          lam_init=lam_init, col_q=base, col_k=base + da_width // blk,
                          col_v=base + 2 * da_width // blk, tq=TILES["attn_q"])

        h, u2, w_down_bf16 = _out_proj(h, o_hg, o_da, w_out[l].astype(BF16), norm2_w[l], w_down, l,
                                       TILES["out_proj_m"])

        act = _ffn_up(u2, w_gate, w_up, l, TILES["ffn_up_m"], TILES["ffn_up_n"])
        last = l == depth - 1
        assert last, "final norm is fused into the last layer's down projection"
        h = _ffn_down(act, w_down_bf16, h, final_norm_w, TILES["ffn_down_m"])
    return h.reshape(batch, seq, d_model)
```

```python
import functools
import math

import numpy as np
import jax
import jax.numpy as jnp
from jax import lax
from jax.experimental import pallas as pl
from jax.experimental.pallas import tpu as pltpu

F32 = jnp.float32
BF16 = jnp.bfloat16

EPS = 1e-6
LOG2E = math.log2(math.e)
HG_HEAD_DIM = 128
DA_HEAD_DIM = 64
N_BUCKETS = 32
MAX_DISTANCE = 128

LANES = 128
SUBLANES = 8
HG_CHUNK = 64
HG_LEVELS = 6
HG_GROUP = 2
HG_UNROLL = 16
BIAS_TILE = 128
VMEM_LIMIT = 56 * 1024 * 1024

TILES = {
    "in_proj_m": 1024, "in_proj_n": 1024,
    "attn_q": 256,
    "out_proj_m": 512,
    "ffn_up_m": 2048, "ffn_up_n": 512,
    "ffn_down_m": 512,
}


def _cparams(sem, flags=None):
    return pltpu.CompilerParams(dimension_semantics=sem, vmem_limit_bytes=VMEM_LIMIT, flags=flags)


def _dot(a, b):
    return jnp.dot(a, b, preferred_element_type=F32)


def _dot_nt(a, b):
    return lax.dot_general(a, b, (((1,), (1,)), ((), ())), preferred_element_type=F32)


def _dot_tn(a, b):
    return lax.dot_general(a, b, (((0,), (0,)), ((), ())), preferred_element_type=F32)


def _sigmoid(x):
    return 1.0 / (1.0 + jnp.exp(-x))


def _silu(x):
    return x * _sigmoid(x)


def _in_proj_kernel(x_ref, nw_ref, w_ref, main_ref, gates_ref, u_ref, *, n_main):
    j = pl.program_id(1)

    @pl.when(j == 0)
    def _():
        x = x_ref[...].astype(F32)
        ms = jnp.mean(x * x, axis=-1, keepdims=True)
        u_ref[...] = (x * lax.rsqrt(ms + EPS) * nw_ref[...]).astype(BF16)

    def emit(o_ref):
        r = _dot(u_ref[...], w_ref[...]).astype(o_ref.dtype)
        for c in range(o_ref.shape[0]):
            o_ref[c] = r[:, c * LANES:(c + 1) * LANES]

    @pl.when(j < n_main)
    def _():
        emit(main_ref)

    @pl.when(j >= n_main)
    def _():
        emit(gates_ref)


def _in_proj(x, nw, w, gate_lo, gate_hi, tm, tn):
    m, d = x.shape
    n = w.shape[1]
    assert gate_lo % tn == 0 and gate_hi % tn == 0
    n_before, n_gate = gate_lo // tn, (gate_hi - gate_lo) // tn
    n_main = n // tn - n_gate
    n_main_cols = n_main * tn

    def w_block(i, j):
        return 0, jnp.where(j < n_before, j, jnp.where(j < n_main, j + n_gate, j - (n_main - n_before)))

    kern = functools.partial(_in_proj_kernel, n_main=n_main)
    return pl.pallas_call(
        kern,
        grid=(m // tm, n // tn),
        in_specs=[
            pl.BlockSpec((tm, d), lambda i, j: (i, 0)),
            pl.BlockSpec((1, d), lambda i, j: (0, 0)),
            pl.BlockSpec((d, tn), w_block),
        ],
        out_specs=[
            pl.BlockSpec((tn // LANES, tm, LANES), lambda i, j: (jnp.minimum(j, n_main - 1), i, 0)),
            pl.BlockSpec((tn // LANES, tm, LANES), lambda i, j: (jnp.maximum(j - n_main, 0), i, 0)),
        ],
        out_shape=[
            jax.ShapeDtypeStruct((n_main_cols // LANES, m, LANES), BF16),
            jax.ShapeDtypeStruct(((n - n_main_cols) // LANES, m, LANES), F32),
        ],
        scratch_shapes=[pltpu.VMEM((tm, d), BF16)],
        compiler_params=_cparams(("parallel", "arbitrary")),
        name="in_proj",
    )(x, nw.reshape(1, d), w)


def _hgrn_constants():
    c, nl, g = HG_CHUNK, HG_LEVELS, HG_GROUP
    idx = np.arange(g * c)
    same_chunk = (idx[:, None] // c) == (idx[None, :] // c)
    tri = (same_chunk & (idx[:, None] >= idx[None, :])).astype(np.float32)
    masks = np.zeros((nl + 1, g * c, g * c), np.float32)
    masks[0] = np.eye(g * c, dtype=np.float32)
    for lvl in range(nl):
        m = 1 << lvl
        same_pair = (idx[:, None] // (2 * m)) == (idx[None, :] // (2 * m))
        other_sibling = (idx[:, None] // m) != (idx[None, :] // m)
        masks[lvl + 1] = (same_pair & other_sibling).astype(np.float32)
    return tri, masks


def _level_operands(lvl, j, q, f_f, f_b, k_f, k_b, beta_f, bx_b, odd, hi2, hi4):
    sl = slice(j * SUBLANES, (j + 1) * SUBLANES)
    qj, ffj, fbj, kfj, kbj = q[sl], f_f[sl], f_b[sl], k_f[sl], k_b[sl]
    if lvl == 0:
        return qj * jnp.where(odd, ffj, fbj), jnp.where(odd, kbj, kfj)
    if lvl == 1:
        up_f, dn_f = pltpu.roll(ffj, SUBLANES - 1, axis=0), pltpu.roll(ffj, 1, axis=0)
        up_b, dn_b = pltpu.roll(fbj, SUBLANES - 1, axis=0), pltpu.roll(fbj, 1, axis=0)
        fq = jnp.where(hi2, jnp.where(odd, ffj * dn_f, ffj), jnp.where(odd, fbj, fbj * up_b))
        ks = jnp.where(hi2, jnp.where(odd, kbj * dn_b, kbj), jnp.where(odd, kfj, kfj * up_f))
        return qj * fq, ks
    bfj, bbj = beta_f[sl], bx_b[sl]
    if lvl == 2:
        r = j * SUBLANES + SUBLANES // 2
        ef = jnp.exp2(-jnp.abs(bfj - beta_f[r - 1:r]))
        eb = jnp.exp2(-jnp.abs(bbj - bx_b[r:r + 1]))
        return qj * jnp.where(hi4, ef, eb), jnp.where(hi4, kbj * eb, kfj * ef)
    mb = (1 << lvl) // SUBLANES
    r = ((j // (2 * mb)) * 2 * mb + mb) * SUBLANES
    ref_f, ref_b = beta_f[r - 1:r], bx_b[r:r + 1]
    if (j // mb) & 1:
        return qj * jnp.exp2(bfj - ref_f), kbj * jnp.exp2(bbj - ref_b)
    return qj * jnp.exp2(ref_b - bbj), kfj * jnp.exp2(ref_f - bfj)


def _hgrn_kernel(lg_ref, hq_ref, hi_ref, hg_ref, gf_ref, gb_ref, ow_ref, tri_ref, mk_ref,
                 out_ref, o_scr, qi_scr, ut_scr, dec_scr, st_scr, g_scr, *, layer, n_slots):
    c, nl, dh = HG_CHUNK, HG_LEVELS, HG_HEAD_DIM
    seq = hq_ref.shape[0]
    nc = seq // c

    lg = lg_ref[...].astype(F32)
    lbs = []
    for d in range(2):
        rows = lg[d * n_slots:(d + 1) * n_slots]
        e = jnp.exp(rows - jnp.max(rows, axis=0, keepdims=True))
        lbs.append(jnp.sum(e[:layer + 1], axis=0, keepdims=True) / jnp.sum(e, axis=0, keepdims=True))
    lb_f, lb_b = lbs

    pos = lax.broadcasted_iota(jnp.int32, (SUBLANES, dh), 0)
    odd = (pos & 1) != 0
    hi2 = (pos & 2) != 0
    hi4 = (pos & 4) != 0

    def split(x):
        hi = x.astype(BF16)
        return hi, (x - hi.astype(F32)).astype(BF16)

    gc = HG_GROUP * c
    ng = nc // HG_GROUP

    def gates(gi, carry):
        r0 = pl.multiple_of(gi * gc, gc)
        f_f = lb_f + (1.0 - lb_f) * _sigmoid(gf_ref[pl.ds(r0, gc), :])
        f_b = lb_b + (1.0 - lb_b) * _sigmoid(gb_ref[pl.ds(r0, gc), :])
        lf_f = jnp.log2(f_f)
        lf_b = jnp.log2(f_b)
        pre = _dot(tri_ref[...], jnp.concatenate(split(lf_f) + split(lf_b), axis=1))
        g_scr[0, pl.ds(r0, gc), :] = f_f
        g_scr[1, pl.ds(r0, gc), :] = f_b
        g_scr[2, pl.ds(r0, gc), :] = pre[:, :dh] + pre[:, dh:2 * dh]
        g_scr[3, pl.ds(r0, gc), :] = pre[:, 2 * dh:3 * dh] + pre[:, 3 * dh:] - lf_b
        return carry

    lax.fori_loop(0, ng, gates, 0, unroll=HG_UNROLL // HG_GROUP)

    def per_chunk_rows(x, row):
        return jnp.concatenate([jnp.broadcast_to(x[g * c + row:g * c + row + 1], (c, dh))
                                for g in range(HG_GROUP)], axis=0)

    def intra(gi, carry):
        r0 = pl.multiple_of(gi * gc, gc)
        q = _silu(hq_ref[pl.ds(r0, gc), :].astype(F32))
        v = hi_ref[pl.ds(r0, gc), :]
        f_f = g_scr[0, pl.ds(r0, gc), :]
        f_b = g_scr[1, pl.ds(r0, gc), :]
        beta_f = g_scr[2, pl.ds(r0, gc), :]
        bx_b = g_scr[3, pl.ds(r0, gc), :]
        k_f = 1.0 - f_f
        k_b = 1.0 - f_b
        tot_f = per_chunk_rows(beta_f, c - 1)
        tot_b = per_chunk_rows(bx_b + jnp.log2(f_b), c - 1)

        a = mk_ref[0] * _dot_nt(q.astype(BF16), (k_f + k_b).astype(BF16))
        for lvl in range(nl):
            ops = [_level_operands(lvl, j, q, f_f, f_b, k_f, k_b, beta_f, bx_b, odd, hi2, hi4)
                   for j in range(gc // SUBLANES)]
            qs = jnp.concatenate([o[0] for o in ops], axis=0)
            ks = jnp.concatenate([o[1] for o in ops], axis=0)
            a = a + mk_ref[lvl + 1] * _dot_nt(qs.astype(BF16), ks.astype(BF16))
        o_scr[pl.ds(r0, gc), :] = _dot(a.astype(BF16), v)

        qi = jnp.concatenate([q * jnp.exp2(beta_f), q * jnp.exp2(tot_b - bx_b)], axis=1)
        qi_scr[pl.ds(r0, gc), :] = qi.astype(BF16)
        ks = jnp.concatenate([k_f * jnp.exp2(tot_f - beta_f), k_b * jnp.exp2(bx_b)], axis=1).astype(BF16)
        zero = jnp.zeros((c, dh), BF16)
        vd = jnp.concatenate([jnp.concatenate([v[g * c:(g + 1) * c] if h == g else zero
                                               for h in range(HG_GROUP)], axis=1)
                              for g in range(HG_GROUP)], axis=0)
        ut = _dot_tn(vd, ks)
        dec = jnp.exp2(jnp.concatenate([tot_f, tot_b], axis=1))
        for g in range(HG_GROUP):
            ut_scr[gi * HG_GROUP + g] = ut[g * dh:(g + 1) * dh]
            dec_scr[gi * HG_GROUP + g] = dec[g * c:g * c + 8]
        return carry

    lax.fori_loop(0, ng, intra, 0, unroll=HG_UNROLL // HG_GROUP)

    def states(i, carry):
        st_f, st_b = carry
        cf = i
        cb = nc - 1 - i
        st_scr[cf, :, :dh] = st_f.astype(BF16)
        st_scr[cb, :, dh:] = st_b.astype(BF16)
        st_f = st_f * dec_scr[cf, 0:1, :dh] + ut_scr[cf, :, :dh]
        st_b = st_b * dec_scr[cb, 0:1, dh:] + ut_scr[cb, :, dh:]
        return st_f, st_b

    zero = jnp.zeros((dh, dh), F32)
    lax.fori_loop(0, nc, states, (zero, zero))

    ow = ow_ref[...].astype(F32)

    def finish(ci, carry):
        r0 = pl.multiple_of(ci * c, c)
        o = o_scr[pl.ds(r0, c), :] + _dot_nt(qi_scr[pl.ds(r0, c), :], st_scr[ci])
        y = o * lax.rsqrt(jnp.mean(o * o, axis=-1, keepdims=True) + EPS) * ow
        out_ref[pl.ds(r0, c), :] = (y * _silu(hg_ref[pl.ds(r0, c), :].astype(F32))).astype(out_ref.dtype)
        return carry

    lax.fori_loop(0, nc, finish, 0, unroll=HG_UNROLL)


def _hgrn(main, gates, lb_logits, onorm_w, *, batch, seq, heads, layer, col_q, col_i, col_g):
    dh, c = HG_HEAD_DIM, HG_CHUNK
    n_slots = lb_logits.shape[1]
    lg = lb_logits.reshape(2 * n_slots, heads * dh)
    tri, masks = _hgrn_constants()
    nc = seq // c
    kern = functools.partial(_hgrn_kernel, layer=layer, n_slots=n_slots)
    return pl.pallas_call(
        kern,
        grid=(batch, heads),
        in_specs=[
            pl.BlockSpec((2 * n_slots, dh), lambda b, h: (0, h)),
            pl.BlockSpec((None, seq, dh), lambda b, h: (col_q + h, b, 0)),
            pl.BlockSpec((None, seq, dh), lambda b, h: (col_i + h, b, 0)),
            pl.BlockSpec((None, seq, dh), lambda b, h: (col_g + h, b, 0)),
            pl.BlockSpec((None, seq, dh), lambda b, h: (h, b, 0)),
            pl.BlockSpec((None, seq, dh), lambda b, h: (heads + h, b, 0)),
            pl.BlockSpec((1, dh), lambda b, h: (0, 0)),
            pl.BlockSpec(tri.shape, lambda b, h: (0, 0)),
            pl.BlockSpec(masks.shape, lambda b, h: (0, 0, 0)),
        ],
        out_specs=pl.BlockSpec((None, seq, dh), lambda b, h: (h, b, 0)),
        out_shape=jax.ShapeDtypeStruct((heads, batch * seq, dh), BF16),
        scratch_shapes=[
            pltpu.VMEM((seq, dh), F32),
            pltpu.VMEM((seq, 2 * dh), BF16),
            pltpu.VMEM((nc, dh, 2 * dh), F32),
            pltpu.VMEM((nc, 8, 2 * dh), F32),
            pltpu.VMEM((nc, dh, 2 * dh), BF16),
            pltpu.VMEM((4, seq, dh), F32),
        ],
        compiler_params=_cparams(("parallel", "parallel")),
        name="hgrn2",
    )(lg, main, main, main, gates, gates, onorm_w.reshape(1, dh),
      jnp.asarray(tri, BF16), jnp.asarray(masks, F32))


def _rel_bucket_index(rel):
    nb = N_BUCKETS // 2
    max_exact = nb // 2
    ret = jnp.where(rel > 0, nb, 0)
    n = jnp.abs(rel)
    nf = jnp.maximum(n, 1).astype(jnp.float32)
    large = max_exact + (jnp.log(nf / max_exact) / math.log(MAX_DISTANCE / max_exact)
                         * (nb - max_exact)).astype(jnp.int32)
    large = jnp.minimum(large, nb - 1)
    return ret + jnp.where(n < max_exact, n, large)


def _bias_tiles_kernel(tbl_ref, bkt_ref, out_ref):
    h = pl.program_id(0)
    bkt = bkt_ref[...]
    acc = jnp.zeros(bkt.shape, F32)
    for cidx in range(N_BUCKETS):
        acc = jnp.where(bkt == cidx, tbl_ref[cidx, h], acc)
    out_ref[0] = acc * LOG2E


def _bias_tiles(rel_bias):
    t = BIAS_TILE
    assert t >= MAX_DISTANCE
    heads = rel_bias.shape[1]
    key = jnp.arange(t, dtype=jnp.int32)[:, None]
    qry = jnp.arange(t, dtype=jnp.int32)[None, :]
    rel = jnp.stack([t * d + key - qry for d in range(-2, 3)])
    bkt = _rel_bucket_index(rel).astype(jnp.int32)
    return pl.pallas_call(
        _bias_tiles_kernel,
        grid=(heads,),
        in_specs=[
            pl.BlockSpec(memory_space=pltpu.SMEM),
            pl.BlockSpec((5, t, t), lambda h: (0, 0, 0)),
        ],
        out_specs=pl.BlockSpec((1, 5, t, t), lambda h: (h, 0, 0, 0)),
        out_shape=jax.ShapeDtypeStruct((heads, 5, t, t), F32),
        compiler_params=_cparams(("arbitrary",)),
        name="bias_tiles",
    )(rel_bias.astype(F32), bkt)


def _diff_attn_kernel(lam_ref, q_ref, k_ref, v_ref, tiles_ref, sw_ref, out_ref, vaug_scr,
                      s0_scr, s1_scr, m0_scr, m1_scr, e0_scr, e1_scr, *, lam_init, tq):
    seq, d2 = q_ref.shape
    t = BIAS_TILE
    s_scr, m_scr, e_scr = (s0_scr, s1_scr), (m0_scr, m1_scr), (e0_scr, e1_scr)

    vaug_scr[:d2, :] = v_ref[...].T
    vaug_scr[d2:, :] = jnp.ones((vaug_scr.shape[0] - d2, seq), BF16)

    lv = lam_ref[...].astype(F32)
    lam = (jnp.exp(jnp.sum(lv[0:1] * lv[1:2], axis=-1, keepdims=True))
           - jnp.exp(jnp.sum(lv[2:3] * lv[3:4], axis=-1, keepdims=True)) + lam_init)

    c = DA_HEAD_DIM ** -0.5 * LOG2E
    lane = lax.broadcasted_iota(jnp.int32, (tq, d2), 1)
    first = lane < DA_HEAD_DIM
    sw = sw_ref[...].astype(F32) * (1.0 - lam_init)

    def scores(n, slot):
        r0 = n * tq if isinstance(n, int) else pl.multiple_of(n * tq, tq)
        q = q_ref[pl.ds(r0, tq), :]
        zero = jnp.zeros_like(q)
        cols = []
        for ib in range(tq // t):
            d = [jnp.clip(jb - (n * (tq // t) + ib), -2, 2) + 2 for jb in range(seq // t)]
            cols.append(jnp.concatenate([tiles_ref[0, dj] for dj in d], axis=0))
        bias = jnp.concatenate(cols, axis=1)
        qq = jnp.concatenate([jnp.where(first, q, zero), jnp.where(first, zero, q)], axis=0)
        st = _dot_nt(k_ref[...], qq) * c
        st = jnp.concatenate([st[:, :tq] + bias, st[:, tq:] + bias], axis=1)
        s_scr[slot][...] = st
        m_scr[slot][...] = jnp.max(st, axis=0, keepdims=True)

    def numerators(slot):
        e_scr[slot][...] = jnp.exp2(s_scr[slot][...] - m_scr[slot][...]).astype(BF16)

    def values(n, slot):
        r0 = n * tq if isinstance(n, int) else pl.multiple_of(n * tq, tq)
        ot = _dot(vaug_scr[...], e_scr[slot][...])
        on = ot[:d2] / ot[d2:d2 + 1]
        o = on[:, :tq] - lam * on[:, tq:]
        yt = o * lax.rsqrt(jnp.mean(o * o, axis=0, keepdims=True) + EPS)
        out_ref[pl.ds(r0, tq), :] = (yt.T * sw).astype(out_ref.dtype)

    nq = seq // tq
    assert nq % 2 == 0 and nq >= 4
    scores(0, 0)
    scores(1, 1)
    numerators(0)

    def pair(i, carry):
        scores(2 * i + 2, 0)
        numerators(1)
        values(2 * i, 0)
        scores(2 * i + 3, 1)
        numerators(0)
        values(2 * i + 1, 1)
        return carry

    lax.fori_loop(0, nq // 2 - 1, pair, 0)
    numerators(1)
    values(nq - 2, 0)
    values(nq - 1, 1)


def _diff_attn(main, lam_vecs, tiles, subln_w, *, batch, seq, heads, lam_init, col_q, col_k, col_v, tq):
    d2 = 2 * DA_HEAD_DIM
    t = BIAS_TILE
    kern = functools.partial(_diff_attn_kernel, lam_init=lam_init, tq=tq)
    return pl.pallas_call(
        kern,
        grid=(heads, batch),
        in_specs=[
            pl.BlockSpec(lam_vecs.shape, lambda h, b: (0, 0)),
            pl.BlockSpec((None, seq, d2), lambda h, b: (col_q + h, b, 0)),
            pl.BlockSpec((None, seq, d2), lambda h, b: (col_k + h, b, 0)),
            pl.BlockSpec((None, seq, d2), lambda h, b: (col_v + h, b, 0)),
            pl.BlockSpec((1, 5, t, t), lambda h, b: (h, 0, 0, 0)),
            pl.BlockSpec((1, d2), lambda h, b: (0, 0)),
        ],
        out_specs=pl.BlockSpec((None, seq, d2), lambda h, b: (h, b, 0)),
        out_shape=jax.ShapeDtypeStruct((heads, batch * seq, d2), BF16),
        scratch_shapes=[pltpu.VMEM((d2 + 2 * SUBLANES, seq), BF16),
                        pltpu.VMEM((seq, 2 * tq), F32), pltpu.VMEM((seq, 2 * tq), F32),
                        pltpu.VMEM((1, 2 * tq), F32), pltpu.VMEM((1, 2 * tq), F32),
                        pltpu.VMEM((seq, 2 * tq), BF16), pltpu.VMEM((seq, 2 * tq), BF16)],
        compiler_params=_cparams(("parallel", "parallel")),
        name="diff_attn",
    )(lam_vecs, main, main, main, tiles, subln_w.reshape(1, d2))


def _out_proj_kernel(x_ref, a_ref, b_ref, wa_ref, wb_ref, nw_ref, wd_ref, h_ref, u_ref, wd_out_ref):
    def heads_to_lanes(ref):
        return jnp.concatenate([ref[c] for c in range(ref.shape[0])], axis=1)

    h = x_ref[...] + _dot(heads_to_lanes(a_ref), wa_ref[...]) + _dot(heads_to_lanes(b_ref), wb_ref[...])
    h_ref[...] = h
    ms = jnp.mean(h * h, axis=-1, keepdims=True)
    u_ref[...] = (h * lax.rsqrt(ms + EPS) * nw_ref[...]).astype(u_ref.dtype)
    wd_out_ref[...] = wd_ref[...].astype(wd_out_ref.dtype)


def _out_proj(x, a, b, w, nw, wd, layer, tm):
    m, d = x.shape
    ka, kb = a.shape[0] * a.shape[2], b.shape[0] * b.shape[2]
    assert ka == kb and w.shape == (ka + kb, d)
    steps = m // tm
    f = wd.shape[1]
    rows = f // steps
    assert rows * steps == f and rows % (2 * SUBLANES) == 0
    resident = pl.Buffered(1)
    return pl.pallas_call(
        _out_proj_kernel,
        grid=(steps,),
        in_specs=[
            pl.BlockSpec((tm, d), lambda i: (i, 0)),
            pl.BlockSpec((a.shape[0], tm, a.shape[2]), lambda i: (0, i, 0)),
            pl.BlockSpec((b.shape[0], tm, b.shape[2]), lambda i: (0, i, 0)),
            pl.BlockSpec((ka, d), lambda i: (0, 0), pipeline_mode=resident),
            pl.BlockSpec((kb, d), lambda i: (1, 0), pipeline_mode=resident),
            pl.BlockSpec((1, d), lambda i: (0, 0)),
            pl.BlockSpec((None, rows, wd.shape[2]), lambda i: (layer, i, 0)),
        ],
        out_specs=[pl.BlockSpec((tm, d), lambda i: (i, 0)), pl.BlockSpec((tm, d), lambda i: (i, 0)),
                   pl.BlockSpec((rows, wd.shape[2]), lambda i: (i, 0))],
        out_shape=[jax.ShapeDtypeStruct((m, d), F32), jax.ShapeDtypeStruct((m, d), BF16),
                   jax.ShapeDtypeStruct(wd.shape[1:], BF16)],
        compiler_params=_cparams(("parallel",)),
        name="out_proj",
    )(x, a, b, w, w, nw.reshape(1, d), wd)


def _ffn_up_kernel(u_ref, wg_ref, wu_ref, o_ref, wg_scr, wu_scr):
    @pl.when(pl.program_id(1) == 0)
    def _():
        wg_scr[...] = wg_ref[...].astype(BF16)
        wu_scr[...] = wu_ref[...].astype(BF16)

    sub = min(512, u_ref.shape[0])
    for r in range(u_ref.shape[0] // sub):
        rows = slice(r * sub, (r + 1) * sub)
        u = u_ref[rows, :]
        g = _dot(u, wg_scr[...])
        up = _dot(u, wu_scr[...])
        o_ref[rows, :] = (_silu(g) * up).astype(o_ref.dtype)


def _ffn_up(u, wg, wu, layer, tm, tn):
    m, d = u.shape
    n = wg.shape[2]
    return pl.pallas_call(
        _ffn_up_kernel,
        grid=(n // tn, m // tm),
        in_specs=[
            pl.BlockSpec((tm, d), lambda j, i: (i, 0)),
            pl.BlockSpec((None, d, tn), lambda j, i: (layer, 0, j)),
            pl.BlockSpec((None, d, tn), lambda j, i: (layer, 0, j)),
        ],
        out_specs=pl.BlockSpec((tm, tn), lambda j, i: (i, j)),
        out_shape=jax.ShapeDtypeStruct((m, n), BF16),
        scratch_shapes=[pltpu.VMEM((d, tn), BF16), pltpu.VMEM((d, tn), BF16)],
        compiler_params=_cparams(("parallel", "arbitrary")),
        name="ffn_up",
    )(u, wg, wu)


def _ffn_down_kernel(a_ref, w_ref, h_ref, fw_ref, o_ref):
    y = h_ref[...] + _dot(a_ref[...], w_ref[...])
    ms = jnp.mean(y * y, axis=-1, keepdims=True)
    o_ref[...] = y * lax.rsqrt(ms + EPS) * fw_ref[...]


def _ffn_down(a, w, h, fw, tm):
    m, f = a.shape
    d = w.shape[1]
    return pl.pallas_call(
        _ffn_down_kernel,
        grid=(m // tm,),
        in_specs=[
            pl.BlockSpec((tm, f), lambda i: (i, 0)),
            pl.BlockSpec((f, d), lambda i: (0, 0), pipeline_mode=pl.Buffered(1)),
            pl.BlockSpec((tm, d), lambda i: (i, 0)),
            pl.BlockSpec((1, d), lambda i: (0, 0)),
        ],
        out_specs=pl.BlockSpec((tm, d), lambda i: (i, 0)),
        out_shape=jax.ShapeDtypeStruct((m, d), F32),
        compiler_params=_cparams(("parallel",)),
        name="ffn_down",
    )(a, w, h, fw.reshape(1, d))


def kernel(x, norm1_w, w_in, hg_lb_logits, hg_onorm_w, lambda_q1, lambda_k1, lambda_q2, lambda_k2,
           da_subln_w, rel_bias, w_out, norm2_w, w_gate, w_up, w_down, final_norm_w):
    batch, seq, d_model = x.shape
    depth = w_in.shape[0]
    hg_width = hg_lb_logits.shape[-1]
    da_width = d_model - hg_width
    hg_heads = hg_width // HG_HEAD_DIM
    da_heads = da_width // (2 * DA_HEAD_DIM)
    assert w_in.shape[2] == 5 * hg_width + 3 * da_width
    assert seq % HG_CHUNK == 0 and seq % BIAS_TILE == 0
    m = batch * seq
    blk = LANES

    tiles = _bias_tiles(rel_bias)
    h = x.reshape(m, d_model)
    for l in range(depth):
        main, gates = _in_proj(h, norm1_w[l], w_in[l].astype(BF16), 2 * hg_width, 4 * hg_width,
                               TILES["in_proj_m"], TILES["in_proj_n"])

        o_hg = _hgrn(main, gates, hg_lb_logits, hg_onorm_w[l], batch=batch, seq=seq, heads=hg_heads,
                     layer=l, col_q=0, col_i=hg_width // blk, col_g=2 * hg_width // blk)

        lam_init = 0.8 - 0.6 * math.exp(-0.3 * l)
        lam_vecs = jnp.stack([lambda_q1[l], lambda_k1[l], lambda_q2[l], lambda_k2[l]]).astype(F32)
        base = 3 * hg_width // blk
        o_da = _diff_attn(main, lam_vecs, tiles, da_subln_w[l], batch=batch, seq=seq, heads=da_heads,
                          lam_init=lam_init, col_q=base, col_k=base + da_width // blk,
                          col_v=base + 2 * da_width // blk, tq=TILES["attn_q"])

        h, u2, w_down_bf16 = _out_proj(h, o_hg, o_da, w_out[l].astype(BF16), norm2_w[l], w_down, l,
                                       TILES["out_proj_m"])

        act = _ffn_up(u2, w_gate, w_up, l, TILES["ffn_up_m"], TILES["ffn_up_n"])
        last = l == depth - 1
        assert last, "final norm is fused into the last layer's down projection"
        h = _ffn_down(act, w_down_bf16, h, final_norm_w, TILES["ffn_down_m"])
    return h.reshape(batch, seq, d_model)
```

```python
import functools
import math

import numpy as np
import jax
import jax.numpy as jnp
from jax import lax
from jax.experimental import pallas as pl
from jax.experimental.pallas import tpu as pltpu

F32 = jnp.float32
BF16 = jnp.bfloat16

EPS = 1e-6
LOG2E = math.log2(math.e)
HG_HEAD_DIM = 128
DA_HEAD_DIM = 64
N_BUCKETS = 32
MAX_DISTANCE = 128

LANES = 128
SUBLANES = 8
HG_CHUNK = 64
HG_LEVELS = 6
HG_GROUP = 2
HG_UNROLL = 16
BIAS_TILE = 128
VMEM_LIMIT = 56 * 1024 * 1024

TILES = {
    "in_proj_m": 1024, "in_proj_n": 1024,
    "attn_q": 256,
    "out_proj_m": 512,
    "ffn_up_m": 2048, "ffn_up_n": 512,
    "ffn_down_m": 512,
}


def _cparams(sem, flags=None):
    return pltpu.CompilerParams(dimension_semantics=sem, vmem_limit_bytes=VMEM_LIMIT, flags=flags)


def _dot(a, b):
    return jnp.dot(a, b, preferred_element_type=F32)


def _dot_nt(a, b):
    return lax.dot_general(a, b, (((1,), (1,)), ((), ())), preferred_element_type=F32)


def _dot_tn(a, b):
    return lax.dot_general(a, b, (((0,), (0,)), ((), ())), preferred_element_type=F32)


def _sigmoid(x):
    return 1.0 / (1.0 + jnp.exp(-x))


def _silu(x):
    return x * _sigmoid(x)


def _in_proj_kernel(*refs, n_main, first):
    if first:
        x_ref, nw_ref, w_ref, main_ref, gates_ref, wb_ref, u_ref = refs
    else:
        x_ref, nw_ref, w_ref, _, _, main_ref, gates_ref, u_ref = refs
    j = pl.program_id(1)

    @pl.when(j == 0)
    def _():
        x = x_ref[...].astype(F32)
        ms = jnp.mean(x * x, axis=-1, keepdims=True)
        u_ref[...] = (x * lax.rsqrt(ms + EPS) * nw_ref[...]).astype(BF16)

    if first:
        w = w_ref[...].astype(BF16)
        wb_ref[...] = w
    else:
        w = w_ref[...]

    def emit(o_ref):
        r = _dot(u_ref[...], w).astype(o_ref.dtype)
        for c in range(o_ref.shape[0]):
            o_ref[c] = r[:, c * LANES:(c + 1) * LANES]

    @pl.when(j < n_main)
    def _():
        emit(main_ref)

    @pl.when(j >= n_main)
    def _():
        emit(gates_ref)


def _in_proj(x, nw, w, layer, gate_lo, gate_hi, tm, tn):
    m, d = x.shape
    n = w.shape[2]
    assert gate_lo % tn == 0 and gate_hi % tn == 0
    n_before, n_gate = gate_lo // tn, (gate_hi - gate_lo) // tn
    n_main = n // tn - n_gate
    n_main_cols = n_main * tn

    def w_col(j):
        return jnp.where(j < n_before, j, jnp.where(j < n_main, j + n_gate, j - (n_main - n_before)))

    def out_specs(row0):
        return [
            pl.BlockSpec((tn // LANES, tm, LANES), lambda i, j: (jnp.minimum(j, n_main - 1), i + row0, 0)),
            pl.BlockSpec((tn // LANES, tm, LANES), lambda i, j: (jnp.maximum(j - n_main, 0), i + row0, 0)),
        ]

    out_shape = [
        jax.ShapeDtypeStruct((n_main_cols // LANES, m, LANES), BF16),
        jax.ShapeDtypeStruct(((n - n_main_cols) // LANES, m, LANES), F32),
    ]
    nw2 = nw.reshape(1, d)
    main, gates, wb = pl.pallas_call(
        functools.partial(_in_proj_kernel, n_main=n_main, first=True),
        grid=(1, n // tn),
        in_specs=[
            pl.BlockSpec((tm, d), lambda i, j: (0, 0), pipeline_mode=pl.Buffered(1)),
            pl.BlockSpec((1, d), lambda i, j: (0, 0)),
            pl.BlockSpec((None, d, tn), lambda i, j: (layer, 0, w_col(j))),
        ],
        out_specs=out_specs(0) + [pl.BlockSpec((d, tn), lambda i, j: (0, w_col(j)))],
        out_shape=out_shape + [jax.ShapeDtypeStruct((d, n), BF16)],
        scratch_shapes=[pltpu.VMEM((tm, d), BF16)],
        compiler_params=_cparams(("arbitrary", "arbitrary")),
        name="in_proj_first",
    )(x, nw2, w)
    return pl.pallas_call(
        functools.partial(_in_proj_kernel, n_main=n_main, first=False),
        grid=(m // tm - 1, n // tn),
        in_specs=[
            pl.BlockSpec((tm, d), lambda i, j: (i + 1, 0)),
            pl.BlockSpec((1, d), lambda i, j: (0, 0)),
            pl.BlockSpec((d, tn), lambda i, j: (0, w_col(j))),
            pl.BlockSpec(memory_space=pl.ANY),
            pl.BlockSpec(memory_space=pl.ANY),
        ],
        out_specs=out_specs(1),
        out_shape=out_shape,
        input_output_aliases={3: 0, 4: 1},
        scratch_shapes=[pltpu.VMEM((tm, d), BF16)],
        compiler_params=_cparams(("parallel", "arbitrary")),
        name="in_proj",
    )(x, nw2, wb, main, gates)


def _hgrn_constants():
    c, nl, g = HG_CHUNK, HG_LEVELS, HG_GROUP
    idx = np.arange(g * c)
    same_chunk = (idx[:, None] // c) == (idx[None, :] // c)
    tri = (same_chunk & (idx[:, None] >= idx[None, :])).astype(np.float32)
    masks = np.zeros((nl + 1, g * c, g * c), np.float32)
    masks[0] = np.eye(g * c, dtype=np.float32)
    for lvl in range(nl):
        m = 1 << lvl
        same_pair = (idx[:, None] // (2 * m)) == (idx[None, :] // (2 * m))
        other_sibling = (idx[:, None] // m) != (idx[None, :] // m)
        masks[lvl + 1] = (same_pair & other_sibling).astype(np.float32)
    return tri, masks


def _level_operands(lvl, j, q, f_f, f_b, k_f, k_b, beta_f, bx_b, odd, hi2, hi4):
    sl = slice(j * SUBLANES, (j + 1) * SUBLANES)
    qj, ffj, fbj, kfj, kbj = q[sl], f_f[sl], f_b[sl], k_f[sl], k_b[sl]
    if lvl == 0:
        return qj * jnp.where(odd, ffj, fbj), jnp.where(odd, kbj, kfj)
    if lvl == 1:
        up_f, dn_f = pltpu.roll(ffj, SUBLANES - 1, axis=0), pltpu.roll(ffj, 1, axis=0)
        up_b, dn_b = pltpu.roll(fbj, SUBLANES - 1, axis=0), pltpu.roll(fbj, 1, axis=0)
        fq = jnp.where(hi2, jnp.where(odd, ffj * dn_f, ffj), jnp.where(odd, fbj, fbj * up_b))
        ks = jnp.where(hi2, jnp.where(odd, kbj * dn_b, kbj), jnp.where(odd, kfj, kfj * up_f))
        return qj * fq, ks
    bfj, bbj = beta_f[sl], bx_b[sl]
    if lvl == 2:
        r = j * SUBLANES + SUBLANES // 2
        ef = jnp.exp2(-jnp.abs(bfj - beta_f[r - 1:r]))
        eb = jnp.exp2(-jnp.abs(bbj - bx_b[r:r + 1]))
        return qj * jnp.where(hi4, ef, eb), jnp.where(hi4, kbj * eb, kfj * ef)
    mb = (1 << lvl) // SUBLANES
    r = ((j // (2 * mb)) * 2 * mb + mb) * SUBLANES
    ref_f, ref_b = beta_f[r - 1:r], bx_b[r:r + 1]
    if (j // mb) & 1:
        return qj * jnp.exp2(bfj - ref_f), kbj * jnp.exp2(bbj - ref_b)
    return qj * jnp.exp2(ref_b - bbj), kfj * jnp.exp2(ref_f - bfj)


def _hgrn_kernel(lg_ref, hq_ref, hi_ref, hg_ref, gf_ref, gb_ref, ow_ref, tri_ref, mk_ref,
                 out_ref, o_scr, qi_scr, ut_scr, dec_scr, st_scr, g_scr, *, layer, n_slots):
    c, nl, dh = HG_CHUNK, HG_LEVELS, HG_HEAD_DIM
    seq = hq_ref.shape[0]
    nc = seq // c

    lg = lg_ref[...].astype(F32)
    lbs = []
    for d in range(2):
        rows = lg[d * n_slots:(d + 1) * n_slots]
        e = jnp.exp(rows - jnp.max(rows, axis=0, keepdims=True))
        lbs.append(jnp.sum(e[:layer + 1], axis=0, keepdims=True) / jnp.sum(e, axis=0, keepdims=True))
    lb_f, lb_b = lbs

    pos = lax.broadcasted_iota(jnp.int32, (SUBLANES, dh), 0)
    odd = (pos & 1) != 0
    hi2 = (pos & 2) != 0
    hi4 = (pos & 4) != 0

    def split(x):
        hi = x.astype(BF16)
        return hi, (x - hi.astype(F32)).astype(BF16)

    gc = HG_GROUP * c
    ng = nc // HG_GROUP

    def gates(gi, carry):
        r0 = pl.multiple_of(gi * gc, gc)
        f_f = lb_f + (1.0 - lb_f) * _sigmoid(gf_ref[pl.ds(r0, gc), :])
        f_b = lb_b + (1.0 - lb_b) * _sigmoid(gb_ref[pl.ds(r0, gc), :])
        lf_f = jnp.log2(f_f)
        lf_b = jnp.log2(f_b)
        pre = _dot(tri_ref[...], jnp.concatenate(split(lf_f) + split(lf_b), axis=1))
        g_scr[0, pl.ds(r0, gc), :] = f_f
        g_scr[1, pl.ds(r0, gc), :] = f_b
        g_scr[2, pl.ds(r0, gc), :] = pre[:, :dh] + pre[:, dh:2 * dh]
        g_scr[3, pl.ds(r0, gc), :] = pre[:, 2 * dh:3 * dh] + pre[:, 3 * dh:] - lf_b
        return carry

    lax.fori_loop(0, ng, gates, 0, unroll=HG_UNROLL // HG_GROUP)

    def per_chunk_rows(x, row):
        return jnp.concatenate([jnp.broadcast_to(x[g * c + row:g * c + row + 1], (c, dh))
                                for g in range(HG_GROUP)], axis=0)

    def intra(gi, carry):
        r0 = pl.multiple_of(gi * gc, gc)
        q = _silu(hq_ref[pl.ds(r0, gc), :].astype(F32))
        v = hi_ref[pl.ds(r0, gc), :]
        f_f = g_scr[0, pl.ds(r0, gc), :]
        f_b = g_scr[1, pl.ds(r0, gc), :]
        beta_f = g_scr[2, pl.ds(r0, gc), :]
        bx_b = g_scr[3, pl.ds(r0, gc), :]
        k_f = 1.0 - f_f
        k_b = 1.0 - f_b
        tot_f = per_chunk_rows(beta_f, c - 1)
        tot_b = per_chunk_rows(bx_b + jnp.log2(f_b), c - 1)

        a = mk_ref[0] * _dot_nt(q.astype(BF16), (k_f + k_b).astype(BF16))
        for lvl in range(nl):
            ops = [_level_operands(lvl, j, q, f_f, f_b, k_f, k_b, beta_f, bx_b, odd, hi2, hi4)
                   for j in range(gc // SUBLANES)]
            qs = jnp.concatenate([o[0] for o in ops], axis=0)
            ks = jnp.concatenate([o[1] for o in ops], axis=0)
            a = a + mk_ref[lvl + 1] * _dot_nt(qs.astype(BF16), ks.astype(BF16))
        o_scr[pl.ds(r0, gc), :] = _dot(a.astype(BF16), v)

        qi = jnp.concatenate([q * jnp.exp2(beta_f), q * jnp.exp2(tot_b - bx_b)], axis=1)
        qi_scr[pl.ds(r0, gc), :] = qi.astype(BF16)
        ks = jnp.concatenate([k_f * jnp.exp2(tot_f - beta_f), k_b * jnp.exp2(bx_b)], axis=1).astype(BF16)
        zero = jnp.zeros((c, dh), BF16)
        vd = jnp.concatenate([jnp.concatenate([v[g * c:(g + 1) * c] if h == g else zero
                                               for h in range(HG_GROUP)], axis=1)
                              for g in range(HG_GROUP)], axis=0)
        ut = _dot_tn(vd, ks)
        dec = jnp.exp2(jnp.concatenate([tot_f, tot_b], axis=1))
        for g in range(HG_GROUP):
            ut_scr[gi * HG_GROUP + g] = ut[g * dh:(g + 1) * dh]
            dec_scr[gi * HG_GROUP + g] = dec[g * c:g * c + 8]
        return carry

    lax.fori_loop(0, ng, intra, 0, unroll=HG_UNROLL // HG_GROUP)

    def states(i, carry):
        st_f, st_b = carry
        cf = i
        cb = nc - 1 - i
        st_scr[cf, :, :dh] = st_f.astype(BF16)
        st_scr[cb, :, dh:] = st_b.astype(BF16)
        st_f = st_f * dec_scr[cf, 0:1, :dh] + ut_scr[cf, :, :dh]
        st_b = st_b * dec_scr[cb, 0:1, dh:] + ut_scr[cb, :, dh:]
        return st_f, st_b

    zero = jnp.zeros((dh, dh), F32)
    lax.fori_loop(0, nc, states, (zero, zero))

    ow = ow_ref[...].astype(F32)

    def finish(ci, carry):
        r0 = pl.multiple_of(ci * c, c)
        o = o_scr[pl.ds(r0, c), :] + _dot_nt(qi_scr[pl.ds(r0, c), :], st_scr[ci])
        y = o * lax.rsqrt(jnp.mean(o * o, axis=-1, keepdims=True) + EPS) * ow
        out_ref[pl.ds(r0, c), :] = (y * _silu(hg_ref[pl.ds(r0, c), :].astype(F32))).astype(out_ref.dtype)
        return carry

    lax.fori_loop(0, nc, finish, 0, unroll=HG_UNROLL)


def _hgrn(main, gates, lb_logits, onorm_w, *, batch, seq, heads, layer, col_q, col_i, col_g):
    dh, c = HG_HEAD_DIM, HG_CHUNK
    n_slots = lb_logits.shape[1]
    lg = lb_logits.reshape(2 * n_slots, heads * dh)
    tri, masks = _hgrn_constants()
    nc = seq // c
    kern = functools.partial(_hgrn_kernel, layer=layer, n_slots=n_slots)
    return pl.pallas_call(
        kern,
        grid=(batch, heads),
        in_specs=[
            pl.BlockSpec((2 * n_slots, dh), lambda b, h: (0, h)),
            pl.BlockSpec((None, seq, dh), lambda b, h: (col_q + h, b, 0)),
            pl.BlockSpec((None, seq, dh), lambda b, h: (col_i + h, b, 0)),
            pl.BlockSpec((None, seq, dh), lambda b, h: (col_g + h, b, 0)),
            pl.BlockSpec((None, seq, dh), lambda b, h: (h, b, 0)),
            pl.BlockSpec((None, seq, dh), lambda b, h: (heads + h, b, 0)),
            pl.BlockSpec((1, dh), lambda b, h: (0, 0)),
            pl.BlockSpec(tri.shape, lambda b, h: (0, 0)),
            pl.BlockSpec(masks.shape, lambda b, h: (0, 0, 0)),
        ],
        out_specs=pl.BlockSpec((None, seq, dh), lambda b, h: (h, b, 0)),
        out_shape=jax.ShapeDtypeStruct((heads, batch * seq, dh), BF16),
        scratch_shapes=[
            pltpu.VMEM((seq, dh), F32),
            pltpu.VMEM((seq, 2 * dh), BF16),
            pltpu.VMEM((nc, dh, 2 * dh), F32),
            pltpu.VMEM((nc, 8, 2 * dh), F32),
            pltpu.VMEM((nc, dh, 2 * dh), BF16),
            pltpu.VMEM((4, seq, dh), F32),
        ],
        compiler_params=_cparams(("parallel", "parallel")),
        name="hgrn2",
    )(lg, main, main, main, gates, gates, onorm_w.reshape(1, dh),
      jnp.asarray(tri, BF16), jnp.asarray(masks, F32))


def _rel_bucket_index(rel):
    nb = N_BUCKETS // 2
    max_exact = nb // 2
    ret = jnp.where(rel > 0, nb, 0)
    n = jnp.abs(rel)
    nf = jnp.maximum(n, 1).astype(jnp.float32)
    large = max_exact + (jnp.log(nf / max_exact) / math.log(MAX_DISTANCE / max_exact)
                         * (nb - max_exact)).astype(jnp.int32)
    large = jnp.minimum(large, nb - 1)
    return ret + jnp.where(n < max_exact, n, large)


def _bias_tiles_kernel(tbl_ref, bkt_ref, out_ref):
    h = pl.program_id(0)
    bkt = bkt_ref[...]
    acc = jnp.zeros(bkt.shape, F32)
    for cidx in range(N_BUCKETS):
        acc = jnp.where(bkt == cidx, tbl_ref[cidx, h], acc)
    out_ref[0] = acc * LOG2E


def _bias_tiles(rel_bias):
    t = BIAS_TILE
    assert t >= MAX_DISTANCE
    heads = rel_bias.shape[1]
    key = jnp.arange(t, dtype=jnp.int32)[:, None]
    qry = jnp.arange(t, dtype=jnp.int32)[None, :]
    rel = jnp.stack([t * d + key - qry for d in range(-2, 3)])
    bkt = _rel_bucket_index(rel).astype(jnp.int32)
    return pl.pallas_call(
        _bias_tiles_kernel,
        grid=(heads,),
        in_specs=[
            pl.BlockSpec(memory_space=pltpu.SMEM),
            pl.BlockSpec((5, t, t), lambda h: (0, 0, 0)),
        ],
        out_specs=pl.BlockSpec((1, 5, t, t), lambda h: (h, 0, 0, 0)),
        out_shape=jax.ShapeDtypeStruct((heads, 5, t, t), F32),
        compiler_params=_cparams(("arbitrary",)),
        name="bias_tiles",
    )(rel_bias.astype(F32), bkt)


def _diff_attn_kernel(lam_ref, q_ref, k_ref, v_ref, tiles_ref, sw_ref, out_ref, vaug_scr,
                      s0_scr, s1_scr, m0_scr, m1_scr, e0_scr, e1_scr, *, lam_init, tq):
    seq, d2 = q_ref.shape
    t = BIAS_TILE
    s_scr, m_scr, e_scr = (s0_scr, s1_scr), (m0_scr, m1_scr), (e0_scr, e1_scr)

    vaug_scr[:d2, :] = v_ref[...].T
    vaug_scr[d2:, :] = jnp.ones((vaug_scr.shape[0] - d2, seq), BF16)

    lv = lam_ref[...].astype(F32)
    lam = (jnp.exp(jnp.sum(lv[0:1] * lv[1:2], axis=-1, keepdims=True))
           - jnp.exp(jnp.sum(lv[2:3] * lv[3:4], axis=-1, keepdims=True)) + lam_init)

    c = DA_HEAD_DIM ** -0.5 * LOG2E
    lane = lax.broadcasted_iota(jnp.int32, (tq, d2), 1)
    first = lane < DA_HEAD_DIM
    sw = sw_ref[...].astype(F32) * (1.0 - lam_init)

    def scores(n, slot):
        r0 = n * tq if isinstance(n, int) else pl.multiple_of(n * tq, tq)
        q = q_ref[pl.ds(r0, tq), :]
        zero = jnp.zeros_like(q)
        cols = []
        for ib in range(tq // t):
            d = [jnp.clip(jb - (n * (tq // t) + ib), -2, 2) + 2 for jb in range(seq // t)]
            cols.append(jnp.concatenate([tiles_ref[0, dj] for dj in d], axis=0))
        bias = jnp.concatenate(cols, axis=1)
        qq = jnp.concatenate([jnp.where(first, q, zero), jnp.where(first, zero, q)], axis=0)
        st = _dot_nt(k_ref[...], qq) * c
        st = jnp.concatenate([st[:, :tq] + bias, st[:, tq:] + bias], axis=1)
        s_scr[slot][...] = st
        m_scr[slot][...] = jnp.max(st, axis=0, keepdims=True)

    def numerators(slot):
        e_scr[slot][...] = jnp.exp2(s_scr[slot][...] - m_scr[slot][...]).astype(BF16)

    def values(n, slot):
        r0 = n * tq if isinstance(n, int) else pl.multiple_of(n * tq, tq)
        ot = _dot(vaug_scr[...], e_scr[slot][...])
        on = ot[:d2] / ot[d2:d2 + 1]
        o = on[:, :tq] - lam * on[:, tq:]
        yt = o * lax.rsqrt(jnp.mean(o * o, axis=0, keepdims=True) + EPS)
        out_ref[pl.ds(r0, tq), :] = (yt.T * sw).astype(out_ref.dtype)

    nq = seq // tq
    assert nq % 2 == 0 and nq >= 4
    scores(0, 0)
    scores(1, 1)
    numerators(0)

    def pair(i, carry):
        scores(2 * i + 2, 0)
        numerators(1)
        values(2 * i, 0)
        scores(2 * i + 3, 1)
        numerators(0)
        values(2 * i + 1, 1)
        return carry

    lax.fori_loop(0, nq // 2 - 1, pair, 0)
    numerators(1)
    values(nq - 2, 0)
    values(nq - 1, 1)


def _diff_attn(main, lam_vecs, tiles, subln_w, *, batch, seq, heads, lam_init, col_q, col_k, col_v, tq):
    d2 = 2 * DA_HEAD_DIM
    t = BIAS_TILE
    kern = functools.partial(_diff_attn_kernel, lam_init=lam_init, tq=tq)
    return pl.pallas_call(
        kern,
        grid=(heads, batch),
        in_specs=[
            pl.BlockSpec(lam_vecs.shape, lambda h, b: (0, 0)),
            pl.BlockSpec((None, seq, d2), lambda h, b: (col_q + h, b, 0)),
            pl.BlockSpec((None, seq, d2), lambda h, b: (col_k + h, b, 0)),
            pl.BlockSpec((None, seq, d2), lambda h, b: (col_v + h, b, 0)),
            pl.BlockSpec((1, 5, t, t), lambda h, b: (h, 0, 0, 0)),
            pl.BlockSpec((1, d2), lambda h, b: (0, 0)),
        ],
        out_specs=pl.BlockSpec((None, seq, d2), lambda h, b: (h, b, 0)),
        out_shape=jax.ShapeDtypeStruct((heads, batch * seq, d2), BF16),
        scratch_shapes=[pltpu.VMEM((d2 + 2 * SUBLANES, seq), BF16),
                        pltpu.VMEM((seq, 2 * tq), F32), pltpu.VMEM((seq, 2 * tq), F32),
                        pltpu.VMEM((1, 2 * tq), F32), pltpu.VMEM((1, 2 * tq), F32),
                        pltpu.VMEM((seq, 2 * tq), BF16), pltpu.VMEM((seq, 2 * tq), BF16)],
        compiler_params=_cparams(("parallel", "parallel")),
        name="diff_attn",
    )(lam_vecs, main, main, main, tiles, subln_w.reshape(1, d2))


def _out_proj_kernel(x_ref, a_ref, b_ref, wa_ref, wb_ref, nw_ref, wd_ref, h_ref, u_ref, wd_out_ref):
    def heads_to_lanes(ref):
        return jnp.concatenate([ref[c] for c in range(ref.shape[0])], axis=1)

    h = x_ref[...] + _dot(heads_to_lanes(a_ref), wa_ref[...]) + _dot(heads_to_lanes(b_ref), wb_ref[...])
    h_ref[...] = h
    ms = jnp.mean(h * h, axis=-1, keepdims=True)
    u_ref[...] = (h * lax.rsqrt(ms + EPS) * nw_ref[...]).astype(u_ref.dtype)
    wd_out_ref[...] = wd_ref[...].astype(wd_out_ref.dtype)


def _out_proj(x, a, b, w, nw, wd, layer, tm):
    m, d = x.shape
    ka, kb = a.shape[0] * a.shape[2], b.shape[0] * b.shape[2]
    assert ka == kb and w.shape == (ka + kb, d)
    steps = m // tm
    f = wd.shape[1]
    rows = f // steps
    assert rows * steps == f and rows % (2 * SUBLANES) == 0
    resident = pl.Buffered(1)
    return pl.pallas_call(
        _out_proj_kernel,
        grid=(steps,),
        in_specs=[
            pl.BlockSpec((tm, d), lambda i: (i, 0)),
            pl.BlockSpec((a.shape[0], tm, a.shape[2]), lambda i: (0, i, 0)),
            pl.BlockSpec((b.shape[0], tm, b.shape[2]), lambda i: (0, i, 0)),
            pl.BlockSpec((ka, d), lambda i: (0, 0), pipeline_mode=resident),
            pl.BlockSpec((kb, d), lambda i: (1, 0), pipeline_mode=resident),
            pl.BlockSpec((1, d), lambda i: (0, 0)),
            pl.BlockSpec((None, rows, wd.shape[2]), lambda i: (layer, i, 0)),
        ],
        out_specs=[pl.BlockSpec((tm, d), lambda i: (i, 0)), pl.BlockSpec((tm, d), lambda i: (i, 0)),
                   pl.BlockSpec((rows, wd.shape[2]), lambda i: (i, 0))],
        out_shape=[jax.ShapeDtypeStruct((m, d), F32), jax.ShapeDtypeStruct((m, d), BF16),
                   jax.ShapeDtypeStruct(wd.shape[1:], BF16)],
        compiler_params=_cparams(("parallel",)),
        name="out_proj",
    )(x, a, b, w, w, nw.reshape(1, d), wd)


def _ffn_up_kernel(u_ref, wg_ref, wu_ref, o_ref, wg_scr, wu_scr):
    @pl.when(pl.program_id(1) == 0)
    def _():
        wg_scr[...] = wg_ref[...].astype(BF16)
        wu_scr[...] = wu_ref[...].astype(BF16)

    sub = min(512, u_ref.shape[0])
    for r in range(u_ref.shape[0] // sub):
        rows = slice(r * sub, (r + 1) * sub)
        u = u_ref[rows, :]
        g = _dot(u, wg_scr[...])
        up = _dot(u, wu_scr[...])
        o_ref[rows, :] = (_silu(g) * up).astype(o_ref.dtype)


def _ffn_up(u, wg, wu, layer, tm, tn):
    m, d = u.shape
    n = wg.shape[2]
    return pl.pallas_call(
        _ffn_up_kernel,
        grid=(n // tn, m // tm),
        in_specs=[
            pl.BlockSpec((tm, d), lambda j, i: (i, 0)),
            pl.BlockSpec((None, d, tn), lambda j, i: (layer, 0, j)),
            pl.BlockSpec((None, d, tn), lambda j, i: (layer, 0, j)),
        ],
        out_specs=pl.BlockSpec((tm, tn), lambda j, i: (i, j)),
        out_shape=jax.ShapeDtypeStruct((m, n), BF16),
        scratch_shapes=[pltpu.VMEM((d, tn), BF16), pltpu.VMEM((d, tn), BF16)],
        compiler_params=_cparams(("parallel", "arbitrary")),
        name="ffn_up",
    )(u, wg, wu)


def _ffn_down_kernel(a_ref, w_ref, h_ref, fw_ref, o_ref):
    y = h_ref[...] + _dot(a_ref[...], w_ref[...])
    ms = jnp.mean(y * y, axis=-1, keepdims=True)
    o_ref[...] = y * lax.rsqrt(ms + EPS) * fw_ref[...]


def _ffn_down(a, w, h, fw, tm):
    m, f = a.shape
    d = w.shape[1]
    return pl.pallas_call(
        _ffn_down_kernel,
        grid=(m // tm,),
        in_specs=[
            pl.BlockSpec((tm, f), lambda i: (i, 0)),
            pl.BlockSpec((f, d), lambda i: (0, 0), pipeline_mode=pl.Buffered(1)),
            pl.BlockSpec((tm, d), lambda i: (i, 0)),
            pl.BlockSpec((1, d), lambda i: (0, 0)),
        ],
        out_specs=pl.BlockSpec((tm, d), lambda i: (i, 0)),
        out_shape=jax.ShapeDtypeStruct((m, d), F32),
        compiler_params=_cparams(("parallel",)),
        name="ffn_down",
    )(a, w, h, fw.reshape(1, d))


def kernel(x, norm1_w, w_in, hg_lb_logits, hg_onorm_w, lambda_q1, lambda_k1, lambda_q2, lambda_k2,
           da_subln_w, rel_bias, w_out, norm2_w, w_gate, w_up, w_down, final_norm_w):
    batch, seq, d_model = x.shape
    depth = w_in.shape[0]
    hg_width = hg_lb_logits.shape[-1]
    da_width = d_model - hg_width
    hg_heads = hg_width // HG_HEAD_DIM
    da_heads = da_width // (2 * DA_HEAD_DIM)
    assert w_in.shape[2] == 5 * hg_width + 3 * da_width
    assert seq % HG_CHUNK == 0 and seq % BIAS_TILE == 0
    m = batch * seq
    blk = LANES

    tiles = _bias_tiles(rel_bias)
    h = x.reshape(m, d_model)
    for l in range(depth):
        main, gates = _in_proj(h, norm1_w[l], w_in, l, 2 * hg_width, 4 * hg_width,
                               TILES["in_proj_m"], TILES["in_proj_n"])

        o_hg = _hgrn(main, gates, hg_lb_logits, hg_onorm_w[l], batch=batch, seq=seq, heads=hg_heads,
                     layer=l, col_q=0, col_i=hg_width // blk, col_g=2 * hg_width // blk)

        lam_init = 0.8 - 0.6 * math.exp(-0.3 * l)
        lam_vecs = jnp.stack([lambda_q1[l], lambda_k1[l], lambda_q2[l], lambda_k2[l]]).astype(F32)
        base = 3 * hg_width // blk
        o_da = _diff_attn(main, lam_vecs, tiles, da_subln_w[l], batch=batch, seq=seq, heads=da_heads,
                          lam_init=lam_init, col_q=base, col_k=base + da_width // blk,
                          col_v=base + 2 * da_width // blk, tq=TILES["attn_q"])

        h, u2, w_down_bf16 = _out_proj(h, o_hg, o_da, w_out[l].astype(BF16), norm2_w[l], w_down, l,
                                       TILES["out_proj_m"])

        act = _ffn_up(u2, w_gate, w_up, l, TILES["ffn_up_m"], TILES["ffn_up_n"])
        last = l == depth - 1
        assert last, "final norm is fused into the last layer's down projection"
        h = _ffn_down(act, w_down_bf16, h, final_norm_w, TILES["ffn_down_m"])
    return h.reshape(batch, seq, d_model)
```

```python
import functools
import math

import numpy as np
import jax
import jax.numpy as jnp
from jax import lax
from jax.experimental import pallas as pl
from jax.experimental.pallas import tpu as pltpu

F32 = jnp.float32
BF16 = jnp.bfloat16

EPS = 1e-6
LOG2E = math.log2(math.e)
HG_HEAD_DIM = 128
DA_HEAD_DIM = 64
N_BUCKETS = 32
MAX_DISTANCE = 128

LANES = 128
SUBLANES = 8
HG_CHUNK = 64
HG_LEVELS = 6
HG_GROUP = 2
HG_UNROLL = 16
BIAS_TILE = 128
VMEM_LIMIT = 56 * 1024 * 1024

TILES = {
    "in_proj_m": 1024, "in_proj_n": 1024,
    "attn_q": 256,
    "out_proj_m": 512,
    "ffn_up_m": 2048, "ffn_up_n": 512,
    "ffn_down_m": 512,
}


def _cparams(sem, flags=None):
    return pltpu.CompilerParams(dimension_semantics=sem, vmem_limit_bytes=VMEM_LIMIT, flags=flags)


def _dot(a, b):
    return jnp.dot(a, b, preferred_element_type=F32)


def _dot_nt(a, b):
    return lax.dot_general(a, b, (((1,), (1,)), ((), ())), preferred_element_type=F32)


def _dot_tn(a, b):
    return lax.dot_general(a, b, (((0,), (0,)), ((), ())), preferred_element_type=F32)


def _sigmoid(x):
    return 0.5 * jnp.tanh(0.5 * x) + 0.5


def _silu(x):
    h = 0.5 * x
    return h + h * jnp.tanh(h)


def _in_proj_kernel(x_ref, nw_ref, w_ref, main_ref, gates_ref, u_ref, *, n_main):
    j = pl.program_id(1)

    @pl.when(j == 0)
    def _():
        x = x_ref[...].astype(F32)
        ms = jnp.mean(x * x, axis=-1, keepdims=True)
        u_ref[...] = (x * lax.rsqrt(ms + EPS) * nw_ref[...]).astype(BF16)

    def emit(o_ref):
        r = _dot(u_ref[...], w_ref[...]).astype(o_ref.dtype)
        for c in range(o_ref.shape[0]):
            o_ref[c] = r[:, c * LANES:(c + 1) * LANES]

    @pl.when(j < n_main)
    def _():
        emit(main_ref)

    @pl.when(j >= n_main)
    def _():
        emit(gates_ref)


def _in_proj(x, nw, w, gate_lo, gate_hi, tm, tn):
    m, d = x.shape
    n = w.shape[1]
    assert gate_lo % tn == 0 and gate_hi % tn == 0
    n_before, n_gate = gate_lo // tn, (gate_hi - gate_lo) // tn
    n_main = n // tn - n_gate
    n_main_cols = n_main * tn

    def w_block(i, j):
        return 0, jnp.where(j < n_before, j, jnp.where(j < n_main, j + n_gate, j - (n_main - n_before)))

    kern = functools.partial(_in_proj_kernel, n_main=n_main)
    return pl.pallas_call(
        kern,
        grid=(m // tm, n // tn),
        in_specs=[
            pl.BlockSpec((tm, d), lambda i, j: (i, 0)),
            pl.BlockSpec((1, d), lambda i, j: (0, 0)),
            pl.BlockSpec((d, tn), w_block),
        ],
        out_specs=[
            pl.BlockSpec((tn // LANES, tm, LANES), lambda i, j: (jnp.minimum(j, n_main - 1), i, 0)),
            pl.BlockSpec((tn // LANES, tm, LANES), lambda i, j: (jnp.maximum(j - n_main, 0), i, 0)),
        ],
        out_shape=[
            jax.ShapeDtypeStruct((n_main_cols // LANES, m, LANES), BF16),
            jax.ShapeDtypeStruct(((n - n_main_cols) // LANES, m, LANES), F32),
        ],
        scratch_shapes=[pltpu.VMEM((tm, d), BF16)],
        compiler_params=_cparams(("parallel", "arbitrary")),
        name="in_proj",
    )(x, nw.reshape(1, d), w)


def _hgrn_constants():
    c, nl, g = HG_CHUNK, HG_LEVELS, HG_GROUP
    idx = np.arange(g * c)
    same_chunk = (idx[:, None] // c) == (idx[None, :] // c)
    tri = (same_chunk & (idx[:, None] >= idx[None, :])).astype(np.float32)
    masks = np.zeros((nl + 1, g * c, g * c), np.float32)
    masks[0] = np.eye(g * c, dtype=np.float32)
    for lvl in range(nl):
        m = 1 << lvl
        same_pair = (idx[:, None] // (2 * m)) == (idx[None, :] // (2 * m))
        other_sibling = (idx[:, None] // m) != (idx[None, :] // m)
        masks[lvl + 1] = (same_pair & other_sibling).astype(np.float32)
    return tri, masks


def _level_operands(lvl, j, q, f_f, f_b, k_f, k_b, beta_f, bx_b, odd, hi2, hi4):
    sl = slice(j * SUBLANES, (j + 1) * SUBLANES)
    qj, ffj, fbj, kfj, kbj = q[sl], f_f[sl], f_b[sl], k_f[sl], k_b[sl]
    if lvl == 0:
        return qj * jnp.where(odd, ffj, fbj), jnp.where(odd, kbj, kfj)
    if lvl == 1:
        up_f, dn_f = pltpu.roll(ffj, SUBLANES - 1, axis=0), pltpu.roll(ffj, 1, axis=0)
        up_b, dn_b = pltpu.roll(fbj, SUBLANES - 1, axis=0), pltpu.roll(fbj, 1, axis=0)
        fq = jnp.where(hi2, jnp.where(odd, ffj * dn_f, ffj), jnp.where(odd, fbj, fbj * up_b))
        ks = jnp.where(hi2, jnp.where(odd, kbj * dn_b, kbj), jnp.where(odd, kfj, kfj * up_f))
        return qj * fq, ks
    bfj, bbj = beta_f[sl], bx_b[sl]
    if lvl == 2:
        r = j * SUBLANES + SUBLANES // 2
        ef = jnp.exp2(-jnp.abs(bfj - beta_f[r - 1:r]))
        eb = jnp.exp2(-jnp.abs(bbj - bx_b[r:r + 1]))
        return qj * jnp.where(hi4, ef, eb), jnp.where(hi4, kbj * eb, kfj * ef)
    mb = (1 << lvl) // SUBLANES
    r = ((j // (2 * mb)) * 2 * mb + mb) * SUBLANES
    ref_f, ref_b = beta_f[r - 1:r], bx_b[r:r + 1]
    if (j // mb) & 1:
        return qj * jnp.exp2(bfj - ref_f), kbj * jnp.exp2(bbj - ref_b)
    return qj * jnp.exp2(ref_b - bbj), kfj * jnp.exp2(ref_f - bfj)


def _hgrn_kernel(lg_ref, hq_ref, hi_ref, hg_ref, gf_ref, gb_ref, ow_ref, tri_ref, mk_ref,
                 out_ref, o_scr, qi_scr, ut_scr, dec_scr, st_scr, g_scr, *, layer, n_slots):
    c, nl, dh = HG_CHUNK, HG_LEVELS, HG_HEAD_DIM
    seq = hq_ref.shape[0]
    nc = seq // c

    lg = lg_ref[...].astype(F32)
    lbs = []
    for d in range(2):
        rows = lg[d * n_slots:(d + 1) * n_slots]
        e = jnp.exp(rows - jnp.max(rows, axis=0, keepdims=True))
        lbs.append(jnp.sum(e[:layer + 1], axis=0, keepdims=True) / jnp.sum(e, axis=0, keepdims=True))
    lb_f, lb_b = lbs

    pos = lax.broadcasted_iota(jnp.int32, (SUBLANES, dh), 0)
    odd = (pos & 1) != 0
    hi2 = (pos & 2) != 0
    hi4 = (pos & 4) != 0

    def split(x):
        hi = x.astype(BF16)
        return hi, (x - hi.astype(F32)).astype(BF16)

    gc = HG_GROUP * c
    ng = nc // HG_GROUP

    def gates(gi, carry):
        r0 = pl.multiple_of(gi * gc, gc)
        f_f = lb_f + (1.0 - lb_f) * _sigmoid(gf_ref[pl.ds(r0, gc), :])
        f_b = lb_b + (1.0 - lb_b) * _sigmoid(gb_ref[pl.ds(r0, gc), :])
        lf_f = jnp.log2(f_f)
        lf_b = jnp.log2(f_b)
        pre = _dot(tri_ref[...], jnp.concatenate(split(lf_f) + split(lf_b), axis=1))
        g_scr[0, pl.ds(r0, gc), :] = f_f
        g_scr[1, pl.ds(r0, gc), :] = f_b
        g_scr[2, pl.ds(r0, gc), :] = pre[:, :dh] + pre[:, dh:2 * dh]
        g_scr[3, pl.ds(r0, gc), :] = pre[:, 2 * dh:3 * dh] + pre[:, 3 * dh:] - lf_b
        return carry

    lax.fori_loop(0, ng, gates, 0, unroll=HG_UNROLL // HG_GROUP)

    def per_chunk_rows(x, row):
        return jnp.concatenate([jnp.broadcast_to(x[g * c + row:g * c + row + 1], (c, dh))
                                for g in range(HG_GROUP)], axis=0)

    def intra(gi, carry):
        r0 = pl.multiple_of(gi * gc, gc)
        q = _silu(hq_ref[pl.ds(r0, gc), :].astype(F32))
        v = hi_ref[pl.ds(r0, gc), :]
        f_f = g_scr[0, pl.ds(r0, gc), :]
        f_b = g_scr[1, pl.ds(r0, gc), :]
        beta_f = g_scr[2, pl.ds(r0, gc), :]
        bx_b = g_scr[3, pl.ds(r0, gc), :]
        k_f = 1.0 - f_f
        k_b = 1.0 - f_b
        tot_f = per_chunk_rows(beta_f, c - 1)
        tot_b = per_chunk_rows(bx_b + jnp.log2(f_b), c - 1)

        a = mk_ref[0] * _dot_nt(q.astype(BF16), (k_f + k_b).astype(BF16))
        for lvl in range(nl):
            ops = [_level_operands(lvl, j, q, f_f, f_b, k_f, k_b, beta_f, bx_b, odd, hi2, hi4)
                   for j in range(gc // SUBLANES)]
            qs = jnp.concatenate([o[0] for o in ops], axis=0)
            ks = jnp.concatenate([o[1] for o in ops], axis=0)
            a = a + mk_ref[lvl + 1] * _dot_nt(qs.astype(BF16), ks.astype(BF16))
        o_scr[pl.ds(r0, gc), :] = _dot(a.astype(BF16), v)

        qi = jnp.concatenate([q * jnp.exp2(beta_f), q * jnp.exp2(tot_b - bx_b)], axis=1)
        qi_scr[pl.ds(r0, gc), :] = qi.astype(BF16)
        ks = jnp.concatenate([k_f * jnp.exp2(tot_f - beta_f), k_b * jnp.exp2(bx_b)], axis=1).astype(BF16)
        zero = jnp.zeros((c, dh), BF16)
        vd = jnp.concatenate([jnp.concatenate([v[g * c:(g + 1) * c] if h == g else zero
                                               for h in range(HG_GROUP)], axis=1)
                              for g in range(HG_GROUP)], axis=0)
        ut = _dot_tn(vd, ks)
        dec = jnp.exp2(jnp.concatenate([tot_f, tot_b], axis=1))
        for g in range(HG_GROUP):
            ut_scr[gi * HG_GROUP + g] = ut[g * dh:(g + 1) * dh]
            dec_scr[gi * HG_GROUP + g] = dec[g * c:g * c + 8]
        return carry

    lax.fori_loop(0, ng, intra, 0, unroll=HG_UNROLL // HG_GROUP)

    def states(i, carry):
        st_f, st_b = carry
        cf = i
        cb = nc - 1 - i
        st_scr[cf, :, :dh] = st_f.astype(BF16)
        st_scr[cb, :, dh:] = st_b.astype(BF16)
        st_f = st_f * dec_scr[cf, 0:1, :dh] + ut_scr[cf, :, :dh]
        st_b = st_b * dec_scr[cb, 0:1, dh:] + ut_scr[cb, :, dh:]
        return st_f, st_b

    zero = jnp.zeros((dh, dh), F32)
    lax.fori_loop(0, nc, states, (zero, zero))

    ow = ow_ref[...].astype(F32)

    def finish(ci, carry):
        r0 = pl.multiple_of(ci * c, c)
        o = o_scr[pl.ds(r0, c), :] + _dot_nt(qi_scr[pl.ds(r0, c), :], st_scr[ci])
        y = o * lax.rsqrt(jnp.mean(o * o, axis=-1, keepdims=True) + EPS) * ow
        out_ref[pl.ds(r0, c), :] = (y * _silu(hg_ref[pl.ds(r0, c), :].astype(F32))).astype(out_ref.dtype)
        return carry

    lax.fori_loop(0, nc, finish, 0, unroll=HG_UNROLL)


def _hgrn(main, gates, lb_logits, onorm_w, *, batch, seq, heads, layer, col_q, col_i, col_g):
    dh, c = HG_HEAD_DIM, HG_CHUNK
    n_slots = lb_logits.shape[1]
    lg = lb_logits.reshape(2 * n_slots, heads * dh)
    tri, masks = _hgrn_constants()
    nc = seq // c
    kern = functools.partial(_hgrn_kernel, layer=layer, n_slots=n_slots)
    return pl.pallas_call(
        kern,
        grid=(batch, heads),
        in_specs=[
            pl.BlockSpec((2 * n_slots, dh), lambda b, h: (0, h)),
            pl.BlockSpec((None, seq, dh), lambda b, h: (col_q + h, b, 0)),
            pl.BlockSpec((None, seq, dh), lambda b, h: (col_i + h, b, 0)),
            pl.BlockSpec((None, seq, dh), lambda b, h: (col_g + h, b, 0)),
            pl.BlockSpec((None, seq, dh), lambda b, h: (h, b, 0)),
            pl.BlockSpec((None, seq, dh), lambda b, h: (heads + h, b, 0)),
            pl.BlockSpec((1, dh), lambda b, h: (0, 0)),
            pl.BlockSpec(tri.shape, lambda b, h: (0, 0)),
            pl.BlockSpec(masks.shape, lambda b, h: (0, 0, 0)),
        ],
        out_specs=pl.BlockSpec((None, seq, dh), lambda b, h: (h, b, 0)),
        out_shape=jax.ShapeDtypeStruct((heads, batch * seq, dh), BF16),
        scratch_shapes=[
            pltpu.VMEM((seq, dh), F32),
            pltpu.VMEM((seq, 2 * dh), BF16),
            pltpu.VMEM((nc, dh, 2 * dh), F32),
            pltpu.VMEM((nc, 8, 2 * dh), F32),
            pltpu.VMEM((nc, dh, 2 * dh), BF16),
            pltpu.VMEM((4, seq, dh), F32),
        ],
        compiler_params=_cparams(("parallel", "parallel")),
        name="hgrn2",
    )(lg, main, main, main, gates, gates, onorm_w.reshape(1, dh),
      jnp.asarray(tri, BF16), jnp.asarray(masks, F32))


def _rel_bucket_index(rel):
    nb = N_BUCKETS // 2
    max_exact = nb // 2
    ret = jnp.where(rel > 0, nb, 0)
    n = jnp.abs(rel)
    nf = jnp.maximum(n, 1).astype(jnp.float32)
    large = max_exact + (jnp.log(nf / max_exact) / math.log(MAX_DISTANCE / max_exact)
                         * (nb - max_exact)).astype(jnp.int32)
    large = jnp.minimum(large, nb - 1)
    return ret + jnp.where(n < max_exact, n, large)


def _bias_tiles_kernel(tbl_ref, bkt_ref, out_ref):
    h = pl.program_id(0)
    bkt = bkt_ref[...]
    acc = jnp.zeros(bkt.shape, F32)
    for cidx in range(N_BUCKETS):
        acc = jnp.where(bkt == cidx, tbl_ref[cidx, h], acc)
    out_ref[0] = acc * LOG2E


def _bias_tiles(rel_bias):
    t = BIAS_TILE
    assert t >= MAX_DISTANCE
    heads = rel_bias.shape[1]
    key = jnp.arange(t, dtype=jnp.int32)[:, None]
    qry = jnp.arange(t, dtype=jnp.int32)[None, :]
    rel = jnp.stack([t * d + key - qry for d in range(-2, 3)])
    bkt = _rel_bucket_index(rel).astype(jnp.int32)
    return pl.pallas_call(
        _bias_tiles_kernel,
        grid=(heads,),
        in_specs=[
            pl.BlockSpec(memory_space=pltpu.SMEM),
            pl.BlockSpec((5, t, t), lambda h: (0, 0, 0)),
        ],
        out_specs=pl.BlockSpec((1, 5, t, t), lambda h: (h, 0, 0, 0)),
        out_shape=jax.ShapeDtypeStruct((heads, 5, t, t), F32),
        compiler_params=_cparams(("arbitrary",)),
        name="bias_tiles",
    )(rel_bias.astype(F32), bkt)


def _diff_attn_kernel(lam_ref, q_ref, k_ref, v_ref, tiles_ref, sw_ref, out_ref, vaug_scr,
                      s0_scr, s1_scr, m0_scr, m1_scr, e0_scr, e1_scr, *, lam_init, tq):
    seq, d2 = q_ref.shape
    t = BIAS_TILE
    s_scr, m_scr, e_scr = (s0_scr, s1_scr), (m0_scr, m1_scr), (e0_scr, e1_scr)

    vaug_scr[:d2, :] = v_ref[...].T
    vaug_scr[d2:, :] = jnp.ones((vaug_scr.shape[0] - d2, seq), BF16)

    lv = lam_ref[...].astype(F32)
    lam = (jnp.exp(jnp.sum(lv[0:1] * lv[1:2], axis=-1, keepdims=True))
           - jnp.exp(jnp.sum(lv[2:3] * lv[3:4], axis=-1, keepdims=True)) + lam_init)

    c = DA_HEAD_DIM ** -0.5 * LOG2E
    lane = lax.broadcasted_iota(jnp.int32, (tq, d2), 1)
    first = lane < DA_HEAD_DIM
    sw = sw_ref[...].astype(F32) * (1.0 - lam_init)

    def scores(n, slot):
        r0 = n * tq if isinstance(n, int) else pl.multiple_of(n * tq, tq)
        q = q_ref[pl.ds(r0, tq), :]
        zero = jnp.zeros_like(q)
        cols = []
        for ib in range(tq // t):
            d = [jnp.clip(jb - (n * (tq // t) + ib), -2, 2) + 2 for jb in range(seq // t)]
            cols.append(jnp.concatenate([tiles_ref[0, dj] for dj in d], axis=0))
        bias = jnp.concatenate(cols, axis=1)
        qq = jnp.concatenate([jnp.where(first, q, zero), jnp.where(first, zero, q)], axis=0)
        st = _dot_nt(k_ref[...], qq) * c
        st = jnp.concatenate([st[:, :tq] + bias, st[:, tq:] + bias], axis=1)
        s_scr[slot][...] = st
        m_scr[slot][...] = jnp.max(st, axis=0, keepdims=True)

    def numerators(slot):
        e_scr[slot][...] = jnp.exp2(s_scr[slot][...] - m_scr[slot][...]).astype(BF16)

    def values(n, slot):
        r0 = n * tq if isinstance(n, int) else pl.multiple_of(n * tq, tq)
        ot = _dot(vaug_scr[...], e_scr[slot][...])
        on = ot[:d2] / ot[d2:d2 + 1]
        o = on[:, :tq] - lam * on[:, tq:]
        yt = o * lax.rsqrt(jnp.mean(o * o, axis=0, keepdims=True) + EPS)
        out_ref[pl.ds(r0, tq), :] = (yt.T * sw).astype(out_ref.dtype)

    nq = seq // tq
    assert nq % 2 == 0 and nq >= 4
    scores(0, 0)
    scores(1, 1)
    numerators(0)

    def pair(i, carry):
        scores(2 * i + 2, 0)
        numerators(1)
        values(2 * i, 0)
        scores(2 * i + 3, 1)
        numerators(0)
        values(2 * i + 1, 1)
        return carry

    lax.fori_loop(0, nq // 2 - 1, pair, 0)
    numerators(1)
    values(nq - 2, 0)
    values(nq - 1, 1)


def _diff_attn(main, lam_vecs, tiles, subln_w, *, batch, seq, heads, lam_init, col_q, col_k, col_v, tq):
    d2 = 2 * DA_HEAD_DIM
    t = BIAS_TILE
    kern = functools.partial(_diff_attn_kernel, lam_init=lam_init, tq=tq)
    return pl.pallas_call(
        kern,
        grid=(heads, batch),
        in_specs=[
            pl.BlockSpec(lam_vecs.shape, lambda h, b: (0, 0)),
            pl.BlockSpec((None, seq, d2), lambda h, b: (col_q + h, b, 0)),
            pl.BlockSpec((None, seq, d2), lambda h, b: (col_k + h, b, 0)),
            pl.BlockSpec((None, seq, d2), lambda h, b: (col_v + h, b, 0)),
            pl.BlockSpec((1, 5, t, t), lambda h, b: (h, 0, 0, 0)),
            pl.BlockSpec((1, d2), lambda h, b: (0, 0)),
        ],
        out_specs=pl.BlockSpec((None, seq, d2), lambda h, b: (h, b, 0)),
        out_shape=jax.ShapeDtypeStruct((heads, batch * seq, d2), BF16),
        scratch_shapes=[pltpu.VMEM((d2 + 2 * SUBLANES, seq), BF16),
                        pltpu.VMEM((seq, 2 * tq), F32), pltpu.VMEM((seq, 2 * tq), F32),
                        pltpu.VMEM((1, 2 * tq), F32), pltpu.VMEM((1, 2 * tq), F32),
                        pltpu.VMEM((seq, 2 * tq), BF16), pltpu.VMEM((seq, 2 * tq), BF16)],
        compiler_params=_cparams(("parallel", "parallel")),
        name="diff_attn",
    )(lam_vecs, main, main, main, tiles, subln_w.reshape(1, d2))


def _out_proj_kernel(x_ref, a_ref, b_ref, wa_ref, wb_ref, nw_ref, wd_ref, h_ref, u_ref, wd_out_ref):
    def heads_to_lanes(ref):
        return jnp.concatenate([ref[c] for c in range(ref.shape[0])], axis=1)

    h = x_ref[...] + _dot(heads_to_lanes(a_ref), wa_ref[...]) + _dot(heads_to_lanes(b_ref), wb_ref[...])
    h_ref[...] = h
    ms = jnp.mean(h * h, axis=-1, keepdims=True)
    u_ref[...] = (h * lax.rsqrt(ms + EPS) * nw_ref[...]).astype(u_ref.dtype)
    wd_out_ref[...] = wd_ref[...].astype(wd_out_ref.dtype)


def _out_proj(x, a, b, w, nw, wd, layer, tm):
    m, d = x.shape
    ka, kb = a.shape[0] * a.shape[2], b.shape[0] * b.shape[2]
    assert ka == kb and w.shape == (ka + kb, d)
    steps = m // tm
    f = wd.shape[1]
    rows = f // steps
    assert rows * steps == f and rows % (2 * SUBLANES) == 0
    resident = pl.Buffered(1)
    return pl.pallas_call(
        _out_proj_kernel,
        grid=(steps,),
        in_specs=[
            pl.BlockSpec((tm, d), lambda i: (i, 0)),
            pl.BlockSpec((a.shape[0], tm, a.shape[2]), lambda i: (0, i, 0)),
            pl.BlockSpec((b.shape[0], tm, b.shape[2]), lambda i: (0, i, 0)),
            pl.BlockSpec((ka, d), lambda i: (0, 0), pipeline_mode=resident),
            pl.BlockSpec((kb, d), lambda i: (1, 0), pipeline_mode=resident),
            pl.BlockSpec((1, d), lambda i: (0, 0)),
            pl.BlockSpec((None, rows, wd.shape[2]), lambda i: (layer, i, 0)),
        ],
        out_specs=[pl.BlockSpec((tm, d), lambda i: (i, 0)), pl.BlockSpec((tm, d), lambda i: (i, 0)),
                   pl.BlockSpec((rows, wd.shape[2]), lambda i: (i, 0))],
        out_shape=[jax.ShapeDtypeStruct((m, d), F32), jax.ShapeDtypeStruct((m, d), BF16),
                   jax.ShapeDtypeStruct(wd.shape[1:], BF16)],
        compiler_params=_cparams(("parallel",)),
        name="out_proj",
    )(x, a, b, w, w, nw.reshape(1, d), wd)


def _ffn_up_kernel(u_ref, wg_ref, wu_ref, o_ref, wg_scr, wu_scr):
    @pl.when(pl.program_id(1) == 0)
    def _():
        wg_scr[...] = wg_ref[...].astype(BF16)
        wu_scr[...] = wu_ref[...].astype(BF16)

    sub = min(512, u_ref.shape[0])
    for r in range(u_ref.shape[0] // sub):
        rows = slice(r * sub, (r + 1) * sub)
        u = u_ref[rows, :]
        g = _dot(u, wg_scr[...])
        up = _dot(u, wu_scr[...])
        o_ref[rows, :] = (_silu(g) * up).astype(o_ref.dtype)


def _ffn_up(u, wg, wu, layer, tm, tn):
    m, d = u.shape
    n = wg.shape[2]
    return pl.pallas_call(
        _ffn_up_kernel,
        grid=(n // tn, m // tm),
        in_specs=[
            pl.BlockSpec((tm, d), lambda j, i: (i, 0)),
            pl.BlockSpec((None, d, tn), lambda j, i: (layer, 0, j)),
            pl.BlockSpec((None, d, tn), lambda j, i: (layer, 0, j)),
        ],
        out_specs=pl.BlockSpec((tm, tn), lambda j, i: (i, j)),
        out_shape=jax.ShapeDtypeStruct((m, n), BF16),
        scratch_shapes=[pltpu.VMEM((d, tn), BF16), pltpu.VMEM((d, tn), BF16)],
        compiler_params=_cparams(("parallel", "arbitrary")),
        name="ffn_up",
    )(u, wg, wu)


def _ffn_down_kernel(a_ref, w_ref, h_ref, fw_ref, o_ref):
    y = h_ref[...] + _dot(a_ref[...], w_ref[...])
    ms = jnp.mean(y * y, axis=-1, keepdims=True)
    o_ref[...] = y * lax.rsqrt(ms + EPS) * fw_ref[...]


def _ffn_down(a, w, h, fw, tm):
    m, f = a.shape
    d = w.shape[1]
    return pl.pallas_call(
        _ffn_down_kernel,
        grid=(m // tm,),
        in_specs=[
            pl.BlockSpec((tm, f), lambda i: (i, 0)),
            pl.BlockSpec((f, d), lambda i: (0, 0), pipeline_mode=pl.Buffered(1)),
            pl.BlockSpec((tm, d), lambda i: (i, 0)),
            pl.BlockSpec((1, d), lambda i: (0, 0)),
        ],
        out_specs=pl.BlockSpec((tm, d), lambda i: (i, 0)),
        out_shape=jax.ShapeDtypeStruct((m, d), F32),
        compiler_params=_cparams(("parallel",)),
        name="ffn_down",
    )(a, w, h, fw.reshape(1, d))


def kernel(x, norm1_w, w_in, hg_lb_logits, hg_onorm_w, lambda_q1, lambda_k1, lambda_q2, lambda_k2,
           da_subln_w, rel_bias, w_out, norm2_w, w_gate, w_up, w_down, final_norm_w):
    batch, seq, d_model = x.shape
    depth = w_in.shape[0]
    hg_width = hg_lb_logits.shape[-1]
    da_width = d_model - hg_width
    hg_heads = hg_width // HG_HEAD_DIM
    da_heads = da_width // (2 * DA_HEAD_DIM)
    assert w_in.shape[2] == 5 * hg_width + 3 * da_width
    assert seq % HG_CHUNK == 0 and seq % BIAS_TILE == 0
    m = batch * seq
    blk = LANES

    tiles = _bias_tiles(rel_bias)
    h = x.reshape(m, d_model)
    for l in range(depth):
        main, gates = _in_proj(h, norm1_w[l], w_in[l].astype(BF16), 2 * hg_width, 4 * hg_width,
                               TILES["in_proj_m"], TILES["in_proj_n"])

        o_hg = _hgrn(main, gates, hg_lb_logits, hg_onorm_w[l], batch=batch, seq=seq, heads=hg_heads,
                     layer=l, col_q=0, col_i=hg_width // blk, col_g=2 * hg_width // blk)

        lam_init = 0.8 - 0.6 * math.exp(-0.3 * l)
        lam_vecs = jnp.stack([lambda_q1[l], lambda_k1[l], lambda_q2[l], lambda_k2[l]]).astype(F32)
        base = 3 * hg_width // blk
        o_da = _diff_attn(main, lam_vecs, tiles, da_subln_w[l], batch=batch, seq=seq, heads=da_heads,
                          lam_init=lam_init, col_q=base, col_k=base + da_width // blk,
                          col_v=base + 2 * da_width // blk, tq=TILES["attn_q"])

        h, u2, w_down_bf16 = _out_proj(h, o_hg, o_da, w_out[l].astype(BF16), norm2_w[l], w_down, l,
                                       TILES["out_proj_m"])

        act = _ffn_up(u2, w_gate, w_up, l, TILES["ffn_up_m"], TILES["ffn_up_n"])
        last = l == depth - 1
        assert last, "final norm is fused into the last layer's down projection"
        h = _ffn_down(act, w_down_bf16, h, final_norm_w, TILES["ffn_down_m"])
    return h.reshape(batch, seq, d_model)
```

```python
import functools
import math

import numpy as np
import jax
import jax.numpy as jnp
from jax import lax
from jax.experimental import pallas as pl
from jax.experimental.pallas import tpu as pltpu

F32 = jnp.float32
BF16 = jnp.bfloat16

EPS = 1e-6
LOG2E = math.log2(math.e)
HG_HEAD_DIM = 128
DA_HEAD_DIM = 64
N_BUCKETS = 32
MAX_DISTANCE = 128

LANES = 128
SUBLANES = 8
HG_CHUNK = 64
HG_LEVELS = 6
HG_GROUP = 2
HG_UNROLL = 16
BIAS_TILE = 128
VMEM_LIMIT = 56 * 1024 * 1024

TILES = {
    "in_proj_m": 256, "in_proj_n": 1024,
    "attn_q": 256,
    "out_proj_m": 512,
    "ffn_up_m": 2048, "ffn_up_n": 512,
    "ffn_down_m": 512,
}


def _cparams(sem, flags=None):
    return pltpu.CompilerParams(dimension_semantics=sem, vmem_limit_bytes=VMEM_LIMIT, flags=flags)


def _dot(a, b):
    return jnp.dot(a, b, preferred_element_type=F32)


def _dot_nt(a, b):
    return lax.dot_general(a, b, (((1,), (1,)), ((), ())), preferred_element_type=F32)


def _dot_tn(a, b):
    return lax.dot_general(a, b, (((0,), (0,)), ((), ())), preferred_element_type=F32)


def _sigmoid(x):
    return 0.5 * jnp.tanh(0.5 * x) + 0.5


def _silu(x):
    h = 0.5 * x
    return h + h * jnp.tanh(h)


def _in_proj_kernel(x_ref, nw_ref, w_ref, main_ref, gates_ref, *, gate_lo, gate_hi, tn):
    x = x_ref[...].astype(F32)
    ms = jnp.mean(x * x, axis=-1, keepdims=True)
    u = (x * lax.rsqrt(ms + EPS) * nw_ref[...]).astype(BF16)

    n = w_ref.shape[1]
    slab = {id(main_ref): 0, id(gates_ref): 0}
    for c0 in range(0, n, tn):
        o_ref = gates_ref if gate_lo <= c0 < gate_hi else main_ref
        r = _dot(u, w_ref[:, c0:c0 + tn]).astype(o_ref.dtype)
        for c in range(tn // LANES):
            o_ref[slab[id(o_ref)]] = r[:, c * LANES:(c + 1) * LANES]
            slab[id(o_ref)] += 1


def _in_proj(x, nw, w, gate_lo, gate_hi, tm, tn):
    m, d = x.shape
    n = w.shape[1]
    assert gate_lo % tn == 0 and gate_hi % tn == 0 and n % tn == 0
    n_gate_cols = gate_hi - gate_lo
    kern = functools.partial(_in_proj_kernel, gate_lo=gate_lo, gate_hi=gate_hi, tn=tn)
    return pl.pallas_call(
        kern,
        grid=(m // tm,),
        in_specs=[
            pl.BlockSpec((tm, d), lambda i: (i, 0)),
            pl.BlockSpec((1, d), lambda i: (0, 0)),
            pl.BlockSpec((d, n), lambda i: (0, 0), pipeline_mode=pl.Buffered(1)),
        ],
        out_specs=[
            pl.BlockSpec(((n - n_gate_cols) // LANES, tm, LANES), lambda i: (0, i, 0)),
            pl.BlockSpec((n_gate_cols // LANES, tm, LANES), lambda i: (0, i, 0)),
        ],
        out_shape=[
            jax.ShapeDtypeStruct(((n - n_gate_cols) // LANES, m, LANES), BF16),
            jax.ShapeDtypeStruct((n_gate_cols // LANES, m, LANES), F32),
        ],
        compiler_params=_cparams(("parallel",)),
        name="in_proj",
    )(x, nw.reshape(1, d), w)


def _hgrn_constants():
    c, nl, g = HG_CHUNK, HG_LEVELS, HG_GROUP
    idx = np.arange(g * c)
    same_chunk = (idx[:, None] // c) == (idx[None, :] // c)
    tri = (same_chunk & (idx[:, None] >= idx[None, :])).astype(np.float32)
    masks = np.zeros((nl + 1, g * c, g * c), np.float32)
    masks[0] = np.eye(g * c, dtype=np.float32)
    for lvl in range(nl):
        m = 1 << lvl
        same_pair = (idx[:, None] // (2 * m)) == (idx[None, :] // (2 * m))
        other_sibling = (idx[:, None] // m) != (idx[None, :] // m)
        masks[lvl + 1] = (same_pair & other_sibling).astype(np.float32)
    return tri, masks


def _level_operands(lvl, j, q, f_f, f_b, k_f, k_b, beta_f, bx_b, odd, hi2, hi4):
    sl = slice(j * SUBLANES, (j + 1) * SUBLANES)
    qj, ffj, fbj, kfj, kbj = q[sl], f_f[sl], f_b[sl], k_f[sl], k_b[sl]
    if lvl == 0:
        return qj * jnp.where(odd, ffj, fbj), jnp.where(odd, kbj, kfj)
    if lvl == 1:
        up_f, dn_f = pltpu.roll(ffj, SUBLANES - 1, axis=0), pltpu.roll(ffj, 1, axis=0)
        up_b, dn_b = pltpu.roll(fbj, SUBLANES - 1, axis=0), pltpu.roll(fbj, 1, axis=0)
        fq = jnp.where(hi2, jnp.where(odd, ffj * dn_f, ffj), jnp.where(odd, fbj, fbj * up_b))
        ks = jnp.where(hi2, jnp.where(odd, kbj * dn_b, kbj), jnp.where(odd, kfj, kfj * up_f))
        return qj * fq, ks
    bfj, bbj = beta_f[sl], bx_b[sl]
    if lvl == 2:
        r = j * SUBLANES + SUBLANES // 2
        ef = jnp.exp2(-jnp.abs(bfj - beta_f[r - 1:r]))
        eb = jnp.exp2(-jnp.abs(bbj - bx_b[r:r + 1]))
        return qj * jnp.where(hi4, ef, eb), jnp.where(hi4, kbj * eb, kfj * ef)
    mb = (1 << lvl) // SUBLANES
    r = ((j // (2 * mb)) * 2 * mb + mb) * SUBLANES
    ref_f, ref_b = beta_f[r - 1:r], bx_b[r:r + 1]
    if (j // mb) & 1:
        return qj * jnp.exp2(bfj - ref_f), kbj * jnp.exp2(bbj - ref_b)
    return qj * jnp.exp2(ref_b - bbj), kfj * jnp.exp2(ref_f - bfj)


def _hgrn_kernel(lg_ref, hq_ref, hi_ref, hg_ref, gf_ref, gb_ref, ow_ref, tri_ref, mk_ref,
                 out_ref, o_scr, qi_scr, ut_scr, dec_scr, st_scr, g_scr, *, layer, n_slots):
    c, nl, dh = HG_CHUNK, HG_LEVELS, HG_HEAD_DIM
    seq = hq_ref.shape[0]
    nc = seq // c

    lg = lg_ref[...].astype(F32)
    lbs = []
    for d in range(2):
        rows = lg[d * n_slots:(d + 1) * n_slots]
        e = jnp.exp(rows - jnp.max(rows, axis=0, keepdims=True))
        lbs.append(jnp.sum(e[:layer + 1], axis=0, keepdims=True) / jnp.sum(e, axis=0, keepdims=True))
    lb_f, lb_b = lbs

    pos = lax.broadcasted_iota(jnp.int32, (SUBLANES, dh), 0)
    odd = (pos & 1) != 0
    hi2 = (pos & 2) != 0
    hi4 = (pos & 4) != 0

    def split(x):
        hi = x.astype(BF16)
        return hi, (x - hi.astype(F32)).astype(BF16)

    gc = HG_GROUP * c
    ng = nc // HG_GROUP

    def gates(gi, carry):
        r0 = pl.multiple_of(gi * gc, gc)
        f_f = lb_f + (1.0 - lb_f) * _sigmoid(gf_ref[pl.ds(r0, gc), :])
        f_b = lb_b + (1.0 - lb_b) * _sigmoid(gb_ref[pl.ds(r0, gc), :])
        lf_f = jnp.log2(f_f)
        lf_b = jnp.log2(f_b)
        pre = _dot(tri_ref[...], jnp.concatenate(split(lf_f) + split(lf_b), axis=1))
        g_scr[0, pl.ds(r0, gc), :] = f_f
        g_scr[1, pl.ds(r0, gc), :] = f_b
        g_scr[2, pl.ds(r0, gc), :] = pre[:, :dh] + pre[:, dh:2 * dh]
        g_scr[3, pl.ds(r0, gc), :] = pre[:, 2 * dh:3 * dh] + pre[:, 3 * dh:] - lf_b
        return carry

    lax.fori_loop(0, ng, gates, 0, unroll=HG_UNROLL // HG_GROUP)

    def per_chunk_rows(x, row):
        return jnp.concatenate([jnp.broadcast_to(x[g * c + row:g * c + row + 1], (c, dh))
                                for g in range(HG_GROUP)], axis=0)

    def intra(gi, carry):
        r0 = pl.multiple_of(gi * gc, gc)
        q = _silu(hq_ref[pl.ds(r0, gc), :].astype(F32))
        v = hi_ref[pl.ds(r0, gc), :]
        f_f = g_scr[0, pl.ds(r0, gc), :]
        f_b = g_scr[1, pl.ds(r0, gc), :]
        beta_f = g_scr[2, pl.ds(r0, gc), :]
        bx_b = g_scr[3, pl.ds(r0, gc), :]
        k_f = 1.0 - f_f
        k_b = 1.0 - f_b
        tot_f = per_chunk_rows(beta_f, c - 1)
        tot_b = per_chunk_rows(bx_b + jnp.log2(f_b), c - 1)

        a = mk_ref[0] * _dot_nt(q.astype(BF16), (k_f + k_b).astype(BF16))
        for lvl in range(nl):
            ops = [_level_operands(lvl, j, q, f_f, f_b, k_f, k_b, beta_f, bx_b, odd, hi2, hi4)
                   for j in range(gc // SUBLANES)]
            qs = jnp.concatenate([o[0] for o in ops], axis=0)
            ks = jnp.concatenate([o[1] for o in ops], axis=0)
            a = a + mk_ref[lvl + 1] * _dot_nt(qs.astype(BF16), ks.astype(BF16))
        o_scr[pl.ds(r0, gc), :] = _dot(a.astype(BF16), v)

        qi = jnp.concatenate([q * jnp.exp2(beta_f), q * jnp.exp2(tot_b - bx_b)], axis=1)
        qi_scr[pl.ds(r0, gc), :] = qi.astype(BF16)
        ks = jnp.concatenate([k_f * jnp.exp2(tot_f - beta_f), k_b * jnp.exp2(bx_b)], axis=1).astype(BF16)
        zero = jnp.zeros((c, dh), BF16)
        vd = jnp.concatenate([jnp.concatenate([v[g * c:(g + 1) * c] if h == g else zero
                                               for h in range(HG_GROUP)], axis=1)
                              for g in range(HG_GROUP)], axis=0)
        ut = _dot_tn(vd, ks)
        dec = jnp.exp2(jnp.concatenate([tot_f, tot_b], axis=1))
        for g in range(HG_GROUP):
            ut_scr[gi * HG_GROUP + g] = ut[g * dh:(g + 1) * dh]
            dec_scr[gi * HG_GROUP + g] = dec[g * c:g * c + 8]
        return carry

    lax.fori_loop(0, ng, intra, 0, unroll=HG_UNROLL // HG_GROUP)

    def states(i, carry):
        st_f, st_b = carry
        cf = i
        cb = nc - 1 - i
        st_scr[cf, :, :dh] = st_f.astype(BF16)
        st_scr[cb, :, dh:] = st_b.astype(BF16)
        st_f = st_f * dec_scr[cf, 0:1, :dh] + ut_scr[cf, :, :dh]
        st_b = st_b * dec_scr[cb, 0:1, dh:] + ut_scr[cb, :, dh:]
        return st_f, st_b

    zero = jnp.zeros((dh, dh), F32)
    lax.fori_loop(0, nc, states, (zero, zero))

    ow = ow_ref[...].astype(F32)

    def finish(ci, carry):
        r0 = pl.multiple_of(ci * c, c)
        o = o_scr[pl.ds(r0, c), :] + _dot_nt(qi_scr[pl.ds(r0, c), :], st_scr[ci])
        y = o * lax.rsqrt(jnp.mean(o * o, axis=-1, keepdims=True) + EPS) * ow
        out_ref[pl.ds(r0, c), :] = (y * _silu(hg_ref[pl.ds(r0, c), :].astype(F32))).astype(out_ref.dtype)
        return carry

    lax.fori_loop(0, nc, finish, 0, unroll=HG_UNROLL)


def _hgrn(main, gates, lb_logits, onorm_w, *, batch, seq, heads, layer, col_q, col_i, col_g):
    dh, c = HG_HEAD_DIM, HG_CHUNK
    n_slots = lb_logits.shape[1]
    lg = lb_logits.reshape(2 * n_slots, heads * dh)
    tri, masks = _hgrn_constants()
    nc = seq // c
    kern = functools.partial(_hgrn_kernel, layer=layer, n_slots=n_slots)
    return pl.pallas_call(
        kern,
        grid=(batch, heads),
        in_specs=[
            pl.BlockSpec((2 * n_slots, dh), lambda b, h: (0, h)),
            pl.BlockSpec((None, seq, dh), lambda b, h: (col_q + h, b, 0)),
            pl.BlockSpec((None, seq, dh), lambda b, h: (col_i + h, b, 0)),
            pl.BlockSpec((None, seq, dh), lambda b, h: (col_g + h, b, 0)),
            pl.BlockSpec((None, seq, dh), lambda b, h: (h, b, 0)),
            pl.BlockSpec((None, seq, dh), lambda b, h: (heads + h, b, 0)),
            pl.BlockSpec((1, dh), lambda b, h: (0, 0)),
            pl.BlockSpec(tri.shape, lambda b, h: (0, 0)),
            pl.BlockSpec(masks.shape, lambda b, h: (0, 0, 0)),
        ],
        out_specs=pl.BlockSpec((None, seq, dh), lambda b, h: (h, b, 0)),
        out_shape=jax.ShapeDtypeStruct((heads, batch * seq, dh), BF16),
        scratch_shapes=[
            pltpu.VMEM((seq, dh), F32),
            pltpu.VMEM((seq, 2 * dh), BF16),
            pltpu.VMEM((nc, dh, 2 * dh), F32),
            pltpu.VMEM((nc, 8, 2 * dh), F32),
            pltpu.VMEM((nc, dh, 2 * dh), BF16),
            pltpu.VMEM((4, seq, dh), F32),
        ],
        compiler_params=_cparams(("parallel", "parallel")),
        name="hgrn2",
    )(lg, main, main, main, gates, gates, onorm_w.reshape(1, dh),
      jnp.asarray(tri, BF16), jnp.asarray(masks, F32))


def _rel_bucket_index(rel):
    nb = N_BUCKETS // 2
    max_exact = nb // 2
    ret = jnp.where(rel > 0, nb, 0)
    n = jnp.abs(rel)
    nf = jnp.maximum(n, 1).astype(jnp.float32)
    large = max_exact + (jnp.log(nf / max_exact) / math.log(MAX_DISTANCE / max_exact)
                         * (nb - max_exact)).astype(jnp.int32)
    large = jnp.minimum(large, nb - 1)
    return ret + jnp.where(n < max_exact, n, large)


def _bias_tiles_kernel(tbl_ref, bkt_ref, out_ref):
    h = pl.program_id(0)
    bkt = bkt_ref[...]
    acc = jnp.zeros(bkt.shape, F32)
    for cidx in range(N_BUCKETS):
        acc = jnp.where(bkt == cidx, tbl_ref[cidx, h], acc)
    out_ref[0] = acc * LOG2E


def _bias_tiles(rel_bias):
    t = BIAS_TILE
    assert t >= MAX_DISTANCE
    heads = rel_bias.shape[1]
    key = jnp.arange(t, dtype=jnp.int32)[:, None]
    qry = jnp.arange(t, dtype=jnp.int32)[None, :]
    rel = jnp.stack([t * d + key - qry for d in range(-2, 3)])
    bkt = _rel_bucket_index(rel).astype(jnp.int32)
    return pl.pallas_call(
        _bias_tiles_kernel,
        grid=(heads,),
        in_specs=[
            pl.BlockSpec(memory_space=pltpu.SMEM),
            pl.BlockSpec((5, t, t), lambda h: (0, 0, 0)),
        ],
        out_specs=pl.BlockSpec((1, 5, t, t), lambda h: (h, 0, 0, 0)),
        out_shape=jax.ShapeDtypeStruct((heads, 5, t, t), F32),
        compiler_params=_cparams(("arbitrary",)),
        name="bias_tiles",
    )(rel_bias.astype(F32), bkt)


def _diff_attn_kernel(lam_ref, q_ref, k_ref, v_ref, tiles_ref, sw_ref, out_ref, vaug_scr,
                      s0_scr, s1_scr, m0_scr, m1_scr, e0_scr, e1_scr, *, lam_init, tq):
    seq, d2 = q_ref.shape
    t = BIAS_TILE
    s_scr, m_scr, e_scr = (s0_scr, s1_scr), (m0_scr, m1_scr), (e0_scr, e1_scr)

    vaug_scr[:d2, :] = v_ref[...].T
    vaug_scr[d2:, :] = jnp.ones((vaug_scr.shape[0] - d2, seq), BF16)

    lv = lam_ref[...].astype(F32)
    lam = (jnp.exp(jnp.sum(lv[0:1] * lv[1:2], axis=-1, keepdims=True))
           - jnp.exp(jnp.sum(lv[2:3] * lv[3:4], axis=-1, keepdims=True)) + lam_init)

    c = DA_HEAD_DIM ** -0.5 * LOG2E
    lane = lax.broadcasted_iota(jnp.int32, (tq, d2), 1)
    first = lane < DA_HEAD_DIM
    sw = sw_ref[...].astype(F32) * (1.0 - lam_init)

    def scores(n, slot):
        r0 = n * tq if isinstance(n, int) else pl.multiple_of(n * tq, tq)
        q = q_ref[pl.ds(r0, tq), :]
        zero = jnp.zeros_like(q)
        cols = []
        for ib in range(tq // t):
            d = [jnp.clip(jb - (n * (tq // t) + ib), -2, 2) + 2 for jb in range(seq // t)]
            cols.append(jnp.concatenate([tiles_ref[0, dj] for dj in d], axis=0))
        bias = jnp.concatenate(cols, axis=1)
        qq = jnp.concatenate([jnp.where(first, q, zero), jnp.where(first, zero, q)], axis=0)
        st = _dot_nt(k_ref[...], qq) * c
        st = jnp.concatenate([st[:, :tq] + bias, st[:, tq:] + bias], axis=1)
        s_scr[slot][...] = st
        m_scr[slot][...] = jnp.max(st, axis=0, keepdims=True)

    def numerators(slot):
        e_scr[slot][...] = jnp.exp2(s_scr[slot][...] - m_scr[slot][...]).astype(BF16)

    def values(n, slot):
        r0 = n * tq if isinstance(n, int) else pl.multiple_of(n * tq, tq)
        ot = _dot(vaug_scr[...], e_scr[slot][...])
        on = ot[:d2] / ot[d2:d2 + 1]
        o = on[:, :tq] - lam * on[:, tq:]
        yt = o * lax.rsqrt(jnp.mean(o * o, axis=0, keepdims=True) + EPS)
        out_ref[pl.ds(r0, tq), :] = (yt.T * sw).astype(out_ref.dtype)

    nq = seq // tq
    assert nq % 2 == 0 and nq >= 4
    scores(0, 0)
    scores(1, 1)
    numerators(0)

    def pair(i, carry):
        scores(2 * i + 2, 0)
        numerators(1)
        values(2 * i, 0)
        scores(2 * i + 3, 1)
        numerators(0)
        values(2 * i + 1, 1)
        return carry

    lax.fori_loop(0, nq // 2 - 1, pair, 0)
    numerators(1)
    values(nq - 2, 0)
    values(nq - 1, 1)


def _diff_attn(main, lam_vecs, tiles, subln_w, *, batch, seq, heads, lam_init, col_q, col_k, col_v, tq):
    d2 = 2 * DA_HEAD_DIM
    t = BIAS_TILE
    kern = functools.partial(_diff_attn_kernel, lam_init=lam_init, tq=tq)
    return pl.pallas_call(
        kern,
        grid=(heads, batch),
        in_specs=[
            pl.BlockSpec(lam_vecs.shape, lambda h, b: (0, 0)),
            pl.BlockSpec((None, seq, d2), lambda h, b: (col_q + h, b, 0)),
            pl.BlockSpec((None, seq, d2), lambda h, b: (col_k + h, b, 0)),
            pl.BlockSpec((None, seq, d2), lambda h, b: (col_v + h, b, 0)),
            pl.BlockSpec((1, 5, t, t), lambda h, b: (h, 0, 0, 0)),
            pl.BlockSpec((1, d2), lambda h, b: (0, 0)),
        ],
        out_specs=pl.BlockSpec((None, seq, d2), lambda h, b: (h, b, 0)),
        out_shape=jax.ShapeDtypeStruct((heads, batch * seq, d2), BF16),
        scratch_shapes=[pltpu.VMEM((d2 + 2 * SUBLANES, seq), BF16),
                        pltpu.VMEM((seq, 2 * tq), F32), pltpu.VMEM((seq, 2 * tq), F32),
                        pltpu.VMEM((1, 2 * tq), F32), pltpu.VMEM((1, 2 * tq), F32),
                        pltpu.VMEM((seq, 2 * tq), BF16), pltpu.VMEM((seq, 2 * tq), BF16)],
        compiler_params=_cparams(("parallel", "parallel")),
        name="diff_attn",
    )(lam_vecs, main, main, main, tiles, subln_w.reshape(1, d2))


def _out_proj_kernel(x_ref, a_ref, b_ref, wa_ref, wb_ref, nw_ref, wd_ref, h_ref, u_ref, wd_out_ref):
    def heads_to_lanes(ref):
        return jnp.concatenate([ref[c] for c in range(ref.shape[0])], axis=1)

    h = x_ref[...] + _dot(heads_to_lanes(a_ref), wa_ref[...]) + _dot(heads_to_lanes(b_ref), wb_ref[...])
    h_ref[...] = h
    ms = jnp.mean(h * h, axis=-1, keepdims=True)
    u_ref[...] = (h * lax.rsqrt(ms + EPS) * nw_ref[...]).astype(u_ref.dtype)
    wd_out_ref[...] = wd_ref[...].astype(wd_out_ref.dtype)


def _out_proj(x, a, b, w, nw, wd, layer, tm):
    m, d = x.shape
    ka, kb = a.shape[0] * a.shape[2], b.shape[0] * b.shape[2]
    assert ka == kb and w.shape == (ka + kb, d)
    steps = m // tm
    f = wd.shape[1]
    rows = f // steps
    assert rows * steps == f and rows % (2 * SUBLANES) == 0
    resident = pl.Buffered(1)
    return pl.pallas_call(
        _out_proj_kernel,
        grid=(steps,),
        in_specs=[
            pl.BlockSpec((tm, d), lambda i: (i, 0)),
            pl.BlockSpec((a.shape[0], tm, a.shape[2]), lambda i: (0, i, 0)),
            pl.BlockSpec((b.shape[0], tm, b.shape[2]), lambda i: (0, i, 0)),
            pl.BlockSpec((ka, d), lambda i: (0, 0), pipeline_mode=resident),
            pl.BlockSpec((kb, d), lambda i: (1, 0), pipeline_mode=resident),
            pl.BlockSpec((1, d), lambda i: (0, 0)),
            pl.BlockSpec((None, rows, wd.shape[2]), lambda i: (layer, i, 0)),
        ],
        out_specs=[pl.BlockSpec((tm, d), lambda i: (i, 0)), pl.BlockSpec((tm, d), lambda i: (i, 0)),
                   pl.BlockSpec((rows, wd.shape[2]), lambda i: (i, 0))],
        out_shape=[jax.ShapeDtypeStruct((m, d), F32), jax.ShapeDtypeStruct((m, d), BF16),
                   jax.ShapeDtypeStruct(wd.shape[1:], BF16)],
        compiler_params=_cparams(("parallel",)),
        name="out_proj",
    )(x, a, b, w, w, nw.reshape(1, d), wd)


def _ffn_up_kernel(u_ref, wg_ref, wu_ref, o_ref, wg_scr, wu_scr):
    @pl.when(pl.program_id(1) == 0)
    def _():
        wg_scr[...] = wg_ref[...].astype(BF16)
        wu_scr[...] = wu_ref[...].astype(BF16)

    sub = min(512, u_ref.shape[0])
    for r in range(u_ref.shape[0] // sub):
        rows = slice(r * sub, (r + 1) * sub)
        u = u_ref[rows, :]
        g = _dot(u, wg_scr[...])
        up = _dot(u, wu_scr[...])
        o_ref[rows, :] = (_silu(g) * up).astype(o_ref.dtype)


def _ffn_up(u, wg, wu, layer, tm, tn):
    m, d = u.shape
    n = wg.shape[2]
    return pl.pallas_call(
        _ffn_up_kernel,
        grid=(n // tn, m // tm),
        in_specs=[
            pl.BlockSpec((tm, d), lambda j, i: (i, 0)),
            pl.BlockSpec((None, d, tn), lambda j, i: (layer, 0, j)),
            pl.BlockSpec((None, d, tn), lambda j, i: (layer, 0, j)),
        ],
        out_specs=pl.BlockSpec((tm, tn), lambda j, i: (i, j)),
        out_shape=jax.ShapeDtypeStruct((m, n), BF16),
        scratch_shapes=[pltpu.VMEM((d, tn), BF16), pltpu.VMEM((d, tn), BF16)],
        compiler_params=_cparams(("parallel", "arbitrary")),
        name="ffn_up",
    )(u, wg, wu)


def _ffn_down_kernel(a_ref, w_ref, h_ref, fw_ref, o_ref):
    y = h_ref[...] + _dot(a_ref[...], w_ref[...])
    ms = jnp.mean(y * y, axis=-1, keepdims=True)
    o_ref[...] = y * lax.rsqrt(ms + EPS) * fw_ref[...]


def _ffn_down(a, w, h, fw, tm):
    m, f = a.shape
    d = w.shape[1]
    return pl.pallas_call(
        _ffn_down_kernel,
        grid=(m // tm,),
        in_specs=[
            pl.BlockSpec((tm, f), lambda i: (i, 0)),
            pl.BlockSpec((f, d), lambda i: (0, 0), pipeline_mode=pl.Buffered(1)),
            pl.BlockSpec((tm, d), lambda i: (i, 0)),
            pl.BlockSpec((1, d), lambda i: (0, 0)),
        ],
        out_specs=pl.BlockSpec((tm, d), lambda i: (i, 0)),
        out_shape=jax.ShapeDtypeStruct((m, d), F32),
        compiler_params=_cparams(("parallel",)),
        name="ffn_down",
    )(a, w, h, fw.reshape(1, d))


def kernel(x, norm1_w, w_in, hg_lb_logits, hg_onorm_w, lambda_q1, lambda_k1, lambda_q2, lambda_k2,
           da_subln_w, rel_bias, w_out, norm2_w, w_gate, w_up, w_down, final_norm_w):
    batch, seq, d_model = x.shape
    depth = w_in.shape[0]
    hg_width = hg_lb_logits.shape[-1]
    da_width = d_model - hg_width
    hg_heads = hg_width // HG_HEAD_DIM
    da_heads = da_width // (2 * DA_HEAD_DIM)
    assert w_in.shape[2] == 5 * hg_width + 3 * da_width
    assert seq % HG_CHUNK == 0 and seq % BIAS_TILE == 0
    m = batch * seq
    blk = LANES

    tiles = _bias_tiles(rel_bias)
    h = x.reshape(m, d_model)
    for l in range(depth):
        main, gates = _in_proj(h, norm1_w[l], w_in[l].astype(BF16), 2 * hg_width, 4 * hg_width,
                               TILES["in_proj_m"], TILES["in_proj_n"])

        o_hg = _hgrn(main, gates, hg_lb_logits, hg_onorm_w[l], batch=batch, seq=seq, heads=hg_heads,
                     layer=l, col_q=0, col_i=hg_width // blk, col_g=2 * hg_width // blk)

        lam_init = 0.8 - 0.6 * math.exp(-0.3 * l)
        lam_vecs = jnp.stack([lambda_q1[l], lambda_k1[l], lambda_q2[l], lambda_k2[l]]).astype(F32)
        base = 3 * hg_width // blk
        o_da = _diff_attn(main, lam_vecs, tiles, da_subln_w[l], batch=batch, seq=seq, heads=da_heads,
                          lam_init=lam_init, col_q=base, col_k=base + da_width // blk,
                          col_v=base + 2 * da_width // blk, tq=TILES["attn_q"])

        h, u2, w_down_bf16 = _out_proj(h, o_hg, o_da, w_out[l].astype(BF16), norm2_w[l], w_down, l,
                                       TILES["out_proj_m"])

        act = _ffn_up(u2, w_gate, w_up, l, TILES["ffn_up_m"], TILES["ffn_up_n"])
        last = l == depth - 1
        assert last, "final norm is fused into the last layer's down projection"
        h = _ffn_down(act, w_down_bf16, h, final_norm_w, TILES["ffn_down_m"])
    return h.reshape(batch, seq, d_model)
```

```python
import functools
import math

import numpy as np
import jax
import jax.numpy as jnp
from jax import lax
from jax.experimental import pallas as pl
from jax.experimental.pallas import tpu as pltpu

F32 = jnp.float32
BF16 = jnp.bfloat16

EPS = 1e-6
LOG2E = math.log2(math.e)
HG_HEAD_DIM = 128
DA_HEAD_DIM = 64
N_BUCKETS = 32
MAX_DISTANCE = 128

LANES = 128
SUBLANES = 8
HG_CHUNK = 64
HG_LEVELS = 6
HG_GROUP = 2
HG_UNROLL = 16
BIAS_TILE = 128
VMEM_LIMIT = 56 * 1024 * 1024

TILES = {
    "in_proj_m": 256, "in_proj_n": 1024,
    "attn_q": 256,
    "out_proj_m": 512,
    "ffn_up_m": 2048, "ffn_up_n": 512,
    "ffn_down_m": 512,
}


def _cparams(sem, flags=None):
    return pltpu.CompilerParams(dimension_semantics=sem, vmem_limit_bytes=VMEM_LIMIT, flags=flags)


def _dot(a, b):
    return jnp.dot(a, b, preferred_element_type=F32)


def _dot_nt(a, b):
    return lax.dot_general(a, b, (((1,), (1,)), ((), ())), preferred_element_type=F32)


def _dot_tn(a, b):
    return lax.dot_general(a, b, (((0,), (0,)), ((), ())), preferred_element_type=F32)


def _sigmoid(x):
    return 0.5 * jnp.tanh(0.5 * x) + 0.5


def _silu(x):
    h = 0.5 * x
    return h + h * jnp.tanh(h)


def _in_proj_kernel(x_ref, nw_ref, w_ref, main_ref, gates_ref, *, gate_lo, gate_hi, tn):
    x = x_ref[...].astype(F32)
    ms = jnp.mean(x * x, axis=-1, keepdims=True)
    u = (x * lax.rsqrt(ms + EPS) * nw_ref[...]).astype(BF16)

    n = w_ref.shape[1]
    slab = {id(main_ref): 0, id(gates_ref): 0}
    for c0 in range(0, n, tn):
        o_ref = gates_ref if gate_lo <= c0 < gate_hi else main_ref
        r = _dot(u, w_ref[:, c0:c0 + tn]).astype(o_ref.dtype)
        for c in range(tn // LANES):
            o_ref[slab[id(o_ref)]] = r[:, c * LANES:(c + 1) * LANES]
            slab[id(o_ref)] += 1


def _in_proj(x, nw, w, gate_lo, gate_hi, tm, tn):
    m, d = x.shape
    n = w.shape[1]
    assert gate_lo % tn == 0 and gate_hi % tn == 0 and n % tn == 0
    n_gate_cols = gate_hi - gate_lo
    kern = functools.partial(_in_proj_kernel, gate_lo=gate_lo, gate_hi=gate_hi, tn=tn)
    return pl.pallas_call(
        kern,
        grid=(m // tm,),
        in_specs=[
            pl.BlockSpec((tm, d), lambda i: (i, 0)),
            pl.BlockSpec((1, d), lambda i: (0, 0)),
            pl.BlockSpec((d, n), lambda i: (0, 0), pipeline_mode=pl.Buffered(1)),
        ],
        out_specs=[
            pl.BlockSpec(((n - n_gate_cols) // LANES, tm, LANES), lambda i: (0, i, 0)),
            pl.BlockSpec((n_gate_cols // LANES, tm, LANES), lambda i: (0, i, 0)),
        ],
        out_shape=[
            jax.ShapeDtypeStruct(((n - n_gate_cols) // LANES, m, LANES), BF16),
            jax.ShapeDtypeStruct((n_gate_cols // LANES, m, LANES), F32),
        ],
        compiler_params=_cparams(("parallel",)),
        name="in_proj",
    )(x, nw.reshape(1, d), w)


def _hgrn_constants():
    c, nl, g = HG_CHUNK, HG_LEVELS, HG_GROUP
    idx = np.arange(g * c)
    same_chunk = (idx[:, None] // c) == (idx[None, :] // c)
    tri = (same_chunk & (idx[:, None] >= idx[None, :])).astype(np.float32)
    masks = np.zeros((nl + 1, g * c, g * c), np.float32)
    masks[0] = np.eye(g * c, dtype=np.float32)
    for lvl in range(nl):
        m = 1 << lvl
        same_pair = (idx[:, None] // (2 * m)) == (idx[None, :] // (2 * m))
        other_sibling = (idx[:, None] // m) != (idx[None, :] // m)
        masks[lvl + 1] = (same_pair & other_sibling).astype(np.float32)
    return tri, masks


def _level_operands(lvl, j, q, f_f, f_b, k_f, k_b, beta_f, bx_b, odd, hi2, hi4):
    sl = slice(j * SUBLANES, (j + 1) * SUBLANES)
    qj, ffj, fbj, kfj, kbj = q[sl], f_f[sl], f_b[sl], k_f[sl], k_b[sl]
    if lvl == 0:
        return qj * jnp.where(odd, ffj, fbj), jnp.where(odd, kbj, kfj)
    if lvl == 1:
        up_f, dn_f = pltpu.roll(ffj, SUBLANES - 1, axis=0), pltpu.roll(ffj, 1, axis=0)
        up_b, dn_b = pltpu.roll(fbj, SUBLANES - 1, axis=0), pltpu.roll(fbj, 1, axis=0)
        fq = jnp.where(hi2, jnp.where(odd, ffj * dn_f, ffj), jnp.where(odd, fbj, fbj * up_b))
        ks = jnp.where(hi2, jnp.where(odd, kbj * dn_b, kbj), jnp.where(odd, kfj, kfj * up_f))
        return qj * fq, ks
    bfj, bbj = beta_f[sl], bx_b[sl]
    if lvl == 2:
        r = j * SUBLANES + SUBLANES // 2
        ef = jnp.exp2(-jnp.abs(bfj - beta_f[r - 1:r]))
        eb = jnp.exp2(-jnp.abs(bbj - bx_b[r:r + 1]))
        return qj * jnp.where(hi4, ef, eb), jnp.where(hi4, kbj * eb, kfj * ef)
    mb = (1 << lvl) // SUBLANES
    r = ((j // (2 * mb)) * 2 * mb + mb) * SUBLANES
    ref_f, ref_b = beta_f[r - 1:r], bx_b[r:r + 1]
    if (j // mb) & 1:
        return qj * jnp.exp2(bfj - ref_f), kbj * jnp.exp2(bbj - ref_b)
    return qj * jnp.exp2(ref_b - bbj), kfj * jnp.exp2(ref_f - bfj)


def _hgrn_kernel(lg_ref, hq_ref, hi_ref, hg_ref, gf_ref, gb_ref, ow_ref, tri_ref, mk_ref,
                 out_ref, o_scr, qi_scr, ut_scr, dec_scr, st_scr, g_scr, *, layer, n_slots):
    c, nl, dh = HG_CHUNK, HG_LEVELS, HG_HEAD_DIM
    seq = hq_ref.shape[0]
    nc = seq // c

    lg = lg_ref[...].astype(F32)
    lbs = []
    for d in range(2):
        rows = lg[d * n_slots:(d + 1) * n_slots]
        e = jnp.exp(rows - jnp.max(rows, axis=0, keepdims=True))
        lbs.append(jnp.sum(e[:layer + 1], axis=0, keepdims=True) / jnp.sum(e, axis=0, keepdims=True))
    lb_f, lb_b = lbs

    pos = lax.broadcasted_iota(jnp.int32, (SUBLANES, dh), 0)
    odd = (pos & 1) != 0
    hi2 = (pos & 2) != 0
    hi4 = (pos & 4) != 0

    def split(x):
        hi = x.astype(BF16)
        return hi, (x - hi.astype(F32)).astype(BF16)

    gc = HG_GROUP * c
    ng = nc // HG_GROUP

    def gates(gi, carry):
        r0 = pl.multiple_of(gi * gc, gc)
        f_f = lb_f + (1.0 - lb_f) * _sigmoid(gf_ref[pl.ds(r0, gc), :])
        f_b = lb_b + (1.0 - lb_b) * _sigmoid(gb_ref[pl.ds(r0, gc), :])
        lf_f = jnp.log2(f_f)
        lf_b = jnp.log2(f_b)
        pre = _dot(tri_ref[...], jnp.concatenate(split(lf_f) + split(lf_b), axis=1))
        g_scr[0, pl.ds(r0, gc), :] = f_f
        g_scr[1, pl.ds(r0, gc), :] = f_b
        g_scr[2, pl.ds(r0, gc), :] = pre[:, :dh] + pre[:, dh:2 * dh]
        g_scr[3, pl.ds(r0, gc), :] = pre[:, 2 * dh:3 * dh] + pre[:, 3 * dh:] - lf_b
        return carry

    lax.fori_loop(0, ng, gates, 0, unroll=2 * HG_UNROLL // HG_GROUP)

    def per_chunk_rows(x, row):
        return jnp.concatenate([jnp.broadcast_to(x[g * c + row:g * c + row + 1], (c, dh))
                                for g in range(HG_GROUP)], axis=0)

    def intra(gi, carry):
        r0 = pl.multiple_of(gi * gc, gc)
        q = _silu(hq_ref[pl.ds(r0, gc), :].astype(F32))
        v = hi_ref[pl.ds(r0, gc), :]
        f_f = g_scr[0, pl.ds(r0, gc), :]
        f_b = g_scr[1, pl.ds(r0, gc), :]
        beta_f = g_scr[2, pl.ds(r0, gc), :]
        bx_b = g_scr[3, pl.ds(r0, gc), :]
        k_f = 1.0 - f_f
        k_b = 1.0 - f_b
        tot_f = per_chunk_rows(beta_f, c - 1)
        tot_b = per_chunk_rows(bx_b + jnp.log2(f_b), c - 1)

        a = mk_ref[0] * _dot_nt(q.astype(BF16), (k_f + k_b).astype(BF16))
        for lvl in range(nl):
            ops = [_level_operands(lvl, j, q, f_f, f_b, k_f, k_b, beta_f, bx_b, odd, hi2, hi4)
                   for j in range(gc // SUBLANES)]
            qs = jnp.concatenate([o[0] for o in ops], axis=0)
            ks = jnp.concatenate([o[1] for o in ops], axis=0)
            a = a + mk_ref[lvl + 1] * _dot_nt(qs.astype(BF16), ks.astype(BF16))
        o_scr[pl.ds(r0, gc), :] = _dot(a.astype(BF16), v)

        qi = jnp.concatenate([q * jnp.exp2(beta_f), q * jnp.exp2(tot_b - bx_b)], axis=1)
        qi_scr[pl.ds(r0, gc), :] = qi.astype(BF16)
        ks = jnp.concatenate([k_f * jnp.exp2(tot_f - beta_f), k_b * jnp.exp2(bx_b)], axis=1).astype(BF16)
        zero = jnp.zeros((c, dh), BF16)
        vd = jnp.concatenate([jnp.concatenate([v[g * c:(g + 1) * c] if h == g else zero
                                               for h in range(HG_GROUP)], axis=1)
                              for g in range(HG_GROUP)], axis=0)
        ut = _dot_tn(vd, ks)
        dec = jnp.exp2(jnp.concatenate([tot_f, tot_b], axis=1))
        for g in range(HG_GROUP):
            ut_scr[gi * HG_GROUP + g] = ut[g * dh:(g + 1) * dh]
            dec_scr[gi * HG_GROUP + g] = dec[g * c:g * c + 8]
        return carry

    lax.fori_loop(0, ng, intra, 0, unroll=HG_UNROLL // HG_GROUP)

    def states(i, carry):
        st_f, st_b = carry
        cf = i
        cb = nc - 1 - i
        st_scr[cf, :, :dh] = st_f.astype(BF16)
        st_scr[cb, :, dh:] = st_b.astype(BF16)
        st_f = st_f * dec_scr[cf, 0:1, :dh] + ut_scr[cf, :, :dh]
        st_b = st_b * dec_scr[cb, 0:1, dh:] + ut_scr[cb, :, dh:]
        return st_f, st_b

    zero = jnp.zeros((dh, dh), F32)
    lax.fori_loop(0, nc, states, (zero, zero))

    ow = ow_ref[...].astype(F32)

    def finish(ci, carry):
        r0 = pl.multiple_of(ci * c, c)
        o = o_scr[pl.ds(r0, c), :] + _dot_nt(qi_scr[pl.ds(r0, c), :], st_scr[ci])
        y = o * lax.rsqrt(jnp.mean(o * o, axis=-1, keepdims=True) + EPS) * ow
        out_ref[pl.ds(r0, c), :] = (y * _silu(hg_ref[pl.ds(r0, c), :].astype(F32))).astype(out_ref.dtype)
        return carry

    lax.fori_loop(0, nc, finish, 0, unroll=2 * HG_UNROLL)


def _hgrn(main, gates, lb_logits, onorm_w, *, batch, seq, heads, layer, col_q, col_i, col_g):
    dh, c = HG_HEAD_DIM, HG_CHUNK
    n_slots = lb_logits.shape[1]
    lg = lb_logits.reshape(2 * n_slots, heads * dh)
    tri, masks = _hgrn_constants()
    nc = seq // c
    kern = functools.partial(_hgrn_kernel, layer=layer, n_slots=n_slots)
    return pl.pallas_call(
        kern,
        grid=(batch, heads),
        in_specs=[
            pl.BlockSpec((2 * n_slots, dh), lambda b, h: (0, h)),
            pl.BlockSpec((None, seq, dh), lambda b, h: (col_q + h, b, 0)),
            pl.BlockSpec((None, seq, dh), lambda b, h: (col_i + h, b, 0)),
            pl.BlockSpec((None, seq, dh), lambda b, h: (col_g + h, b, 0)),
            pl.BlockSpec((None, seq, dh), lambda b, h: (h, b, 0)),
            pl.BlockSpec((None, seq, dh), lambda b, h: (heads + h, b, 0)),
            pl.BlockSpec((1, dh), lambda b, h: (0, 0)),
            pl.BlockSpec(tri.shape, lambda b, h: (0, 0)),
            pl.BlockSpec(masks.shape, lambda b, h: (0, 0, 0)),
        ],
        out_specs=pl.BlockSpec((None, seq, dh), lambda b, h: (h, b, 0)),
        out_shape=jax.ShapeDtypeStruct((heads, batch * seq, dh), BF16),
        scratch_shapes=[
            pltpu.VMEM((seq, dh), F32),
            pltpu.VMEM((seq, 2 * dh), BF16),
            pltpu.VMEM((nc, dh, 2 * dh), F32),
            pltpu.VMEM((nc, 8, 2 * dh), F32),
            pltpu.VMEM((nc, dh, 2 * dh), BF16),
            pltpu.VMEM((4, seq, dh), F32),
        ],
        compiler_params=_cparams(("parallel", "parallel")),
        name="hgrn2",
    )(lg, main, main, main, gates, gates, onorm_w.reshape(1, dh),
      jnp.asarray(tri, BF16), jnp.asarray(masks, F32))


def _rel_bucket_index(rel):
    nb = N_BUCKETS // 2
    max_exact = nb // 2
    ret = jnp.where(rel > 0, nb, 0)
    n = jnp.abs(rel)
    nf = jnp.maximum(n, 1).astype(jnp.float32)
    large = max_exact + (jnp.log(nf / max_exact) / math.log(MAX_DISTANCE / max_exact)
                         * (nb - max_exact)).astype(jnp.int32)
    large = jnp.minimum(large, nb - 1)
    return ret + jnp.where(n < max_exact, n, large)


def _bias_tiles_kernel(tbl_ref, bkt_ref, out_ref):
    h = pl.program_id(0)
    bkt = bkt_ref[...]
    acc = jnp.zeros(bkt.shape, F32)
    for cidx in range(N_BUCKETS):
        acc = jnp.where(bkt == cidx, tbl_ref[cidx, h], acc)
    out_ref[0] = acc * LOG2E


def _bias_tiles(rel_bias):
    t = BIAS_TILE
    assert t >= MAX_DISTANCE
    heads = rel_bias.shape[1]
    key = jnp.arange(t, dtype=jnp.int32)[:, None]
    qry = jnp.arange(t, dtype=jnp.int32)[None, :]
    rel = jnp.stack([t * d + key - qry for d in range(-2, 3)])
    bkt = _rel_bucket_index(rel).astype(jnp.int32)
    return pl.pallas_call(
        _bias_tiles_kernel,
        grid=(heads,),
        in_specs=[
            pl.BlockSpec(memory_space=pltpu.SMEM),
            pl.BlockSpec((5, t, t), lambda h: (0, 0, 0)),
        ],
        out_specs=pl.BlockSpec((1, 5, t, t), lambda h: (h, 0, 0, 0)),
        out_shape=jax.ShapeDtypeStruct((heads, 5, t, t), F32),
        compiler_params=_cparams(("arbitrary",)),
        name="bias_tiles",
    )(rel_bias.astype(F32), bkt)


def _diff_attn_kernel(lam_ref, q_ref, k_ref, v_ref, tiles_ref, sw_ref, out_ref, vaug_scr,
                      s0_scr, s1_scr, m0_scr, m1_scr, e0_scr, e1_scr, *, lam_init, tq):
    seq, d2 = q_ref.shape
    t = BIAS_TILE
    s_scr, m_scr, e_scr = (s0_scr, s1_scr), (m0_scr, m1_scr), (e0_scr, e1_scr)

    vaug_scr[:d2, :] = v_ref[...].T
    vaug_scr[d2:, :] = jnp.ones((vaug_scr.shape[0] - d2, seq), BF16)

    lv = lam_ref[...].astype(F32)
    lam = (jnp.exp(jnp.sum(lv[0:1] * lv[1:2], axis=-1, keepdims=True))
           - jnp.exp(jnp.sum(lv[2:3] * lv[3:4], axis=-1, keepdims=True)) + lam_init)

    c = DA_HEAD_DIM ** -0.5 * LOG2E
    lane = lax.broadcasted_iota(jnp.int32, (tq, d2), 1)
    first = lane < DA_HEAD_DIM
    sw = sw_ref[...].astype(F32) * (1.0 - lam_init)

    def scores(n, slot):
        r0 = n * tq if isinstance(n, int) else pl.multiple_of(n * tq, tq)
        q = q_ref[pl.ds(r0, tq), :]
        zero = jnp.zeros_like(q)
        cols = []
        for ib in range(tq // t):
            d = [jnp.clip(jb - (n * (tq // t) + ib), -2, 2) + 2 for jb in range(seq // t)]
            cols.append(jnp.concatenate([tiles_ref[0, dj] for dj in d], axis=0))
        bias = jnp.concatenate(cols, axis=1)
        qq = jnp.concatenate([jnp.where(first, q, zero), jnp.where(first, zero, q)], axis=0)
        st = _dot_nt(k_ref[...], qq) * c
        st = jnp.concatenate([st[:, :tq] + bias, st[:, tq:] + bias], axis=1)
        s_scr[slot][...] = st
        m_scr[slot][...] = jnp.max(st, axis=0, keepdims=True)

    def numerators(slot):
        e_scr[slot][...] = jnp.exp2(s_scr[slot][...] - m_scr[slot][...]).astype(BF16)

    def values(n, slot):
        r0 = n * tq if isinstance(n, int) else pl.multiple_of(n * tq, tq)
        ot = _dot(vaug_scr[...], e_scr[slot][...])
        on = ot[:d2] / ot[d2:d2 + 1]
        o = on[:, :tq] - lam * on[:, tq:]
        yt = o * lax.rsqrt(jnp.mean(o * o, axis=0, keepdims=True) + EPS)
        out_ref[pl.ds(r0, tq), :] = (yt.T * sw).astype(out_ref.dtype)

    nq = seq // tq
    assert nq % 2 == 0 and nq >= 4
    scores(0, 0)
    scores(1, 1)
    numerators(0)

    def pair(i, carry):
        scores(2 * i + 2, 0)
        numerators(1)
        values(2 * i, 0)
        scores(2 * i + 3, 1)
        numerators(0)
        values(2 * i + 1, 1)
        return carry

    for i in range(nq // 2 - 1):
        pair(i, 0)
    numerators(1)
    values(nq - 2, 0)
    values(nq - 1, 1)


def _diff_attn(main, lam_vecs, tiles, subln_w, *, batch, seq, heads, lam_init, col_q, col_k, col_v, tq):
    d2 = 2 * DA_HEAD_DIM
    t = BIAS_TILE
    kern = functools.partial(_diff_attn_kernel, lam_init=lam_init, tq=tq)
    return pl.pallas_call(
        kern,
        grid=(heads, batch),
        in_specs=[
            pl.BlockSpec(lam_vecs.shape, lambda h, b: (0, 0)),
            pl.BlockSpec((None, seq, d2), lambda h, b: (col_q + h, b, 0)),
            pl.BlockSpec((None, seq, d2), lambda h, b: (col_k + h, b, 0)),
            pl.BlockSpec((None, seq, d2), lambda h, b: (col_v + h, b, 0)),
            pl.BlockSpec((1, 5, t, t), lambda h, b: (h, 0, 0, 0)),
            pl.BlockSpec((1, d2), lambda h, b: (0, 0)),
        ],
        out_specs=pl.BlockSpec((None, seq, d2), lambda h, b: (h, b, 0)),
        out_shape=jax.ShapeDtypeStruct((heads, batch * seq, d2), BF16),
        scratch_shapes=[pltpu.VMEM((d2 + 2 * SUBLANES, seq), BF16),
                        pltpu.VMEM((seq, 2 * tq), F32), pltpu.VMEM((seq, 2 * tq), F32),
                        pltpu.VMEM((1, 2 * tq), F32), pltpu.VMEM((1, 2 * tq), F32),
                        pltpu.VMEM((seq, 2 * tq), BF16), pltpu.VMEM((seq, 2 * tq), BF16)],
        compiler_params=_cparams(("parallel", "parallel")),
        name="diff_attn",
    )(lam_vecs, main, main, main, tiles, subln_w.reshape(1, d2))


def _out_proj_kernel(x_ref, a_ref, b_ref, wa_ref, wb_ref, nw_ref, wd_ref, h_ref, u_ref, wd_out_ref):
    def heads_to_lanes(ref):
        return jnp.concatenate([ref[c] for c in range(ref.shape[0])], axis=1)

    h = x_ref[...] + _dot(heads_to_lanes(a_ref), wa_ref[...]) + _dot(heads_to_lanes(b_ref), wb_ref[...])
    h_ref[...] = h
    ms = jnp.mean(h * h, axis=-1, keepdims=True)
    u_ref[...] = (h * lax.rsqrt(ms + EPS) * nw_ref[...]).astype(u_ref.dtype)
    wd_out_ref[...] = wd_ref[...].astype(wd_out_ref.dtype)


def _out_proj(x, a, b, w, nw, wd, layer, tm):
    m, d = x.shape
    ka, kb = a.shape[0] * a.shape[2], b.shape[0] * b.shape[2]
    assert ka == kb and w.shape == (ka + kb, d)
    steps = m // tm
    f = wd.shape[1]
    rows = f // steps
    assert rows * steps == f and rows % (2 * SUBLANES) == 0
    resident = pl.Buffered(1)
    return pl.pallas_call(
        _out_proj_kernel,
        grid=(steps,),
        in_specs=[
            pl.BlockSpec((tm, d), lambda i: (i, 0)),
            pl.BlockSpec((a.shape[0], tm, a.shape[2]), lambda i: (0, i, 0)),
            pl.BlockSpec((b.shape[0], tm, b.shape[2]), lambda i: (0, i, 0)),
            pl.BlockSpec((ka, d), lambda i: (0, 0), pipeline_mode=resident),
            pl.BlockSpec((kb, d), lambda i: (1, 0), pipeline_mode=resident),
            pl.BlockSpec((1, d), lambda i: (0, 0)),
            pl.BlockSpec((None, rows, wd.shape[2]), lambda i: (layer, i, 0)),
        ],
        out_specs=[pl.BlockSpec((tm, d), lambda i: (i, 0)), pl.BlockSpec((tm, d), lambda i: (i, 0)),
                   pl.BlockSpec((rows, wd.shape[2]), lambda i: (i, 0))],
        out_shape=[jax.ShapeDtypeStruct((m, d), F32), jax.ShapeDtypeStruct((m, d), BF16),
                   jax.ShapeDtypeStruct(wd.shape[1:], BF16)],
        compiler_params=_cparams(("parallel",)),
        name="out_proj",
    )(x, a, b, w, w, nw.reshape(1, d), wd)


def _ffn_up_kernel(u_ref, wg_ref, wu_ref, o_ref, wg_scr, wu_scr):
    @pl.when(pl.program_id(1) == 0)
    def _():
        wg_scr[...] = wg_ref[...].astype(BF16)
        wu_scr[...] = wu_ref[...].astype(BF16)

    sub = min(512, u_ref.shape[0])
    for r in range(u_ref.shape[0] // sub):
        rows = slice(r * sub, (r + 1) * sub)
        u = u_ref[rows, :]
        g = _dot(u, wg_scr[...])
        up = _dot(u, wu_scr[...])
        o_ref[rows, :] = (_silu(g) * up).astype(o_ref.dtype)


def _ffn_up(u, wg, wu, layer, tm, tn):
    m, d = u.shape
    n = wg.shape[2]
    return pl.pallas_call(
        _ffn_up_kernel,
        grid=(n // tn, m // tm),
        in_specs=[
            pl.BlockSpec((tm, d), lambda j, i: (i, 0)),
            pl.BlockSpec((None, d, tn), lambda j, i: (layer, 0, j)),
            pl.BlockSpec((None, d, tn), lambda j, i: (layer, 0, j)),
        ],
        out_specs=pl.BlockSpec((tm, tn), lambda j, i: (i, j)),
        out_shape=jax.ShapeDtypeStruct((m, n), BF16),
        scratch_shapes=[pltpu.VMEM((d, tn), BF16), pltpu.VMEM((d, tn), BF16)],
        compiler_params=_cparams(("parallel", "arbitrary")),
        name="ffn_up",
    )(u, wg, wu)


def _ffn_down_kernel(a_ref, w_ref, h_ref, fw_ref, o_ref):
    y = h_ref[...] + _dot(a_ref[...], w_ref[...])
    ms = jnp.mean(y * y, axis=-1, keepdims=True)
    o_ref[...] = y * lax.rsqrt(ms + EPS) * fw_ref[...]


def _ffn_down(a, w, h, fw, tm):
    m, f = a.shape
    d = w.shape[1]
    return pl.pallas_call(
        _ffn_down_kernel,
        grid=(m // tm,),
        in_specs=[
            pl.BlockSpec((tm, f), lambda i: (i, 0)),
            pl.BlockSpec((f, d), lambda i: (0, 0), pipeline_mode=pl.Buffered(1)),
            pl.BlockSpec((tm, d), lambda i: (i, 0)),
            pl.BlockSpec((1, d), lambda i: (0, 0)),
        ],
        out_specs=pl.BlockSpec((tm, d), lambda i: (i, 0)),
        out_shape=jax.ShapeDtypeStruct((m, d), F32),
        compiler_params=_cparams(("parallel",)),
        name="ffn_down",
    )(a, w, h, fw.reshape(1, d))


def kernel(x, norm1_w, w_in, hg_lb_logits, hg_onorm_w, lambda_q1, lambda_k1, lambda_q2, lambda_k2,
           da_subln_w, rel_bias, w_out, norm2_w, w_gate, w_up, w_down, final_norm_w):
    batch, seq, d_model = x.shape
    depth = w_in.shape[0]
    hg_width = hg_lb_logits.shape[-1]
    da_width = d_model - hg_width
    hg_heads = hg_width // HG_HEAD_DIM
    da_heads = da_width // (2 * DA_HEAD_DIM)
    assert w_in.shape[2] == 5 * hg_width + 3 * da_width
    assert seq % HG_CHUNK == 0 and seq % BIAS_TILE == 0
    m = batch * seq
    blk = LANES

    tiles = _bias_tiles(rel_bias)
    h = x.reshape(m, d_model)
    for l in range(depth):
        main, gates = _in_proj(h, norm1_w[l], w_in[l].astype(BF16), 2 * hg_width, 4 * hg_width,
                               TILES["in_proj_m"], TILES["in_proj_n"])

        o_hg = _hgrn(main, gates, hg_lb_logits, hg_onorm_w[l], batch=batch, seq=seq, heads=hg_heads,
                     layer=l, col_q=0, col_i=hg_width // blk, col_g=2 * hg_width // blk)

        lam_init = 0.8 - 0.6 * math.exp(-0.3 * l)
        lam_vecs = jnp.stack([lambda_q1[l], lambda_k1[l], lambda_q2[l], lambda_k2[l]]).astype(F32)
        base = 3 * hg_width // blk
        o_da = _diff_attn(main, lam_vecs, tiles, da_subln_w[l], batch=batch, seq=seq, heads=da_heads,
                          lam_init=lam_init, col_q=base, col_k=base + da_width // blk,
                          col_v=base + 2 * da_width // blk, tq=TILES["attn_q"])

        h, u2, w_down_bf16 = _out_proj(h, o_hg, o_da, w_out[l].astype(BF16), norm2_w[l], w_down, l,
                                       TILES["out_proj_m"])

        act = _ffn_up(u2, w_gate, w_up, l, TILES["ffn_up_m"], TILES["ffn_up_n"])
        last = l == depth - 1
        assert last, "final norm is fused into the last layer's down projection"
        h = _ffn_down(act, w_down_bf16, h, final_norm_w, TILES["ffn_down_m"])
    return h.reshape(batch, seq, d_model)
```

```python
import functools
import math

import numpy as np
import jax
import jax.numpy as jnp
from jax import lax
from jax.experimental import pallas as pl
from jax.experimental.pallas import tpu as pltpu

F32 = jnp.float32
BF16 = jnp.bfloat16

EPS = 1e-6
LOG2E = math.log2(math.e)
HG_HEAD_DIM = 128
DA_HEAD_DIM = 64
N_BUCKETS = 32
MAX_DISTANCE = 128

LANES = 128
SUBLANES = 8
HG_CHUNK = 64
HG_LEVELS = 6
HG_GROUP = 2
HG_UNROLL = 16
BIAS_TILE = 128
ATTN_SLOTS = 2
VMEM_LIMIT = 56 * 1024 * 1024

TILES = {
    "in_proj_m": 256, "in_proj_n": 1024,
    "attn_q": 256,
    "out_proj_m": 512,
    "ffn_up_m": 2048, "ffn_up_n": 512,
    "ffn_down_m": 512,
}


def _cparams(sem, flags=None):
    return pltpu.CompilerParams(dimension_semantics=sem, vmem_limit_bytes=VMEM_LIMIT, flags=flags)


def _dot(a, b):
    return jnp.dot(a, b, preferred_element_type=F32)


def _dot_nt(a, b):
    return lax.dot_general(a, b, (((1,), (1,)), ((), ())), preferred_element_type=F32)


def _dot_tn(a, b):
    return lax.dot_general(a, b, (((0,), (0,)), ((), ())), preferred_element_type=F32)


def _sigmoid(x):
    return 0.5 * jnp.tanh(0.5 * x) + 0.5


def _silu(x):
    h = 0.5 * x
    return h + h * jnp.tanh(h)


def _in_proj_kernel(x_ref, nw_ref, w_ref, main_ref, gates_ref, *, gate_lo, gate_hi, tn):
    x = x_ref[...].astype(F32)
    ms = jnp.mean(x * x, axis=-1, keepdims=True)
    u = (x * lax.rsqrt(ms + EPS) * nw_ref[...]).astype(BF16)

    n = w_ref.shape[1]
    slab = {id(main_ref): 0, id(gates_ref): 0}
    for c0 in range(0, n, tn):
        o_ref = gates_ref if gate_lo <= c0 < gate_hi else main_ref
        r = _dot(u, w_ref[:, c0:c0 + tn]).astype(o_ref.dtype)
        for c in range(tn // LANES):
            o_ref[slab[id(o_ref)]] = r[:, c * LANES:(c + 1) * LANES]
            slab[id(o_ref)] += 1


def _in_proj(x, nw, w, gate_lo, gate_hi, tm, tn):
    m, d = x.shape
    n = w.shape[1]
    assert gate_lo % tn == 0 and gate_hi % tn == 0 and n % tn == 0
    n_gate_cols = gate_hi - gate_lo
    kern = functools.partial(_in_proj_kernel, gate_lo=gate_lo, gate_hi=gate_hi, tn=tn)
    return pl.pallas_call(
        kern,
        grid=(m // tm,),
        in_specs=[
            pl.BlockSpec((tm, d), lambda i: (i, 0)),
            pl.BlockSpec((1, d), lambda i: (0, 0)),
            pl.BlockSpec((d, n), lambda i: (0, 0), pipeline_mode=pl.Buffered(1)),
        ],
        out_specs=[
            pl.BlockSpec(((n - n_gate_cols) // LANES, tm, LANES), lambda i: (0, i, 0)),
            pl.BlockSpec((n_gate_cols // LANES, tm, LANES), lambda i: (0, i, 0)),
        ],
        out_shape=[
            jax.ShapeDtypeStruct(((n - n_gate_cols) // LANES, m, LANES), BF16),
            jax.ShapeDtypeStruct((n_gate_cols // LANES, m, LANES), F32),
        ],
        compiler_params=_cparams(("parallel",)),
        name="in_proj",
    )(x, nw.reshape(1, d), w)


def _hgrn_constants():
    c, nl, g = HG_CHUNK, HG_LEVELS, HG_GROUP
    idx = np.arange(g * c)
    same_chunk = (idx[:, None] // c) == (idx[None, :] // c)
    tri = (same_chunk & (idx[:, None] >= idx[None, :])).astype(np.float32)
    masks = np.zeros((nl + 1, g * c, g * c), np.float32)
    masks[0] = np.eye(g * c, dtype=np.float32)
    for lvl in range(nl):
        m = 1 << lvl
        same_pair = (idx[:, None] // (2 * m)) == (idx[None, :] // (2 * m))
        other_sibling = (idx[:, None] // m) != (idx[None, :] // m)
        masks[lvl + 1] = (same_pair & other_sibling).astype(np.float32)
    return tri, masks


def _level_operands(lvl, j, q, f_f, f_b, k_f, k_b, beta_f, bx_b, odd, hi2, hi4):
    sl = slice(j * SUBLANES, (j + 1) * SUBLANES)
    qj, ffj, fbj, kfj, kbj = q[sl], f_f[sl], f_b[sl], k_f[sl], k_b[sl]
    if lvl == 0:
        return qj * jnp.where(odd, ffj, fbj), jnp.where(odd, kbj, kfj)
    if lvl == 1:
        up_f, dn_f = pltpu.roll(ffj, SUBLANES - 1, axis=0), pltpu.roll(ffj, 1, axis=0)
        up_b, dn_b = pltpu.roll(fbj, SUBLANES - 1, axis=0), pltpu.roll(fbj, 1, axis=0)
        fq = jnp.where(hi2, jnp.where(odd, ffj * dn_f, ffj), jnp.where(odd, fbj, fbj * up_b))
        ks = jnp.where(hi2, jnp.where(odd, kbj * dn_b, kbj), jnp.where(odd, kfj, kfj * up_f))
        return qj * fq, ks
    bfj, bbj = beta_f[sl], bx_b[sl]
    if lvl == 2:
        r = j * SUBLANES + SUBLANES // 2
        ef = jnp.exp2(-jnp.abs(bfj - beta_f[r - 1:r]))
        eb = jnp.exp2(-jnp.abs(bbj - bx_b[r:r + 1]))
        return qj * jnp.where(hi4, ef, eb), jnp.where(hi4, kbj * eb, kfj * ef)
    mb = (1 << lvl) // SUBLANES
    r = ((j // (2 * mb)) * 2 * mb + mb) * SUBLANES
    ref_f, ref_b = beta_f[r - 1:r], bx_b[r:r + 1]
    if (j // mb) & 1:
        return qj * jnp.exp2(bfj - ref_f), kbj * jnp.exp2(bbj - ref_b)
    return qj * jnp.exp2(ref_b - bbj), kfj * jnp.exp2(ref_f - bfj)


def _hgrn_kernel(lg_ref, hq_ref, hi_ref, hg_ref, gf_ref, gb_ref, ow_ref, tri_ref, mk_ref,
                 out_ref, o_scr, qi_scr, ut_scr, dec_scr, st_scr, g_scr, *, layer, n_slots):
    c, nl, dh = HG_CHUNK, HG_LEVELS, HG_HEAD_DIM
    seq = hq_ref.shape[0]
    nc = seq // c

    lg = lg_ref[...].astype(F32)
    lbs = []
    for d in range(2):
        rows = lg[d * n_slots:(d + 1) * n_slots]
        e = jnp.exp(rows - jnp.max(rows, axis=0, keepdims=True))
        lbs.append(jnp.sum(e[:layer + 1], axis=0, keepdims=True) / jnp.sum(e, axis=0, keepdims=True))
    lb_f, lb_b = lbs

    pos = lax.broadcasted_iota(jnp.int32, (SUBLANES, dh), 0)
    odd = (pos & 1) != 0
    hi2 = (pos & 2) != 0
    hi4 = (pos & 4) != 0

    def split(x):
        hi = x.astype(BF16)
        return hi, (x - hi.astype(F32)).astype(BF16)

    gc = HG_GROUP * c
    ng = nc // HG_GROUP

    def gates(gi, carry):
        r0 = pl.multiple_of(gi * gc, gc)
        f_f = lb_f + (1.0 - lb_f) * _sigmoid(gf_ref[pl.ds(r0, gc), :])
        f_b = lb_b + (1.0 - lb_b) * _sigmoid(gb_ref[pl.ds(r0, gc), :])
        lf_f = jnp.log2(f_f)
        lf_b = jnp.log2(f_b)
        pre = _dot(tri_ref[...], jnp.concatenate(split(lf_f) + split(lf_b), axis=1))
        g_scr[0, pl.ds(r0, gc), :] = f_f
        g_scr[1, pl.ds(r0, gc), :] = f_b
        g_scr[2, pl.ds(r0, gc), :] = pre[:, :dh] + pre[:, dh:2 * dh]
        g_scr[3, pl.ds(r0, gc), :] = pre[:, 2 * dh:3 * dh] + pre[:, 3 * dh:] - lf_b
        return carry

    lax.fori_loop(0, ng, gates, 0, unroll=2 * HG_UNROLL // HG_GROUP)

    def per_chunk_rows(x, row):
        return jnp.concatenate([jnp.broadcast_to(x[g * c + row:g * c + row + 1], (c, dh))
                                for g in range(HG_GROUP)], axis=0)

    def intra(gi, carry):
        r0 = pl.multiple_of(gi * gc, gc)
        q = _silu(hq_ref[pl.ds(r0, gc), :].astype(F32))
        v = hi_ref[pl.ds(r0, gc), :]
        f_f = g_scr[0, pl.ds(r0, gc), :]
        f_b = g_scr[1, pl.ds(r0, gc), :]
        beta_f = g_scr[2, pl.ds(r0, gc), :]
        bx_b = g_scr[3, pl.ds(r0, gc), :]
        k_f = 1.0 - f_f
        k_b = 1.0 - f_b
        tot_f = per_chunk_rows(beta_f, c - 1)
        tot_b = per_chunk_rows(bx_b + jnp.log2(f_b), c - 1)

        a = mk_ref[0] * _dot_nt(q.astype(BF16), (k_f + k_b).astype(BF16))
        for lvl in range(nl):
            ops = [_level_operands(lvl, j, q, f_f, f_b, k_f, k_b, beta_f, bx_b, odd, hi2, hi4)
                   for j in range(gc // SUBLANES)]
            qs = jnp.concatenate([o[0] for o in ops], axis=0)
            ks = jnp.concatenate([o[1] for o in ops], axis=0)
            a = a + mk_ref[lvl + 1] * _dot_nt(qs.astype(BF16), ks.astype(BF16))
        o_scr[pl.ds(r0, gc), :] = _dot(a.astype(BF16), v)

        qi = jnp.concatenate([q * jnp.exp2(beta_f), q * jnp.exp2(tot_b - bx_b)], axis=1)
        qi_scr[pl.ds(r0, gc), :] = qi.astype(BF16)
        ks = jnp.concatenate([k_f * jnp.exp2(tot_f - beta_f), k_b * jnp.exp2(bx_b)], axis=1).astype(BF16)
        zero = jnp.zeros((c, dh), BF16)
        vd = jnp.concatenate([jnp.concatenate([v[g * c:(g + 1) * c] if h == g else zero
                                               for h in range(HG_GROUP)], axis=1)
                              for g in range(HG_GROUP)], axis=0)
        ut = _dot_tn(vd, ks)
        dec = jnp.exp2(jnp.concatenate([tot_f, tot_b], axis=1))
        for g in range(HG_GROUP):
            ut_scr[gi * HG_GROUP + g] = ut[g * dh:(g + 1) * dh]
            dec_scr[gi * HG_GROUP + g] = dec[g * c:g * c + 8]
        return carry

    lax.fori_loop(0, ng, intra, 0, unroll=HG_UNROLL // HG_GROUP)

    def states(i, carry):
        st_f, st_b = carry
        cf = i
        cb = nc - 1 - i
        st_scr[cf, :, :dh] = st_f.astype(BF16)
        st_scr[cb, :, dh:] = st_b.astype(BF16)
        st_f = st_f * dec_scr[cf, 0:1, :dh] + ut_scr[cf, :, :dh]
        st_b = st_b * dec_scr[cb, 0:1, dh:] + ut_scr[cb, :, dh:]
        return st_f, st_b

    zero = jnp.zeros((dh, dh), F32)
    lax.fori_loop(0, nc, states, (zero, zero), unroll=True)

    ow = ow_ref[...].astype(F32)

    def finish(ci, carry):
        r0 = pl.multiple_of(ci * c, c)
        o = o_scr[pl.ds(r0, c), :] + _dot_nt(qi_scr[pl.ds(r0, c), :], st_scr[ci])
        y = o * lax.rsqrt(jnp.mean(o * o, axis=-1, keepdims=True) + EPS) * ow
        out_ref[pl.ds(r0, c), :] = (y * _silu(hg_ref[pl.ds(r0, c), :].astype(F32))).astype(out_ref.dtype)
        return carry

    lax.fori_loop(0, nc, finish, 0, unroll=2 * HG_UNROLL)


def _hgrn(main, gates, lb_logits, onorm_w, *, batch, seq, heads, layer, col_q, col_i, col_g):
    dh, c = HG_HEAD_DIM, HG_CHUNK
    n_slots = lb_logits.shape[1]
    lg = lb_logits.reshape(2 * n_slots, heads * dh)
    tri, masks = _hgrn_constants()
    nc = seq // c
    kern = functools.partial(_hgrn_kernel, layer=layer, n_slots=n_slots)
    return pl.pallas_call(
        kern,
        grid=(batch, heads),
        in_specs=[
            pl.BlockSpec((2 * n_slots, dh), lambda b, h: (0, h)),
            pl.BlockSpec((None, seq, dh), lambda b, h: (col_q + h, b, 0)),
            pl.BlockSpec((None, seq, dh), lambda b, h: (col_i + h, b, 0)),
            pl.BlockSpec((None, seq, dh), lambda b, h: (col_g + h, b, 0)),
            pl.BlockSpec((None, seq, dh), lambda b, h: (h, b, 0)),
            pl.BlockSpec((None, seq, dh), lambda b, h: (heads + h, b, 0)),
            pl.BlockSpec((1, dh), lambda b, h: (0, 0)),
            pl.BlockSpec(tri.shape, lambda b, h: (0, 0)),
            pl.BlockSpec(masks.shape, lambda b, h: (0, 0, 0)),
        ],
        out_specs=pl.BlockSpec((None, seq, dh), lambda b, h: (h, b, 0)),
        out_shape=jax.ShapeDtypeStruct((heads, batch * seq, dh), BF16),
        scratch_shapes=[
            pltpu.VMEM((seq, dh), F32),
            pltpu.VMEM((seq, 2 * dh), BF16),
            pltpu.VMEM((nc, dh, 2 * dh), F32),
            pltpu.VMEM((nc, 8, 2 * dh), F32),
            pltpu.VMEM((nc, dh, 2 * dh), BF16),
            pltpu.VMEM((4, seq, dh), F32),
        ],
        compiler_params=_cparams(("parallel", "parallel")),
        name="hgrn2",
    )(lg, main, main, main, gates, gates, onorm_w.reshape(1, dh),
      jnp.asarray(tri, BF16), jnp.asarray(masks, F32))


def _rel_bucket_index(rel):
    nb = N_BUCKETS // 2
    max_exact = nb // 2
    ret = jnp.where(rel > 0, nb, 0)
    n = jnp.abs(rel)
    nf = jnp.maximum(n, 1).astype(jnp.float32)
    large = max_exact + (jnp.log(nf / max_exact) / math.log(MAX_DISTANCE / max_exact)
                         * (nb - max_exact)).astype(jnp.int32)
    large = jnp.minimum(large, nb - 1)
    return ret + jnp.where(n < max_exact, n, large)


def _bias_tiles_kernel(tbl_ref, bkt_ref, out_ref):
    h = pl.program_id(0)
    bkt = bkt_ref[...]
    acc = jnp.zeros(bkt.shape, F32)
    for cidx in range(N_BUCKETS):
        acc = jnp.where(bkt == cidx, tbl_ref[cidx, h], acc)
    out_ref[0] = acc * LOG2E


def _bias_tiles(rel_bias):
    t = BIAS_TILE
    assert t >= MAX_DISTANCE
    heads = rel_bias.shape[1]
    key = jnp.arange(t, dtype=jnp.int32)[:, None]
    qry = jnp.arange(t, dtype=jnp.int32)[None, :]
    rel = jnp.stack([t * d + key - qry for d in range(-2, 3)])
    bkt = _rel_bucket_index(rel).astype(jnp.int32)
    return pl.pallas_call(
        _bias_tiles_kernel,
        grid=(heads,),
        in_specs=[
            pl.BlockSpec(memory_space=pltpu.SMEM),
            pl.BlockSpec((5, t, t), lambda h: (0, 0, 0)),
        ],
        out_specs=pl.BlockSpec((1, 5, t, t), lambda h: (h, 0, 0, 0)),
        out_shape=jax.ShapeDtypeStruct((heads, 5, t, t), F32),
        compiler_params=_cparams(("arbitrary",)),
        name="bias_tiles",
    )(rel_bias.astype(F32), bkt)


def _diff_attn_kernel(lam_ref, q_ref, k_ref, v_ref, tiles_ref, sw_ref, out_ref, vaug_scr, *slot_scr, lam_init, tq):
    seq, d2 = q_ref.shape
    t = BIAS_TILE
    ns = ATTN_SLOTS
    s_scr, m_scr, e_scr = slot_scr[:ns], slot_scr[ns:2 * ns], slot_scr[2 * ns:]

    vaug_scr[:d2, :] = v_ref[...].T
    vaug_scr[d2:, :] = jnp.ones((vaug_scr.shape[0] - d2, seq), BF16)

    lv = lam_ref[...].astype(F32)
    lam = (jnp.exp(jnp.sum(lv[0:1] * lv[1:2], axis=-1, keepdims=True))
           - jnp.exp(jnp.sum(lv[2:3] * lv[3:4], axis=-1, keepdims=True)) + lam_init)

    c = DA_HEAD_DIM ** -0.5 * LOG2E
    lane = lax.broadcasted_iota(jnp.int32, (tq, d2), 1)
    first = lane < DA_HEAD_DIM
    sw = sw_ref[...].astype(F32) * (1.0 - lam_init)

    def scores(n, slot):
        r0 = n * tq if isinstance(n, int) else pl.multiple_of(n * tq, tq)
        q = q_ref[pl.ds(r0, tq), :]
        zero = jnp.zeros_like(q)
        cols = []
        for ib in range(tq // t):
            d = [jnp.clip(jb - (n * (tq // t) + ib), -2, 2) + 2 for jb in range(seq // t)]
            cols.append(jnp.concatenate([tiles_ref[0, dj] for dj in d], axis=0))
        bias = jnp.concatenate(cols, axis=1)
        qq = jnp.concatenate([jnp.where(first, q, zero), jnp.where(first, zero, q)], axis=0)
        st = _dot_nt(k_ref[...], qq) * c
        st = jnp.concatenate([st[:, :tq] + bias, st[:, tq:] + bias], axis=1)
        s_scr[slot][...] = st
        m_scr[slot][...] = jnp.max(st, axis=0, keepdims=True)

    def numerators(slot):
        e_scr[slot][...] = jnp.exp2(s_scr[slot][...] - m_scr[slot][...]).astype(BF16)

    def values(n, slot):
        r0 = n * tq if isinstance(n, int) else pl.multiple_of(n * tq, tq)
        ot = _dot(vaug_scr[...], e_scr[slot][...])
        on = ot[:d2] / ot[d2:d2 + 1]
        o = on[:, :tq] - lam * on[:, tq:]
        yt = o * lax.rsqrt(jnp.mean(o * o, axis=0, keepdims=True) + EPS)
        out_ref[pl.ds(r0, tq), :] = (yt.T * sw).astype(out_ref.dtype)

    nq = seq // tq
    scores(0, 0)
    scores(1, 1 % ns)
    numerators(0)
    for j in range(nq):
        if j + 2 < nq:
            scores(j + 2, (j + 2) % ns)
        if j + 1 < nq:
            numerators((j + 1) % ns)
        values(j, j % ns)


def _diff_attn(main, lam_vecs, tiles, subln_w, *, batch, seq, heads, lam_init, col_q, col_k, col_v, tq):
    d2 = 2 * DA_HEAD_DIM
    t = BIAS_TILE
    kern = functools.partial(_diff_attn_kernel, lam_init=lam_init, tq=tq)
    return pl.pallas_call(
        kern,
        grid=(heads, batch),
        in_specs=[
            pl.BlockSpec(lam_vecs.shape, lambda h, b: (0, 0)),
            pl.BlockSpec((None, seq, d2), lambda h, b: (col_q + h, b, 0)),
            pl.BlockSpec((None, seq, d2), lambda h, b: (col_k + h, b, 0)),
            pl.BlockSpec((None, seq, d2), lambda h, b: (col_v + h, b, 0)),
            pl.BlockSpec((1, 5, t, t), lambda h, b: (h, 0, 0, 0)),
            pl.BlockSpec((1, d2), lambda h, b: (0, 0)),
        ],
        out_specs=pl.BlockSpec((None, seq, d2), lambda h, b: (h, b, 0)),
        out_shape=jax.ShapeDtypeStruct((heads, batch * seq, d2), BF16),
        scratch_shapes=([pltpu.VMEM((d2 + 2 * SUBLANES, seq), BF16)]
                        + [pltpu.VMEM((seq, 2 * tq), F32)] * ATTN_SLOTS
                        + [pltpu.VMEM((1, 2 * tq), F32)] * ATTN_SLOTS
                        + [pltpu.VMEM((seq, 2 * tq), BF16)] * ATTN_SLOTS),
        compiler_params=_cparams(("parallel", "parallel")),
        name="diff_attn",
    )(lam_vecs, main, main, main, tiles, subln_w.reshape(1, d2))


def _out_proj_kernel(x_ref, a_ref, b_ref, wa_ref, wb_ref, nw_ref, wd_ref, h_ref, u_ref, wd_out_ref):
    def heads_to_lanes(ref):
        return jnp.concatenate([ref[c] for c in range(ref.shape[0])], axis=1)

    h = x_ref[...] + _dot(heads_to_lanes(a_ref), wa_ref[...]) + _dot(heads_to_lanes(b_ref), wb_ref[...])
    h_ref[...] = h
    ms = jnp.mean(h * h, axis=-1, keepdims=True)
    u_ref[...] = (h * lax.rsqrt(ms + EPS) * nw_ref[...]).astype(u_ref.dtype)
    wd_out_ref[...] = wd_ref[...].astype(wd_out_ref.dtype)


def _out_proj(x, a, b, w, nw, wd, layer, tm):
    m, d = x.shape
    ka, kb = a.shape[0] * a.shape[2], b.shape[0] * b.shape[2]
    assert ka == kb and w.shape == (ka + kb, d)
    steps = m // tm
    f = wd.shape[1]
    rows = f // steps
    assert rows * steps == f and rows % (2 * SUBLANES) == 0
    resident = pl.Buffered(1)
    return pl.pallas_call(
        _out_proj_kernel,
        grid=(steps,),
        in_specs=[
            pl.BlockSpec((tm, d), lambda i: (i, 0)),
            pl.BlockSpec((a.shape[0], tm, a.shape[2]), lambda i: (0, i, 0)),
            pl.BlockSpec((b.shape[0], tm, b.shape[2]), lambda i: (0, i, 0)),
            pl.BlockSpec((ka, d), lambda i: (0, 0), pipeline_mode=resident),
            pl.BlockSpec((kb, d), lambda i: (1, 0), pipeline_mode=resident),
            pl.BlockSpec((1, d), lambda i: (0, 0)),
            pl.BlockSpec((None, rows, wd.shape[2]), lambda i: (layer, i, 0)),
        ],
        out_specs=[pl.BlockSpec((tm, d), lambda i: (i, 0)), pl.BlockSpec((tm, d), lambda i: (i, 0)),
                   pl.BlockSpec((rows, wd.shape[2]), lambda i: (i, 0))],
        out_shape=[jax.ShapeDtypeStruct((m, d), F32), jax.ShapeDtypeStruct((m, d), BF16),
                   jax.ShapeDtypeStruct(wd.shape[1:], BF16)],
        compiler_params=_cparams(("parallel",)),
        name="out_proj",
    )(x, a, b, w, w, nw.reshape(1, d), wd)


def _ffn_up_kernel(u_ref, wg_ref, wu_ref, o_ref, wg_scr, wu_scr):
    @pl.when(pl.program_id(1) == 0)
    def _():
        wg_scr[...] = wg_ref[...].astype(BF16)
        wu_scr[...] = wu_ref[...].astype(BF16)

    sub = min(512, u_ref.shape[0])
    for r in range(u_ref.shape[0] // sub):
        rows = slice(r * sub, (r + 1) * sub)
        u = u_ref[rows, :]
        g = _dot(u, wg_scr[...])
        up = _dot(u, wu_scr[...])
        o_ref[rows, :] = (_silu(g) * up).astype(o_ref.dtype)


def _ffn_up(u, wg, wu, layer, tm, tn):
    m, d = u.shape
    n = wg.shape[2]
    return pl.pallas_call(
        _ffn_up_kernel,
        grid=(n // tn, m // tm),
        in_specs=[
            pl.BlockSpec((tm, d), lambda j, i: (i, 0)),
            pl.BlockSpec((None, d, tn), lambda j, i: (layer, 0, j)),
            pl.BlockSpec((None, d, tn), lambda j, i: (layer, 0, j)),
        ],
        out_specs=pl.BlockSpec((tm, tn), lambda j, i: (i, j)),
        out_shape=jax.ShapeDtypeStruct((m, n), BF16),
        scratch_shapes=[pltpu.VMEM((d, tn), BF16), pltpu.VMEM((d, tn), BF16)],
        compiler_params=_cparams(("parallel", "arbitrary")),
        name="ffn_up",
    )(u, wg, wu)


def _ffn_down_kernel(a_ref, w_ref, h_ref, fw_ref, o_ref):
    y = h_ref[...] + _dot(a_ref[...], w_ref[...])
    ms = jnp.mean(y * y, axis=-1, keepdims=True)
    o_ref[...] = y * lax.rsqrt(ms + EPS) * fw_ref[...]


def _ffn_down(a, w, h, fw, tm):
    m, f = a.shape
    d = w.shape[1]
    return pl.pallas_call(
        _ffn_down_kernel,
        grid=(m // tm,),
        in_specs=[
            pl.BlockSpec((tm, f), lambda i: (i, 0)),
            pl.BlockSpec((f, d), lambda i: (0, 0), pipeline_mode=pl.Buffered(1)),
            pl.BlockSpec((tm, d), lambda i: (i, 0)),
            pl.BlockSpec((1, d), lambda i: (0, 0)),
        ],
        out_specs=pl.BlockSpec((tm, d), lambda i: (i, 0)),
        out_shape=jax.ShapeDtypeStruct((m, d), F32),
        compiler_params=_cparams(("parallel",)),
        name="ffn_down",
    )(a, w, h, fw.reshape(1, d))


def kernel(x, norm1_w, w_in, hg_lb_logits, hg_onorm_w, lambda_q1, lambda_k1, lambda_q2, lambda_k2,
           da_subln_w, rel_bias, w_out, norm2_w, w_gate, w_up, w_down, final_norm_w):
    batch, seq, d_model = x.shape
    depth = w_in.shape[0]
    hg_width = hg_lb_logits.shape[-1]
    da_width = d_model - hg_width
    hg_heads = hg_width // HG_HEAD_DIM
    da_heads = da_width // (2 * DA_HEAD_DIM)
    assert w_in.shape[2] == 5 * hg_width + 3 * da_width
    assert seq % HG_CHUNK == 0 and seq % BIAS_TILE == 0
    m = batch * seq
    blk = LANES

    tiles = _bias_tiles(rel_bias)
    h = x.reshape(m, d_model)
    for l in range(depth):
        main, gates = _in_proj(h, norm1_w[l], w_in[l].astype(BF16), 2 * hg_width, 4 * hg_width,
                               TILES["in_proj_m"], TILES["in_proj_n"])

        o_hg = _hgrn(main, gates, hg_lb_logits, hg_onorm_w[l], batch=batch, seq=seq, heads=hg_heads,
                     layer=l, col_q=0, col_i=hg_width // blk, col_g=2 * hg_width // blk)

        lam_init = 0.8 - 0.6 * math.exp(-0.3 * l)
        lam_vecs = jnp.stack([lambda_q1[l], lambda_k1[l], lambda_q2[l], lambda_k2[l]]).astype(F32)
        base = 3 * hg_width // blk
        o_da = _diff_attn(main, lam_vecs, tiles, da_subln_w[l], batch=batch, seq=seq, heads=da_heads,
                          lam_init=lam_init, col_q=base, col_k=base + da_width // blk,
                          col_v=base + 2 * da_width // blk, tq=TILES["attn_q"])

        h, u2, w_down_bf16 = _out_proj(h, o_hg, o_da, w_out[l].astype(BF16), norm2_w[l], w_down, l,
                                       TILES["out_proj_m"])

        act = _ffn_up(u2, w_gate, w_up, l, TILES["ffn_up_m"], TILES["ffn_up_n"])
        last = l == depth - 1
        assert last, "final norm is fused into the last layer's down projection"
        h = _ffn_down(act, w_down_bf16, h, final_norm_w, TILES["ffn_down_m"])
    return h.reshape(batch, seq, d_model)
```

```python
import functools
import math

import numpy as np
import jax
import jax.numpy as jnp
from jax import lax
from jax.experimental import pallas as pl
from jax.experimental.pallas import tpu as pltpu

F32 = jnp.float32
BF16 = jnp.bfloat16

EPS = 1e-6
LOG2E = math.log2(math.e)
HG_HEAD_DIM = 128
DA_HEAD_DIM = 64
N_BUCKETS = 32
MAX_DISTANCE = 128

LANES = 128
SUBLANES = 8
HG_CHUNK = 64
HG_LEVELS = 6
HG_GROUP = 2
BIAS_TILE = 128
ATTN_SLOTS = 2
VMEM_LIMIT = 56 * 1024 * 1024

TILES = {
    "in_proj_m": 256, "in_proj_n": 1024,
    "attn_q": 256,
    "out_proj_m": 512,
    "ffn_up_m": 2048, "ffn_up_n": 512,
    "ffn_down_m": 512,
}


def _cparams(sem, flags=None):
    return pltpu.CompilerParams(dimension_semantics=sem, vmem_limit_bytes=VMEM_LIMIT, flags=flags)


def _dot(a, b):
    return jnp.dot(a, b, preferred_element_type=F32)


def _dot_nt(a, b):
    return lax.dot_general(a, b, (((1,), (1,)), ((), ())), preferred_element_type=F32)


def _dot_tn(a, b):
    return lax.dot_general(a, b, (((0,), (0,)), ((), ())), preferred_element_type=F32)


def _sigmoid(x):
    return 0.5 * jnp.tanh(0.5 * x) + 0.5


def _silu(x):
    h = 0.5 * x
    return h + h * jnp.tanh(h)


def _in_proj_kernel(x_ref, nw_ref, w_ref, main_ref, gates_ref, *, gate_lo, gate_hi, tn):
    x = x_ref[...].astype(F32)
    ms = jnp.mean(x * x, axis=-1, keepdims=True)
    u = (x * lax.rsqrt(ms + EPS) * nw_ref[...]).astype(BF16)

    n = w_ref.shape[1]
    slab = {id(main_ref): 0, id(gates_ref): 0}
    for c0 in range(0, n, tn):
        o_ref = gates_ref if gate_lo <= c0 < gate_hi else main_ref
        r = _dot(u, w_ref[:, c0:c0 + tn]).astype(o_ref.dtype)
        for c in range(tn // LANES):
            o_ref[slab[id(o_ref)]] = r[:, c * LANES:(c + 1) * LANES]
            slab[id(o_ref)] += 1


def _in_proj(x, nw, w, gate_lo, gate_hi, tm, tn):
    m, d = x.shape
    n = w.shape[1]
    assert gate_lo % tn == 0 and gate_hi % tn == 0 and n % tn == 0
    n_gate_cols = gate_hi - gate_lo
    kern = functools.partial(_in_proj_kernel, gate_lo=gate_lo, gate_hi=gate_hi, tn=tn)
    return pl.pallas_call(
        kern,
        grid=(m // tm,),
        in_specs=[
            pl.BlockSpec((tm, d), lambda i: (i, 0)),
            pl.BlockSpec((1, d), lambda i: (0, 0)),
            pl.BlockSpec((d, n), lambda i: (0, 0), pipeline_mode=pl.Buffered(1)),
        ],
        out_specs=[
            pl.BlockSpec(((n - n_gate_cols) // LANES, tm, LANES), lambda i: (0, i, 0)),
            pl.BlockSpec((n_gate_cols // LANES, tm, LANES), lambda i: (0, i, 0)),
        ],
        out_shape=[
            jax.ShapeDtypeStruct(((n - n_gate_cols) // LANES, m, LANES), BF16),
            jax.ShapeDtypeStruct((n_gate_cols // LANES, m, LANES), F32),
        ],
        compiler_params=_cparams(("parallel",)),
        name="in_proj",
    )(x, nw.reshape(1, d), w)


def _hgrn_constants():
    c, nl, g = HG_CHUNK, HG_LEVELS, HG_GROUP
    idx = np.arange(g * c)
    same_chunk = (idx[:, None] // c) == (idx[None, :] // c)
    tri = (same_chunk & (idx[:, None] >= idx[None, :])).astype(np.float32)
    masks = np.zeros((nl + 1, g * c, g * c), np.float32)
    masks[0] = np.eye(g * c, dtype=np.float32)
    for lvl in range(nl):
        m = 1 << lvl
        same_pair = (idx[:, None] // (2 * m)) == (idx[None, :] // (2 * m))
        other_sibling = (idx[:, None] // m) != (idx[None, :] // m)
        masks[lvl + 1] = (same_pair & other_sibling).astype(np.float32)
    return tri, masks


def _level_operands(lvl, j, q, f_f, f_b, k_f, k_b, beta_f, bx_b, odd, hi2, hi4):
    sl = slice(j * SUBLANES, (j + 1) * SUBLANES)
    qj, ffj, fbj, kfj, kbj = q[sl], f_f[sl], f_b[sl], k_f[sl], k_b[sl]
    if lvl == 0:
        return qj * jnp.where(odd, ffj, fbj), jnp.where(odd, kbj, kfj)
    if lvl == 1:
        up_f, dn_f = pltpu.roll(ffj, SUBLANES - 1, axis=0), pltpu.roll(ffj, 1, axis=0)
        up_b, dn_b = pltpu.roll(fbj, SUBLANES - 1, axis=0), pltpu.roll(fbj, 1, axis=0)
        fq = jnp.where(hi2, jnp.where(odd, ffj * dn_f, ffj), jnp.where(odd, fbj, fbj * up_b))
        ks = jnp.where(hi2, jnp.where(odd, kbj * dn_b, kbj), jnp.where(odd, kfj, kfj * up_f))
        return qj * fq, ks
    bfj, bbj = beta_f[sl], bx_b[sl]
    if lvl == 2:
        r = j * SUBLANES + SUBLANES // 2
        ef = jnp.exp2(-jnp.abs(bfj - beta_f[r - 1:r]))
        eb = jnp.exp2(-jnp.abs(bbj - bx_b[r:r + 1]))
        return qj * jnp.where(hi4, ef, eb), jnp.where(hi4, kbj * eb, kfj * ef)
    mb = (1 << lvl) // SUBLANES
    r = ((j // (2 * mb)) * 2 * mb + mb) * SUBLANES
    ref_f, ref_b = beta_f[r - 1:r], bx_b[r:r + 1]
    if (j // mb) & 1:
        return qj * jnp.exp2(bfj - ref_f), kbj * jnp.exp2(bbj - ref_b)
    return qj * jnp.exp2(ref_b - bbj), kfj * jnp.exp2(ref_f - bfj)


def _hgrn_kernel(lg_ref, hq_ref, hi_ref, hg_ref, gf_ref, gb_ref, ow_ref, tri_ref, mk_ref,
                 out_ref, o_scr, qi_scr, ut_scr, dec_scr, st_scr, g_scr, *, layer, n_slots):
    c, nl, dh = HG_CHUNK, HG_LEVELS, HG_HEAD_DIM
    seq = hq_ref.shape[0]
    nc = seq // c

    lg = lg_ref[...].astype(F32)
    lbs = []
    for d in range(2):
        rows = lg[d * n_slots:(d + 1) * n_slots]
        e = jnp.exp(rows - jnp.max(rows, axis=0, keepdims=True))
        lbs.append(jnp.sum(e[:layer + 1], axis=0, keepdims=True) / jnp.sum(e, axis=0, keepdims=True))
    lb_f, lb_b = lbs

    pos = lax.broadcasted_iota(jnp.int32, (SUBLANES, dh), 0)
    odd = (pos & 1) != 0
    hi2 = (pos & 2) != 0
    hi4 = (pos & 4) != 0

    def split(x):
        hi = x.astype(BF16)
        return hi, (x - hi.astype(F32)).astype(BF16)

    gc = HG_GROUP * c
    ng = nc // HG_GROUP

    def gates(gi, carry):
        r0 = gi * gc
        f_f = lb_f + (1.0 - lb_f) * _sigmoid(gf_ref[pl.ds(r0, gc), :])
        f_b = lb_b + (1.0 - lb_b) * _sigmoid(gb_ref[pl.ds(r0, gc), :])
        lf_f = jnp.log2(f_f)
        lf_b = jnp.log2(f_b)
        pre = _dot(tri_ref[...], jnp.concatenate(split(lf_f) + split(lf_b), axis=1))
        g_scr[0, pl.ds(r0, gc), :] = f_f
        g_scr[1, pl.ds(r0, gc), :] = f_b
        g_scr[2, pl.ds(r0, gc), :] = pre[:, :dh] + pre[:, dh:2 * dh]
        g_scr[3, pl.ds(r0, gc), :] = pre[:, 2 * dh:3 * dh] + pre[:, 3 * dh:] - lf_b
        return carry


    def per_chunk_rows(x, row):
        return jnp.concatenate([jnp.broadcast_to(x[g * c + row:g * c + row + 1], (c, dh))
                                for g in range(HG_GROUP)], axis=0)

    def intra(gi, carry):
        r0 = gi * gc
        q = _silu(hq_ref[pl.ds(r0, gc), :].astype(F32))
        v = hi_ref[pl.ds(r0, gc), :]
        f_f = g_scr[0, pl.ds(r0, gc), :]
        f_b = g_scr[1, pl.ds(r0, gc), :]
        beta_f = g_scr[2, pl.ds(r0, gc), :]
        bx_b = g_scr[3, pl.ds(r0, gc), :]
        k_f = 1.0 - f_f
        k_b = 1.0 - f_b
        tot_f = per_chunk_rows(beta_f, c - 1)
        tot_b = per_chunk_rows(bx_b + jnp.log2(f_b), c - 1)

        a = mk_ref[0] * _dot_nt(q.astype(BF16), (k_f + k_b).astype(BF16))
        for lvl in range(nl):
            ops = [_level_operands(lvl, j, q, f_f, f_b, k_f, k_b, beta_f, bx_b, odd, hi2, hi4)
                   for j in range(gc // SUBLANES)]
            qs = jnp.concatenate([o[0] for o in ops], axis=0)
            ks = jnp.concatenate([o[1] for o in ops], axis=0)
            a = a + mk_ref[lvl + 1] * _dot_nt(qs.astype(BF16), ks.astype(BF16))
        o_scr[pl.ds(r0, gc), :] = _dot(a.astype(BF16), v)

        qi = jnp.concatenate([q * jnp.exp2(beta_f), q * jnp.exp2(tot_b - bx_b)], axis=1)
        qi_scr[pl.ds(r0, gc), :] = qi.astype(BF16)
        ks = jnp.concatenate([k_f * jnp.exp2(tot_f - beta_f), k_b * jnp.exp2(bx_b)], axis=1).astype(BF16)
        zero = jnp.zeros((c, dh), BF16)
        vd = jnp.concatenate([jnp.concatenate([v[g * c:(g + 1) * c] if h == g else zero
                                               for h in range(HG_GROUP)], axis=1)
                              for g in range(HG_GROUP)], axis=0)
        ut = _dot_tn(vd, ks)
        dec = jnp.exp2(jnp.concatenate([tot_f, tot_b], axis=1))
        for g in range(HG_GROUP):
            ut_scr[gi * HG_GROUP + g] = ut[g * dh:(g + 1) * dh]
            dec_scr[gi * HG_GROUP + g] = dec[g * c:g * c + 8]
        return carry

    for gi in range(ng):
        gates(gi, 0)
        intra(gi, 0)

    def states(i, carry):
        st_f, st_b = carry
        cf = i
        cb = nc - 1 - i
        st_scr[cf, :, :dh] = st_f.astype(BF16)
        st_scr[cb, :, dh:] = st_b.astype(BF16)
        st_f = st_f * dec_scr[cf, 0:1, :dh] + ut_scr[cf, :, :dh]
        st_b = st_b * dec_scr[cb, 0:1, dh:] + ut_scr[cb, :, dh:]
        return st_f, st_b

    zero = jnp.zeros((dh, dh), F32)
    lax.fori_loop(0, nc, states, (zero, zero), unroll=True)

    ow = ow_ref[...].astype(F32)

    def finish(ci, carry):
        r0 = pl.multiple_of(ci * c, c)
        o = o_scr[pl.ds(r0, c), :] + _dot_nt(qi_scr[pl.ds(r0, c), :], st_scr[ci])
        y = o * lax.rsqrt(jnp.mean(o * o, axis=-1, keepdims=True) + EPS) * ow
        out_ref[pl.ds(r0, c), :] = (y * _silu(hg_ref[pl.ds(r0, c), :].astype(F32))).astype(out_ref.dtype)
        return carry

    lax.fori_loop(0, nc, finish, 0, unroll=True)


def _hgrn(main, gates, lb_logits, onorm_w, *, batch, seq, heads, layer, col_q, col_i, col_g):
    dh, c = HG_HEAD_DIM, HG_CHUNK
    n_slots = lb_logits.shape[1]
    lg = lb_logits.reshape(2 * n_slots, heads * dh)
    tri, masks = _hgrn_constants()
    nc = seq // c
    kern = functools.partial(_hgrn_kernel, layer=layer, n_slots=n_slots)
    return pl.pallas_call(
        kern,
        grid=(batch, heads),
        in_specs=[
            pl.BlockSpec((2 * n_slots, dh), lambda b, h: (0, h)),
            pl.BlockSpec((None, seq, dh), lambda b, h: (col_q + h, b, 0)),
            pl.BlockSpec((None, seq, dh), lambda b, h: (col_i + h, b, 0)),
            pl.BlockSpec((None, seq, dh), lambda b, h: (col_g + h, b, 0)),
            pl.BlockSpec((None, seq, dh), lambda b, h: (h, b, 0)),
            pl.BlockSpec((None, seq, dh), lambda b, h: (heads + h, b, 0)),
            pl.BlockSpec((1, dh), lambda b, h: (0, 0)),
            pl.BlockSpec(tri.shape, lambda b, h: (0, 0)),
            pl.BlockSpec(masks.shape, lambda b, h: (0, 0, 0)),
        ],
        out_specs=pl.BlockSpec((None, seq, dh), lambda b, h: (h, b, 0)),
        out_shape=jax.ShapeDtypeStruct((heads, batch * seq, dh), BF16),
        scratch_shapes=[
            pltpu.VMEM((seq, dh), F32),
            pltpu.VMEM((seq, 2 * dh), BF16),
            pltpu.VMEM((nc, dh, 2 * dh), F32),
            pltpu.VMEM((nc, 8, 2 * dh), F32),
            pltpu.VMEM((nc, dh, 2 * dh), BF16),
            pltpu.VMEM((4, seq, dh), F32),
        ],
        compiler_params=_cparams(("parallel", "parallel")),
        name="hgrn2",
    )(lg, main, main, main, gates, gates, onorm_w.reshape(1, dh),
      jnp.asarray(tri, BF16), jnp.asarray(masks, F32))


def _rel_bucket_index(rel):
    nb = N_BUCKETS // 2
    max_exact = nb // 2
    ret = jnp.where(rel > 0, nb, 0)
    n = jnp.abs(rel)
    nf = jnp.maximum(n, 1).astype(jnp.float32)
    large = max_exact + (jnp.log(nf / max_exact) / math.log(MAX_DISTANCE / max_exact)
                         * (nb - max_exact)).astype(jnp.int32)
    large = jnp.minimum(large, nb - 1)
    return ret + jnp.where(n < max_exact, n, large)


def _bias_tiles_kernel(tbl_ref, bkt_ref, out_ref):
    h = pl.program_id(0)
    bkt = bkt_ref[...]
    acc = jnp.zeros(bkt.shape, F32)
    for cidx in range(N_BUCKETS):
        acc = jnp.where(bkt == cidx, tbl_ref[cidx, h], acc)
    out_ref[0] = acc * LOG2E


def _bias_tiles(rel_bias):
    t = BIAS_TILE
    assert t >= MAX_DISTANCE
    heads = rel_bias.shape[1]
    key = jnp.arange(t, dtype=jnp.int32)[:, None]
    qry = jnp.arange(t, dtype=jnp.int32)[None, :]
    rel = jnp.stack([t * d + key - qry for d in range(-2, 3)])
    bkt = _rel_bucket_index(rel).astype(jnp.int32)
    return pl.pallas_call(
        _bias_tiles_kernel,
        grid=(heads,),
        in_specs=[
            pl.BlockSpec(memory_space=pltpu.SMEM),
            pl.BlockSpec((5, t, t), lambda h: (0, 0, 0)),
        ],
        out_specs=pl.BlockSpec((1, 5, t, t), lambda h: (h, 0, 0, 0)),
        out_shape=jax.ShapeDtypeStruct((heads, 5, t, t), F32),
        compiler_params=_cparams(("arbitrary",)),
        name="bias_tiles",
    )(rel_bias.astype(F32), bkt)


def _diff_attn_kernel(lam_ref, q_ref, k_ref, v_ref, tiles_ref, sw_ref, out_ref, vaug_scr, *slot_scr, lam_init, tq):
    seq, d2 = q_ref.shape
    t = BIAS_TILE
    ns = ATTN_SLOTS
    s_scr, m_scr, e_scr = slot_scr[:ns], slot_scr[ns:2 * ns], slot_scr[2 * ns:]

    vaug_scr[:d2, :] = v_ref[...].T
    vaug_scr[d2:, :] = jnp.ones((vaug_scr.shape[0] - d2, seq), BF16)

    lv = lam_ref[...].astype(F32)
    lam = (jnp.exp(jnp.sum(lv[0:1] * lv[1:2], axis=-1, keepdims=True))
           - jnp.exp(jnp.sum(lv[2:3] * lv[3:4], axis=-1, keepdims=True)) + lam_init)

    c = DA_HEAD_DIM ** -0.5 * LOG2E
    lane = lax.broadcasted_iota(jnp.int32, (tq, d2), 1)
    first = lane < DA_HEAD_DIM
    sw = sw_ref[...].astype(F32) * (1.0 - lam_init)

    def scores(n, slot):
        r0 = n * tq if isinstance(n, int) else pl.multiple_of(n * tq, tq)
        q = q_ref[pl.ds(r0, tq), :]
        zero = jnp.zeros_like(q)
        cols = []
        for ib in range(tq // t):
            d = [jnp.clip(jb - (n * (tq // t) + ib), -2, 2) + 2 for jb in range(seq // t)]
            cols.append(jnp.concatenate([tiles_ref[0, dj] for dj in d], axis=0))
        bias = jnp.concatenate(cols, axis=1)
        qq = jnp.concatenate([jnp.where(first, q, zero), jnp.where(first, zero, q)], axis=0)
        st = _dot_nt(k_ref[...], qq) * c
        st = jnp.concatenate([st[:, :tq] + bias, st[:, tq:] + bias], axis=1)
        s_scr[slot][...] = st
        m_scr[slot][...] = jnp.max(st, axis=0, keepdims=True)

    def numerators(slot):
        e_scr[slot][...] = jnp.exp2(s_scr[slot][...] - m_scr[slot][...]).astype(BF16)

    def values(n, slot):
        r0 = n * tq if isinstance(n, int) else pl.multiple_of(n * tq, tq)
        ot = _dot(vaug_scr[...], e_scr[slot][...])
        on = ot[:d2] / ot[d2:d2 + 1]
        o = on[:, :tq] - lam * on[:, tq:]
        yt = o * lax.rsqrt(jnp.mean(o * o, axis=0, keepdims=True) + EPS)
        out_ref[pl.ds(r0, tq), :] = (yt.T * sw).astype(out_ref.dtype)

    nq = seq // tq
    scores(0, 0)
    scores(1, 1 % ns)
    numerators(0)
    for j in range(nq):
        if j + 2 < nq:
            scores(j + 2, (j + 2) % ns)
        if j + 1 < nq:
            numerators((j + 1) % ns)
        values(j, j % ns)


def _diff_attn(main, lam_vecs, tiles, subln_w, *, batch, seq, heads, lam_init, col_q, col_k, col_v, tq):
    d2 = 2 * DA_HEAD_DIM
    t = BIAS_TILE
    kern = functools.partial(_diff_attn_kernel, lam_init=lam_init, tq=tq)
    return pl.pallas_call(
        kern,
        grid=(heads, batch),
        in_specs=[
            pl.BlockSpec(lam_vecs.shape, lambda h, b: (0, 0)),
            pl.BlockSpec((None, seq, d2), lambda h, b: (col_q + h, b, 0)),
            pl.BlockSpec((None, seq, d2), lambda h, b: (col_k + h, b, 0)),
            pl.BlockSpec((None, seq, d2), lambda h, b: (col_v + h, b, 0)),
            pl.BlockSpec((1, 5, t, t), lambda h, b: (h, 0, 0, 0)),
            pl.BlockSpec((1, d2), lambda h, b: (0, 0)),
        ],
        out_specs=pl.BlockSpec((None, seq, d2), lambda h, b: (h, b, 0)),
        out_shape=jax.ShapeDtypeStruct((heads, batch * seq, d2), BF16),
        scratch_shapes=([pltpu.VMEM((d2 + 2 * SUBLANES, seq), BF16)]
                        + [pltpu.VMEM((seq, 2 * tq), F32)] * ATTN_SLOTS
                        + [pltpu.VMEM((1, 2 * tq), F32)] * ATTN_SLOTS
                        + [pltpu.VMEM((seq, 2 * tq), BF16)] * ATTN_SLOTS),
        compiler_params=_cparams(("parallel", "parallel")),
        name="diff_attn",
    )(lam_vecs, main, main, main, tiles, subln_w.reshape(1, d2))


def _out_proj_kernel(x_ref, a_ref, b_ref, wa_ref, wb_ref, nw_ref, wd_ref, h_ref, u_ref, wd_out_ref):
    def heads_to_lanes(ref):
        return jnp.concatenate([ref[c] for c in range(ref.shape[0])], axis=1)

    h = x_ref[...] + _dot(heads_to_lanes(a_ref), wa_ref[...]) + _dot(heads_to_lanes(b_ref), wb_ref[...])
    h_ref[...] = h
    ms = jnp.mean(h * h, axis=-1, keepdims=True)
    u_ref[...] = (h * lax.rsqrt(ms + EPS) * nw_ref[...]).astype(u_ref.dtype)
    wd_out_ref[...] = wd_ref[...].astype(wd_out_ref.dtype)


def _out_proj(x, a, b, w, nw, wd, layer, tm):
    m, d = x.shape
    ka, kb = a.shape[0] * a.shape[2], b.shape[0] * b.shape[2]
    assert ka == kb and w.shape == (ka + kb, d)
    steps = m // tm
    f = wd.shape[1]
    rows = f // steps
    assert rows * steps == f and rows % (2 * SUBLANES) == 0
    resident = pl.Buffered(1)
    return pl.pallas_call(
        _out_proj_kernel,
        grid=(steps,),
        in_specs=[
            pl.BlockSpec((tm, d), lambda i: (i, 0)),
            pl.BlockSpec((a.shape[0], tm, a.shape[2]), lambda i: (0, i, 0)),
            pl.BlockSpec((b.shape[0], tm, b.shape[2]), lambda i: (0, i, 0)),
            pl.BlockSpec((ka, d), lambda i: (0, 0), pipeline_mode=resident),
            pl.BlockSpec((kb, d), lambda i: (1, 0), pipeline_mode=resident),
            pl.BlockSpec((1, d), lambda i: (0, 0)),
            pl.BlockSpec((None, rows, wd.shape[2]), lambda i: (layer, i, 0)),
        ],
        out_specs=[pl.BlockSpec((tm, d), lambda i: (i, 0)), pl.BlockSpec((tm, d), lambda i: (i, 0)),
                   pl.BlockSpec((rows, wd.shape[2]), lambda i: (i, 0))],
        out_shape=[jax.ShapeDtypeStruct((m, d), F32), jax.ShapeDtypeStruct((m, d), BF16),
                   jax.ShapeDtypeStruct(wd.shape[1:], BF16)],
        compiler_params=_cparams(("parallel",)),
        name="out_proj",
    )(x, a, b, w, w, nw.reshape(1, d), wd)


def _ffn_up_kernel(u_ref, wg_ref, wu_ref, o_ref, wg_scr, wu_scr):
    @pl.when(pl.program_id(1) == 0)
    def _():
        wg_scr[...] = wg_ref[...].astype(BF16)
        wu_scr[...] = wu_ref[...].astype(BF16)

    sub = min(512, u_ref.shape[0])
    for r in range(u_ref.shape[0] // sub):
        rows = slice(r * sub, (r + 1) * sub)
        u = u_ref[rows, :]
        g = _dot(u, wg_scr[...])
        up = _dot(u, wu_scr[...])
        o_ref[rows, :] = (_silu(g) * up).astype(o_ref.dtype)


def _ffn_up(u, wg, wu, layer, tm, tn):
    m, d = u.shape
    n = wg.shape[2]
    return pl.pallas_call(
        _ffn_up_kernel,
        grid=(n // tn, m // tm),
        in_specs=[
            pl.BlockSpec((tm, d), lambda j, i: (i, 0)),
            pl.BlockSpec((None, d, tn), lambda j, i: (layer, 0, j)),
            pl.BlockSpec((None, d, tn), lambda j, i: (layer, 0, j)),
        ],
        out_specs=pl.BlockSpec((tm, tn), lambda j, i: (i, j)),
        out_shape=jax.ShapeDtypeStruct((m, n), BF16),
        scratch_shapes=[pltpu.VMEM((d, tn), BF16), pltpu.VMEM((d, tn), BF16)],
        compiler_params=_cparams(("parallel", "arbitrary")),
        name="ffn_up",
    )(u, wg, wu)


def _ffn_down_kernel(a_ref, w_ref, h_ref, fw_ref, o_ref):
    y = h_ref[...] + _dot(a_ref[...], w_ref[...])
    ms = jnp.mean(y * y, axis=-1, keepdims=True)
    o_ref[...] = y * lax.rsqrt(ms + EPS) * fw_ref[...]


def _ffn_down(a, w, h, fw, tm):
    m, f = a.shape
    d = w.shape[1]
    return pl.pallas_call(
        _ffn_down_kernel,
        grid=(m // tm,),
        in_specs=[
            pl.BlockSpec((tm, f), lambda i: (i, 0)),
            pl.BlockSpec((f, d), lambda i: (0, 0), pipeline_mode=pl.Buffered(1)),
            pl.BlockSpec((tm, d), lambda i: (i, 0)),
            pl.BlockSpec((1, d), lambda i: (0, 0)),
        ],
        out_specs=pl.BlockSpec((tm, d), lambda i: (i, 0)),
        out_shape=jax.ShapeDtypeStruct((m, d), F32),
        compiler_params=_cparams(("parallel",)),
        name="ffn_down",
    )(a, w, h, fw.reshape(1, d))


def kernel(x, norm1_w, w_in, hg_lb_logits, hg_onorm_w, lambda_q1, lambda_k1, lambda_q2, lambda_k2,
           da_subln_w, rel_bias, w_out, norm2_w, w_gate, w_up, w_down, final_norm_w):
    batch, seq, d_model = x.shape
    depth = w_in.shape[0]
    hg_width = hg_lb_logits.shape[-1]
    da_width = d_model - hg_width
    hg_heads = hg_width // HG_HEAD_DIM
    da_heads = da_width // (2 * DA_HEAD_DIM)
    assert w_in.shape[2] == 5 * hg_width + 3 * da_width
    assert seq % HG_CHUNK == 0 and seq % BIAS_TILE == 0
    m = batch * seq
    blk = LANES

    tiles = _bias_tiles(rel_bias)
    h = x.reshape(m, d_model)
    for l in range(depth):
        main, gates = _in_proj(h, norm1_w[l], w_in[l].astype(BF16), 2 * hg_width, 4 * hg_width,
                               TILES["in_proj_m"], TILES["in_proj_n"])

        o_hg = _hgrn(main, gates, hg_lb_logits, hg_onorm_w[l], batch=batch, seq=seq, heads=hg_heads,
                     layer=l, col_q=0, col_i=hg_width // blk, col_g=2 * hg_width // blk)

        lam_init = 0.8 - 0.6 * math.exp(-0.3 * l)
        lam_vecs = jnp.stack([lambda_q1[l], lambda_k1[l], lambda_q2[l], lambda_k2[l]]).astype(F32)
        base = 3 * hg_width // blk
        o_da = _diff_attn(main, lam_vecs, tiles, da_subln_w[l], batch=batch, seq=seq, heads=da_heads,
                          lam_init=lam_init, col_q=base, col_k=base + da_width // blk,
                          col_v=base + 2 * da_width // blk, tq=TILES["attn_q"])

        h, u2, w_down_bf16 = _out_proj(h, o_hg, o_da, w_out[l].astype(BF16), norm2_w[l], w_down, l,
                                       TILES["out_proj_m"])

        act = _ffn_up(u2, w_gate, w_up, l, TILES["ffn_up_m"], TILES["ffn_up_n"])
        last = l == depth - 1
        assert last, "final norm is fused into the last layer's down projection"
        h = _ffn_down(act, w_down_bf16, h, final_norm_w, TILES["ffn_down_m"])
    return h.reshape(batch, seq, d_model)
```

```python
import functools
import math

import numpy as np
import jax
import jax.numpy as jnp
from jax import lax
from jax.experimental import pallas as pl
from jax.experimental.pallas import tpu as pltpu

F32 = jnp.float32
BF16 = jnp.bfloat16

EPS = 1e-6
LOG2E = math.log2(math.e)
HG_HEAD_DIM = 128
DA_HEAD_DIM = 64
N_BUCKETS = 32
MAX_DISTANCE = 128

LANES = 128
SUBLANES = 8
HG_CHUNK = 64
HG_LEVELS = 6
HG_GROUP = 2
BIAS_TILE = 128
ATTN_SLOTS = 2
VMEM_LIMIT = 56 * 1024 * 1024

TILES = {
    "in_proj_m": 256, "in_proj_n": 1024,
    "attn_q": 256,
    "out_proj_m": 512,
    "ffn_up_m": 2048, "ffn_up_n": 512,
    "ffn_down_m": 512,
}


def _cparams(sem, flags=None):
    return pltpu.CompilerParams(dimension_semantics=sem, vmem_limit_bytes=VMEM_LIMIT, flags=flags)


def _dot(a, b):
    return jnp.dot(a, b, preferred_element_type=F32)


def _dot_nt(a, b):
    return lax.dot_general(a, b, (((1,), (1,)), ((), ())), preferred_element_type=F32)


def _dot_tn(a, b):
    return lax.dot_general(a, b, (((0,), (0,)), ((), ())), preferred_element_type=F32)


def _sigmoid(x):
    return 0.5 * jnp.tanh(0.5 * x) + 0.5


def _silu(x):
    h = 0.5 * x
    return h + h * jnp.tanh(h)


def _in_proj_kernel(x_ref, nw_ref, w_ref, main_ref, gates_ref, *, gate_lo, gate_hi, tn):
    x = x_ref[...].astype(F32)
    ms = jnp.mean(x * x, axis=-1, keepdims=True)
    u = (x * lax.rsqrt(ms + EPS) * nw_ref[...]).astype(BF16)

    n = w_ref.shape[1]
    slab = {id(main_ref): 0, id(gates_ref): 0}
    for c0 in range(0, n, tn):
        o_ref = gates_ref if gate_lo <= c0 < gate_hi else main_ref
        r = _dot(u, w_ref[:, c0:c0 + tn]).astype(o_ref.dtype)
        for c in range(tn // LANES):
            o_ref[slab[id(o_ref)]] = r[:, c * LANES:(c + 1) * LANES]
            slab[id(o_ref)] += 1


def _in_proj(x, nw, w, gate_lo, gate_hi, tm, tn):
    m, d = x.shape
    n = w.shape[1]
    assert gate_lo % tn == 0 and gate_hi % tn == 0 and n % tn == 0
    n_gate_cols = gate_hi - gate_lo
    kern = functools.partial(_in_proj_kernel, gate_lo=gate_lo, gate_hi=gate_hi, tn=tn)
    return pl.pallas_call(
        kern,
        grid=(m // tm,),
        in_specs=[
            pl.BlockSpec((tm, d), lambda i: (i, 0)),
            pl.BlockSpec((1, d), lambda i: (0, 0)),
            pl.BlockSpec((d, n), lambda i: (0, 0), pipeline_mode=pl.Buffered(1)),
        ],
        out_specs=[
            pl.BlockSpec(((n - n_gate_cols) // LANES, tm, LANES), lambda i: (0, i, 0)),
            pl.BlockSpec((n_gate_cols // LANES, tm, LANES), lambda i: (0, i, 0)),
        ],
        out_shape=[
            jax.ShapeDtypeStruct(((n - n_gate_cols) // LANES, m, LANES), BF16),
            jax.ShapeDtypeStruct((n_gate_cols // LANES, m, LANES), F32),
        ],
        compiler_params=_cparams(("parallel",)),
        name="in_proj",
    )(x, nw.reshape(1, d), w)


def _hgrn_constants():
    c, nl, g = HG_CHUNK, HG_LEVELS, HG_GROUP
    idx = np.arange(g * c)
    same_chunk = (idx[:, None] // c) == (idx[None, :] // c)
    tri = (same_chunk & (idx[:, None] >= idx[None, :])).astype(np.float32)
    masks = np.zeros((nl + 1, g * c, g * c), np.float32)
    masks[0] = np.eye(g * c, dtype=np.float32)
    for lvl in range(nl):
        m = 1 << lvl
        same_pair = (idx[:, None] // (2 * m)) == (idx[None, :] // (2 * m))
        other_sibling = (idx[:, None] // m) != (idx[None, :] // m)
        masks[lvl + 1] = (same_pair & other_sibling).astype(np.float32)
    return tri, masks


def _level_operands(lvl, j, q, f_f, f_b, k_f, k_b, beta_f, bx_b, odd, hi2, hi4):
    sl = slice(j * SUBLANES, (j + 1) * SUBLANES)
    qj, ffj, fbj, kfj, kbj = q[sl], f_f[sl], f_b[sl], k_f[sl], k_b[sl]
    if lvl == 0:
        return qj * jnp.where(odd, ffj, fbj), jnp.where(odd, kbj, kfj)
    if lvl == 1:
        up_f, dn_f = pltpu.roll(ffj, SUBLANES - 1, axis=0), pltpu.roll(ffj, 1, axis=0)
        up_b, dn_b = pltpu.roll(fbj, SUBLANES - 1, axis=0), pltpu.roll(fbj, 1, axis=0)
        fq = jnp.where(hi2, jnp.where(odd, ffj * dn_f, ffj), jnp.where(odd, fbj, fbj * up_b))
        ks = jnp.where(hi2, jnp.where(odd, kbj * dn_b, kbj), jnp.where(odd, kfj, kfj * up_f))
        return qj * fq, ks
    bfj, bbj = beta_f[sl], bx_b[sl]
    if lvl == 2:
        r = j * SUBLANES + SUBLANES // 2
        ef = jnp.exp2(-jnp.abs(bfj - beta_f[r - 1:r]))
        eb = jnp.exp2(-jnp.abs(bbj - bx_b[r:r + 1]))
        return qj * jnp.where(hi4, ef, eb), jnp.where(hi4, kbj * eb, kfj * ef)
    mb = (1 << lvl) // SUBLANES
    r = ((j // (2 * mb)) * 2 * mb + mb) * SUBLANES
    ref_f, ref_b = beta_f[r - 1:r], bx_b[r:r + 1]
    if (j // mb) & 1:
        return qj * jnp.exp2(bfj - ref_f), kbj * jnp.exp2(bbj - ref_b)
    return qj * jnp.exp2(ref_b - bbj), kfj * jnp.exp2(ref_f - bfj)


def _hgrn_kernel(lg_ref, hq_ref, hi_ref, hg_ref, gf_ref, gb_ref, ow_ref, tri_ref, mk_ref,
                 out_ref, o_scr, qi_scr, ut_scr, dec_scr, st_scr, g_scr, *, layer, n_slots):
    c, nl, dh = HG_CHUNK, HG_LEVELS, HG_HEAD_DIM
    seq = hq_ref.shape[0]
    nc = seq // c

    lg = lg_ref[...].astype(F32)
    lbs = []
    for d in range(2):
        rows = lg[d * n_slots:(d + 1) * n_slots]
        e = jnp.exp(rows - jnp.max(rows, axis=0, keepdims=True))
        lbs.append(jnp.sum(e[:layer + 1], axis=0, keepdims=True) / jnp.sum(e, axis=0, keepdims=True))
    lb_f, lb_b = lbs

    pos = lax.broadcasted_iota(jnp.int32, (SUBLANES, dh), 0)
    odd = (pos & 1) != 0
    hi2 = (pos & 2) != 0
    hi4 = (pos & 4) != 0

    def split(x):
        hi = x.astype(BF16)
        return hi, (x - hi.astype(F32)).astype(BF16)

    gc = HG_GROUP * c
    ng = nc // HG_GROUP

    def gates(gi, carry):
        r0 = gi * gc
        f_f = lb_f + (1.0 - lb_f) * _sigmoid(gf_ref[pl.ds(r0, gc), :])
        f_b = lb_b + (1.0 - lb_b) * _sigmoid(gb_ref[pl.ds(r0, gc), :])
        lf_f = jnp.log2(f_f)
        lf_b = jnp.log2(f_b)
        pre = _dot(tri_ref[...], jnp.concatenate(split(lf_f) + split(lf_b), axis=1))
        g_scr[0, pl.ds(r0, gc), :] = f_f
        g_scr[1, pl.ds(r0, gc), :] = f_b
        g_scr[2, pl.ds(r0, gc), :] = pre[:, :dh] + pre[:, dh:2 * dh]
        g_scr[3, pl.ds(r0, gc), :] = pre[:, 2 * dh:3 * dh] + pre[:, 3 * dh:] - lf_b
        return carry


    def per_chunk_rows(x, row):
        return jnp.concatenate([jnp.broadcast_to(x[g * c + row:g * c + row + 1], (c, dh))
                                for g in range(HG_GROUP)], axis=0)

    def intra(gi, carry):
        r0 = gi * gc
        q = _silu(hq_ref[pl.ds(r0, gc), :].astype(F32))
        v = hi_ref[pl.ds(r0, gc), :]
        f_f = g_scr[0, pl.ds(r0, gc), :]
        f_b = g_scr[1, pl.ds(r0, gc), :]
        beta_f = g_scr[2, pl.ds(r0, gc), :]
        bx_b = g_scr[3, pl.ds(r0, gc), :]
        k_f = 1.0 - f_f
        k_b = 1.0 - f_b
        tot_f = per_chunk_rows(beta_f, c - 1)
        tot_b = per_chunk_rows(bx_b + jnp.log2(f_b), c - 1)

        a = mk_ref[0] * _dot_nt(q.astype(BF16), (k_f + k_b).astype(BF16))
        for lvl in range(nl):
            ops = [_level_operands(lvl, j, q, f_f, f_b, k_f, k_b, beta_f, bx_b, odd, hi2, hi4)
                   for j in range(gc // SUBLANES)]
            qs = jnp.concatenate([o[0] for o in ops], axis=0)
            ks = jnp.concatenate([o[1] for o in ops], axis=0)
            a = a + mk_ref[lvl + 1] * _dot_nt(qs.astype(BF16), ks.astype(BF16))
        o_scr[pl.ds(r0, gc), :] = _dot(a.astype(BF16), v)

        qi = jnp.concatenate([q * jnp.exp2(beta_f), q * jnp.exp2(tot_b - bx_b)], axis=1)
        qi_scr[pl.ds(r0, gc), :] = qi.astype(BF16)
        ks = jnp.concatenate([k_f * jnp.exp2(tot_f - beta_f), k_b * jnp.exp2(bx_b)], axis=1).astype(BF16)
        zero = jnp.zeros((c, dh), BF16)
        vd = jnp.concatenate([jnp.concatenate([v[g * c:(g + 1) * c] if h == g else zero
                                               for h in range(HG_GROUP)], axis=1)
                              for g in range(HG_GROUP)], axis=0)
        ut = _dot_tn(vd, ks)
        dec = jnp.exp2(jnp.concatenate([tot_f, tot_b], axis=1))
        for g in range(HG_GROUP):
            ut_scr[gi * HG_GROUP + g] = ut[g * dh:(g + 1) * dh]
            dec_scr[gi * HG_GROUP + g] = dec[g * c:g * c + 8]
        return carry

    for gi in range(ng):
        gates(gi, 0)
        intra(gi, 0)

    def states(i, carry):
        st_f, st_b = carry
        cf = i
        cb = nc - 1 - i
        st_scr[cf, :, :dh] = st_f.astype(BF16)
        st_scr[cb, :, dh:] = st_b.astype(BF16)
        st_f = st_f * dec_scr[cf, 0:1, :dh] + ut_scr[cf, :, :dh]
        st_b = st_b * dec_scr[cb, 0:1, dh:] + ut_scr[cb, :, dh:]
        return st_f, st_b

    carry = (jnp.zeros((dh, dh), F32),) * 2
    for i in range(nc):
        carry = states(i, carry)

    ow = ow_ref[...].astype(F32)

    def finish(ci):
        r0 = ci * c
        o = o_scr[pl.ds(r0, c), :] + _dot_nt(qi_scr[pl.ds(r0, c), :], st_scr[ci])
        y = o * lax.rsqrt(jnp.mean(o * o, axis=-1, keepdims=True) + EPS) * ow
        out_ref[pl.ds(r0, c), :] = (y * _silu(hg_ref[pl.ds(r0, c), :].astype(F32))).astype(out_ref.dtype)

    for ci in range(nc):
        finish(ci)


def _hgrn(main, gates, lb_logits, onorm_w, *, batch, seq, heads, layer, col_q, col_i, col_g):
    dh, c = HG_HEAD_DIM, HG_CHUNK
    n_slots = lb_logits.shape[1]
    lg = lb_logits.reshape(2 * n_slots, heads * dh)
    tri, masks = _hgrn_constants()
    nc = seq // c
    kern = functools.partial(_hgrn_kernel, layer=layer, n_slots=n_slots)
    return pl.pallas_call(
        kern,
        grid=(batch, heads),
        in_specs=[
            pl.BlockSpec((2 * n_slots, dh), lambda b, h: (0, h)),
            pl.BlockSpec((None, seq, dh), lambda b, h: (col_q + h, b, 0)),
            pl.BlockSpec((None, seq, dh), lambda b, h: (col_i + h, b, 0)),
            pl.BlockSpec((None, seq, dh), lambda b, h: (col_g + h, b, 0)),
            pl.BlockSpec((None, seq, dh), lambda b, h: (h, b, 0)),
            pl.BlockSpec((None, seq, dh), lambda b, h: (heads + h, b, 0)),
            pl.BlockSpec((1, dh), lambda b, h: (0, 0)),
            pl.BlockSpec(tri.shape, lambda b, h: (0, 0)),
            pl.BlockSpec(masks.shape, lambda b, h: (0, 0, 0)),
        ],
        out_specs=pl.BlockSpec((None, seq, dh), lambda b, h: (h, b, 0)),
        out_shape=jax.ShapeDtypeStruct((heads, batch * seq, dh), BF16),
        scratch_shapes=[
            pltpu.VMEM((seq, dh), F32),
            pltpu.VMEM((seq, 2 * dh), BF16),
            pltpu.VMEM((nc, dh, 2 * dh), F32),
            pltpu.VMEM((nc, 8, 2 * dh), F32),
            pltpu.VMEM((nc, dh, 2 * dh), BF16),
            pltpu.VMEM((4, seq, dh), F32),
        ],
        compiler_params=_cparams(("parallel", "parallel")),
        name="hgrn2",
    )(lg, main, main, main, gates, gates, onorm_w.reshape(1, dh),
      jnp.asarray(tri, BF16), jnp.asarray(masks, F32))


def _rel_bucket_index(rel):
    nb = N_BUCKETS // 2
    max_exact = nb // 2
    ret = jnp.where(rel > 0, nb, 0)
    n = jnp.abs(rel)
    nf = jnp.maximum(n, 1).astype(jnp.float32)
    large = max_exact + (jnp.log(nf / max_exact) / math.log(MAX_DISTANCE / max_exact)
                         * (nb - max_exact)).astype(jnp.int32)
    large = jnp.minimum(large, nb - 1)
    return ret + jnp.where(n < max_exact, n, large)


def _bias_tiles_kernel(tbl_ref, bkt_ref, out_ref):
    h = pl.program_id(0)
    bkt = bkt_ref[...]
    acc = jnp.zeros(bkt.shape, F32)
    for cidx in range(N_BUCKETS):
        acc = jnp.where(bkt == cidx, tbl_ref[cidx, h], acc)
    out_ref[0] = acc * LOG2E


def _bias_tiles(rel_bias):
    t = BIAS_TILE
    assert t >= MAX_DISTANCE
    heads = rel_bias.shape[1]
    key = jnp.arange(t, dtype=jnp.int32)[:, None]
    qry = jnp.arange(t, dtype=jnp.int32)[None, :]
    rel = jnp.stack([t * d + key - qry for d in range(-2, 3)])
    bkt = _rel_bucket_index(rel).astype(jnp.int32)
    return pl.pallas_call(
        _bias_tiles_kernel,
        grid=(heads,),
        in_specs=[
            pl.BlockSpec(memory_space=pltpu.SMEM),
            pl.BlockSpec((5, t, t), lambda h: (0, 0, 0)),
        ],
        out_specs=pl.BlockSpec((1, 5, t, t), lambda h: (h, 0, 0, 0)),
        out_shape=jax.ShapeDtypeStruct((heads, 5, t, t), F32),
        compiler_params=_cparams(("arbitrary",)),
        name="bias_tiles",
    )(rel_bias.astype(F32), bkt)


def _diff_attn_kernel(lam_ref, q_ref, k_ref, v_ref, tiles_ref, sw_ref, out_ref, vaug_scr, *slot_scr, lam_init, tq):
    seq, d2 = q_ref.shape
    t = BIAS_TILE
    ns = ATTN_SLOTS
    s_scr, m_scr, e_scr = slot_scr[:ns], slot_scr[ns:2 * ns], slot_scr[2 * ns:]

    vaug_scr[:d2, :] = v_ref[...].T
    vaug_scr[d2:, :] = jnp.ones((vaug_scr.shape[0] - d2, seq), BF16)

    lv = lam_ref[...].astype(F32)
    lam = (jnp.exp(jnp.sum(lv[0:1] * lv[1:2], axis=-1, keepdims=True))
           - jnp.exp(jnp.sum(lv[2:3] * lv[3:4], axis=-1, keepdims=True)) + lam_init)

    c = DA_HEAD_DIM ** -0.5 * LOG2E
    lane = lax.broadcasted_iota(jnp.int32, (tq, d2), 1)
    first = lane < DA_HEAD_DIM
    sw = sw_ref[...].astype(F32) * (1.0 - lam_init)

    def scores(n, slot):
        r0 = n * tq
        q = q_ref[pl.ds(r0, tq), :]
        zero = jnp.zeros_like(q)
        cols = []
        for ib in range(tq // t):
            d = [min(max(jb - (n * (tq // t) + ib), -2), 2) + 2 for jb in range(seq // t)]
            cols.append(jnp.concatenate([tiles_ref[0, dj] for dj in d], axis=0))
        bias = jnp.concatenate(cols, axis=1)
        qq = jnp.concatenate([jnp.where(first, q, zero), jnp.where(first, zero, q)], axis=0)
        st = _dot_nt(k_ref[...], qq) * c
        st = jnp.concatenate([st[:, :tq] + bias, st[:, tq:] + bias], axis=1)
        s_scr[slot][...] = st
        m_scr[slot][...] = jnp.max(st, axis=0, keepdims=True)

    def numerators(slot):
        e_scr[slot][...] = jnp.exp2(s_scr[slot][...] - m_scr[slot][...]).astype(BF16)

    def values(n, slot):
        r0 = n * tq
        ot = _dot(vaug_scr[...], e_scr[slot][...])
        on = ot[:d2] / ot[d2:d2 + 1]
        o = on[:, :tq] - lam * on[:, tq:]
        yt = o * lax.rsqrt(jnp.mean(o * o, axis=0, keepdims=True) + EPS)
        out_ref[pl.ds(r0, tq), :] = (yt.T * sw).astype(out_ref.dtype)

    nq = seq // tq
    scores(0, 0)
    scores(1, 1 % ns)
    numerators(0)
    for j in range(nq):
        if j + 2 < nq:
            scores(j + 2, (j + 2) % ns)
        if j + 1 < nq:
            numerators((j + 1) % ns)
        values(j, j % ns)


def _diff_attn(main, lam_vecs, tiles, subln_w, *, batch, seq, heads, lam_init, col_q, col_k, col_v, tq):
    d2 = 2 * DA_HEAD_DIM
    t = BIAS_TILE
    kern = functools.partial(_diff_attn_kernel, lam_init=lam_init, tq=tq)
    return pl.pallas_call(
        kern,
        grid=(heads, batch),
        in_specs=[
            pl.BlockSpec(lam_vecs.shape, lambda h, b: (0, 0)),
            pl.BlockSpec((None, seq, d2), lambda h, b: (col_q + h, b, 0)),
            pl.BlockSpec((None, seq, d2), lambda h, b: (col_k + h, b, 0)),
            pl.BlockSpec((None, seq, d2), lambda h, b: (col_v + h, b, 0)),
            pl.BlockSpec((1, 5, t, t), lambda h, b: (h, 0, 0, 0)),
            pl.BlockSpec((1, d2), lambda h, b: (0, 0)),
        ],
        out_specs=pl.BlockSpec((None, seq, d2), lambda h, b: (h, b, 0)),
        out_shape=jax.ShapeDtypeStruct((heads, batch * seq, d2), BF16),
        scratch_shapes=([pltpu.VMEM((d2 + 2 * SUBLANES, seq), BF16)]
                        + [pltpu.VMEM((seq, 2 * tq), F32)] * ATTN_SLOTS
                        + [pltpu.VMEM((1, 2 * tq), F32)] * ATTN_SLOTS
                        + [pltpu.VMEM((seq, 2 * tq), BF16)] * ATTN_SLOTS),
        compiler_params=_cparams(("parallel", "parallel")),
        name="diff_attn",
    )(lam_vecs, main, main, main, tiles, subln_w.reshape(1, d2))


def _out_proj_kernel(x_ref, a_ref, b_ref, wa_ref, wb_ref, nw_ref, wd_ref, h_ref, u_ref, wd_out_ref):
    def heads_to_lanes(ref):
        return jnp.concatenate([ref[c] for c in range(ref.shape[0])], axis=1)

    h = x_ref[...] + _dot(heads_to_lanes(a_ref), wa_ref[...]) + _dot(heads_to_lanes(b_ref), wb_ref[...])
    h_ref[...] = h
    ms = jnp.mean(h * h, axis=-1, keepdims=True)
    u_ref[...] = (h * lax.rsqrt(ms + EPS) * nw_ref[...]).astype(u_ref.dtype)
    wd_out_ref[...] = wd_ref[...].astype(wd_out_ref.dtype)


def _out_proj(x, a, b, w, nw, wd, layer, tm):
    m, d = x.shape
    ka, kb = a.shape[0] * a.shape[2], b.shape[0] * b.shape[2]
    assert ka == kb and w.shape == (ka + kb, d)
    steps = m // tm
    f = wd.shape[1]
    rows = f // steps
    assert rows * steps == f and rows % (2 * SUBLANES) == 0
    resident = pl.Buffered(1)
    return pl.pallas_call(
        _out_proj_kernel,
        grid=(steps,),
        in_specs=[
            pl.BlockSpec((tm, d), lambda i: (i, 0)),
            pl.BlockSpec((a.shape[0], tm, a.shape[2]), lambda i: (0, i, 0)),
            pl.BlockSpec((b.shape[0], tm, b.shape[2]), lambda i: (0, i, 0)),
            pl.BlockSpec((ka, d), lambda i: (0, 0), pipeline_mode=resident),
            pl.BlockSpec((kb, d), lambda i: (1, 0), pipeline_mode=resident),
            pl.BlockSpec((1, d), lambda i: (0, 0)),
            pl.BlockSpec((None, rows, wd.shape[2]), lambda i: (layer, i, 0)),
        ],
        out_specs=[pl.BlockSpec((tm, d), lambda i: (i, 0)), pl.BlockSpec((tm, d), lambda i: (i, 0)),
                   pl.BlockSpec((rows, wd.shape[2]), lambda i: (i, 0))],
        out_shape=[jax.ShapeDtypeStruct((m, d), F32), jax.ShapeDtypeStruct((m, d), BF16),
                   jax.ShapeDtypeStruct(wd.shape[1:], BF16)],
        compiler_params=_cparams(("parallel",)),
        name="out_proj",
    )(x, a, b, w, w, nw.reshape(1, d), wd)


def _ffn_up_kernel(u_ref, wg_ref, wu_ref, o_ref, wg_scr, wu_scr):
    @pl.when(pl.program_id(1) == 0)
    def _():
        wg_scr[...] = wg_ref[...].astype(BF16)
        wu_scr[...] = wu_ref[...].astype(BF16)

    sub = min(512, u_ref.shape[0])
    for r in range(u_ref.shape[0] // sub):
        rows = slice(r * sub, (r + 1) * sub)
        u = u_ref[rows, :]
        g = _dot(u, wg_scr[...])
        up = _dot(u, wu_scr[...])
        o_ref[rows, :] = (_silu(g) * up).astype(o_ref.dtype)


def _ffn_up(u, wg, wu, layer, tm, tn):
    m, d = u.shape
    n = wg.shape[2]
    return pl.pallas_call(
        _ffn_up_kernel,
        grid=(n // tn, m // tm),
        in_specs=[
            pl.BlockSpec((tm, d), lambda j, i: (i, 0)),
            pl.BlockSpec((None, d, tn), lambda j, i: (layer, 0, j)),
            pl.BlockSpec((None, d, tn), lambda j, i: (layer, 0, j)),
        ],
        out_specs=pl.BlockSpec((tm, tn), lambda j, i: (i, j)),
        out_shape=jax.ShapeDtypeStruct((m, n), BF16),
        scratch_shapes=[pltpu.VMEM((d, tn), BF16), pltpu.VMEM((d, tn), BF16)],
        compiler_params=_cparams(("parallel", "arbitrary")),
        name="ffn_up",
    )(u, wg, wu)


def _ffn_down_kernel(a_ref, w_ref, h_ref, fw_ref, o_ref):
    y = h_ref[...] + _dot(a_ref[...], w_ref[...])
    ms = jnp.mean(y * y, axis=-1, keepdims=True)
    o_ref[...] = y * lax.rsqrt(ms + EPS) * fw_ref[...]


def _ffn_down(a, w, h, fw, tm):
    m, f = a.shape
    d = w.shape[1]
    return pl.pallas_call(
        _ffn_down_kernel,
        grid=(m // tm,),
        in_specs=[
            pl.BlockSpec((tm, f), lambda i: (i, 0)),
            pl.BlockSpec((f, d), lambda i: (0, 0), pipeline_mode=pl.Buffered(1)),
            pl.BlockSpec((tm, d), lambda i: (i, 0)),
            pl.BlockSpec((1, d), lambda i: (0, 0)),
        ],
        out_specs=pl.BlockSpec((tm, d), lambda i: (i, 0)),
        out_shape=jax.ShapeDtypeStruct((m, d), F32),
        compiler_params=_cparams(("parallel",)),
        name="ffn_down",
    )(a, w, h, fw.reshape(1, d))


def kernel(x, norm1_w, w_in, hg_lb_logits, hg_onorm_w, lambda_q1, lambda_k1, lambda_q2, lambda_k2,
           da_subln_w, rel_bias, w_out, norm2_w, w_gate, w_up, w_down, final_norm_w):
    batch, seq, d_model = x.shape
    depth = w_in.shape[0]
    hg_width = hg_lb_logits.shape[-1]
    da_width = d_model - hg_width
    hg_heads = hg_width // HG_HEAD_DIM
    da_heads = da_width // (2 * DA_HEAD_DIM)
    assert w_in.shape[2] == 5 * hg_width + 3 * da_width
    assert seq % HG_CHUNK == 0 and seq % BIAS_TILE == 0
    m = batch * seq
    blk = LANES

    tiles = _bias_tiles(rel_bias)
    h = x.reshape(m, d_model)
    for l in range(depth):
        main, gates = _in_proj(h, norm1_w[l], w_in[l].astype(BF16), 2 * hg_width, 4 * hg_width,
                               TILES["in_proj_m"], TILES["in_proj_n"])

        o_hg = _hgrn(main, gates, hg_lb_logits, hg_onorm_w[l], batch=batch, seq=seq, heads=hg_heads,
                     layer=l, col_q=0, col_i=hg_width // blk, col_g=2 * hg_width // blk)

        lam_init = 0.8 - 0.6 * math.exp(-0.3 * l)
        lam_vecs = jnp.stack([lambda_q1[l], lambda_k1[l], lambda_q2[l], lambda_k2[l]]).astype(F32)
        base = 3 * hg_width // blk
        o_da = _diff_attn(main, lam_vecs, tiles, da_subln_w[l], batch=batch, seq=seq, heads=da_heads,
                          lam_init=lam_init, col_q=base, col_k=base + da_width // blk,
                          col_v=base + 2 * da_width // blk, tq=TILES["attn_q"])

        h, u2, w_down_bf16 = _out_proj(h, o_hg, o_da, w_out[l].astype(BF16), norm2_w[l], w_down, l,
                                       TILES["out_proj_m"])

        act = _ffn_up(u2, w_gate, w_up, l, TILES["ffn_up_m"], TILES["ffn_up_n"])
        last = l == depth - 1
        assert last, "final norm is fused into the last layer's down projection"
        h = _ffn_down(act, w_down_bf16, h, final_norm_w, TILES["ffn_down_m"])
    return h.reshape(batch, seq, d_model)
```

```python
import functools
import math

import numpy as np
import jax
import jax.numpy as jnp
from jax import lax
from jax.experimental import pallas as pl
from jax.experimental.pallas import tpu as pltpu

F32 = jnp.float32
BF16 = jnp.bfloat16

EPS = 1e-6
LOG2E = math.log2(math.e)
HG_HEAD_DIM = 128
DA_HEAD_DIM = 64
N_BUCKETS = 32
MAX_DISTANCE = 128

LANES = 128
SUBLANES = 8
HG_CHUNK = 64
HG_LEVELS = 6
HG_GROUP = 2
BIAS_TILE = 128
ATTN_BATCH = 2
ATTN_SLOTS = 2
VMEM_LIMIT = 56 * 1024 * 1024

TILES = {
    "in_proj_m": 256, "in_proj_n": 1024,
    "attn_q": 256,
    "out_proj_m": 512,
    "ffn_up_m": 2048, "ffn_up_n": 512,
    "ffn_down_m": 512,
}


def _cparams(sem, flags=None):
    return pltpu.CompilerParams(dimension_semantics=sem, vmem_limit_bytes=VMEM_LIMIT, flags=flags)


def _dot(a, b):
    return jnp.dot(a, b, preferred_element_type=F32)


def _dot_nt(a, b):
    return lax.dot_general(a, b, (((1,), (1,)), ((), ())), preferred_element_type=F32)


def _dot_tn(a, b):
    return lax.dot_general(a, b, (((0,), (0,)), ((), ())), preferred_element_type=F32)


def _sigmoid(x):
    return 0.5 * jnp.tanh(0.5 * x) + 0.5


def _silu(x):
    h = 0.5 * x
    return h + h * jnp.tanh(h)


def _in_proj_kernel(x_ref, nw_ref, w_ref, main_ref, gates_ref, *, gate_lo, gate_hi, tn):
    x = x_ref[...].astype(F32)
    ms = jnp.mean(x * x, axis=-1, keepdims=True)
    u = (x * lax.rsqrt(ms + EPS) * nw_ref[...]).astype(BF16)

    n = w_ref.shape[1]
    slab = {id(main_ref): 0, id(gates_ref): 0}
    for c0 in range(0, n, tn):
        o_ref = gates_ref if gate_lo <= c0 < gate_hi else main_ref
        r = _dot(u, w_ref[:, c0:c0 + tn]).astype(o_ref.dtype)
        for c in range(tn // LANES):
            o_ref[slab[id(o_ref)]] = r[:, c * LANES:(c + 1) * LANES]
            slab[id(o_ref)] += 1


def _in_proj(x, nw, w, gate_lo, gate_hi, tm, tn):
    m, d = x.shape
    n = w.shape[1]
    assert gate_lo % tn == 0 and gate_hi % tn == 0 and n % tn == 0
    n_gate_cols = gate_hi - gate_lo
    kern = functools.partial(_in_proj_kernel, gate_lo=gate_lo, gate_hi=gate_hi, tn=tn)
    return pl.pallas_call(
        kern,
        grid=(m // tm,),
        in_specs=[
            pl.BlockSpec((tm, d), lambda i: (i, 0)),
            pl.BlockSpec((1, d), lambda i: (0, 0)),
            pl.BlockSpec((d, n), lambda i: (0, 0), pipeline_mode=pl.Buffered(1)),
        ],
        out_specs=[
            pl.BlockSpec(((n - n_gate_cols) // LANES, tm, LANES), lambda i: (0, i, 0)),
            pl.BlockSpec((n_gate_cols // LANES, tm, LANES), lambda i: (0, i, 0)),
        ],
        out_shape=[
            jax.ShapeDtypeStruct(((n - n_gate_cols) // LANES, m, LANES), BF16),
            jax.ShapeDtypeStruct((n_gate_cols // LANES, m, LANES), F32),
        ],
        compiler_params=_cparams(("parallel",)),
        name="in_proj",
    )(x, nw.reshape(1, d), w)


def _hgrn_constants():
    c, nl, g = HG_CHUNK, HG_LEVELS, HG_GROUP
    idx = np.arange(g * c)
    same_chunk = (idx[:, None] // c) == (idx[None, :] // c)
    tri = (same_chunk & (idx[:, None] >= idx[None, :])).astype(np.float32)
    masks = np.zeros((nl + 1, g * c, g * c), np.float32)
    masks[0] = np.eye(g * c, dtype=np.float32)
    for lvl in range(nl):
        m = 1 << lvl
        same_pair = (idx[:, None] // (2 * m)) == (idx[None, :] // (2 * m))
        other_sibling = (idx[:, None] // m) != (idx[None, :] // m)
        masks[lvl + 1] = (same_pair & other_sibling).astype(np.float32)
    return tri, masks


def _level_operands(lvl, j, q, f_f, f_b, k_f, k_b, beta_f, bx_b, odd, hi2, hi4):
    sl = slice(j * SUBLANES, (j + 1) * SUBLANES)
    qj, ffj, fbj, kfj, kbj = q[sl], f_f[sl], f_b[sl], k_f[sl], k_b[sl]
    if lvl == 0:
        return qj * jnp.where(odd, ffj, fbj), jnp.where(odd, kbj, kfj)
    if lvl == 1:
        up_f, dn_f = pltpu.roll(ffj, SUBLANES - 1, axis=0), pltpu.roll(ffj, 1, axis=0)
        up_b, dn_b = pltpu.roll(fbj, SUBLANES - 1, axis=0), pltpu.roll(fbj, 1, axis=0)
        fq = jnp.where(hi2, jnp.where(odd, ffj * dn_f, ffj), jnp.where(odd, fbj, fbj * up_b))
        ks = jnp.where(hi2, jnp.where(odd, kbj * dn_b, kbj), jnp.where(odd, kfj, kfj * up_f))
        return qj * fq, ks
    bfj, bbj = beta_f[sl], bx_b[sl]
    if lvl == 2:
        r = j * SUBLANES + SUBLANES // 2
        ef = jnp.exp2(-jnp.abs(bfj - beta_f[r - 1:r]))
        eb = jnp.exp2(-jnp.abs(bbj - bx_b[r:r + 1]))
        return qj * jnp.where(hi4, ef, eb), jnp.where(hi4, kbj * eb, kfj * ef)
    mb = (1 << lvl) // SUBLANES
    r = ((j // (2 * mb)) * 2 * mb + mb) * SUBLANES
    ref_f, ref_b = beta_f[r - 1:r], bx_b[r:r + 1]
    if (j // mb) & 1:
        return qj * jnp.exp2(bfj - ref_f), kbj * jnp.exp2(bbj - ref_b)
    return qj * jnp.exp2(ref_b - bbj), kfj * jnp.exp2(ref_f - bfj)


def _hgrn_kernel(lg_ref, hq_ref, hi_ref, hg_ref, gf_ref, gb_ref, ow_ref, tri_ref, mk_ref,
                 out_ref, o_scr, qi_scr, ut_scr, dec_scr, st_scr, g_scr, *, layer, n_slots):
    c, nl, dh = HG_CHUNK, HG_LEVELS, HG_HEAD_DIM
    seq = hq_ref.shape[0]
    nc = seq // c

    lg = lg_ref[...].astype(F32)
    lbs = []
    for d in range(2):
        rows = lg[d * n_slots:(d + 1) * n_slots]
        e = jnp.exp(rows - jnp.max(rows, axis=0, keepdims=True))
        lbs.append(jnp.sum(e[:layer + 1], axis=0, keepdims=True) / jnp.sum(e, axis=0, keepdims=True))
    lb_f, lb_b = lbs

    pos = lax.broadcasted_iota(jnp.int32, (SUBLANES, dh), 0)
    odd = (pos & 1) != 0
    hi2 = (pos & 2) != 0
    hi4 = (pos & 4) != 0

    def split(x):
        hi = x.astype(BF16)
        return hi, (x - hi.astype(F32)).astype(BF16)

    gc = HG_GROUP * c
    ng = nc // HG_GROUP

    def gates(gi, carry):
        r0 = gi * gc
        f_f = lb_f + (1.0 - lb_f) * _sigmoid(gf_ref[pl.ds(r0, gc), :])
        f_b = lb_b + (1.0 - lb_b) * _sigmoid(gb_ref[pl.ds(r0, gc), :])
        lf_f = jnp.log2(f_f)
        lf_b = jnp.log2(f_b)
        pre = _dot(tri_ref[...], jnp.concatenate(split(lf_f) + split(lf_b), axis=1))
        g_scr[0, pl.ds(r0, gc), :] = f_f
        g_scr[1, pl.ds(r0, gc), :] = f_b
        g_scr[2, pl.ds(r0, gc), :] = pre[:, :dh] + pre[:, dh:2 * dh]
        g_scr[3, pl.ds(r0, gc), :] = pre[:, 2 * dh:3 * dh] + pre[:, 3 * dh:] - lf_b
        return carry


    def per_chunk_rows(x, row):
        return jnp.concatenate([jnp.broadcast_to(x[g * c + row:g * c + row + 1], (c, dh))
                                for g in range(HG_GROUP)], axis=0)

    def intra(gi, carry):
        r0 = gi * gc
        q = _silu(hq_ref[pl.ds(r0, gc), :].astype(F32))
        v = hi_ref[pl.ds(r0, gc), :]
        f_f = g_scr[0, pl.ds(r0, gc), :]
        f_b = g_scr[1, pl.ds(r0, gc), :]
        beta_f = g_scr[2, pl.ds(r0, gc), :]
        bx_b = g_scr[3, pl.ds(r0, gc), :]
        k_f = 1.0 - f_f
        k_b = 1.0 - f_b
        tot_f = per_chunk_rows(beta_f, c - 1)
        tot_b = per_chunk_rows(bx_b + jnp.log2(f_b), c - 1)

        a = mk_ref[0] * _dot_nt(q.astype(BF16), (k_f + k_b).astype(BF16))
        for lvl in range(nl):
            ops = [_level_operands(lvl, j, q, f_f, f_b, k_f, k_b, beta_f, bx_b, odd, hi2, hi4)
                   for j in range(gc // SUBLANES)]
            qs = jnp.concatenate([o[0] for o in ops], axis=0)
            ks = jnp.concatenate([o[1] for o in ops], axis=0)
            a = a + mk_ref[lvl + 1] * _dot_nt(qs.astype(BF16), ks.astype(BF16))
        o_scr[pl.ds(r0, gc), :] = _dot(a.astype(BF16), v)

        qi = jnp.concatenate([q * jnp.exp2(beta_f), q * jnp.exp2(tot_b - bx_b)], axis=1)
        qi_scr[pl.ds(r0, gc), :] = qi.astype(BF16)
        ks = jnp.concatenate([k_f * jnp.exp2(tot_f - beta_f), k_b * jnp.exp2(bx_b)], axis=1).astype(BF16)
        zero = jnp.zeros((c, dh), BF16)
        vd = jnp.concatenate([jnp.concatenate([v[g * c:(g + 1) * c] if h == g else zero
                                               for h in range(HG_GROUP)], axis=1)
                              for g in range(HG_GROUP)], axis=0)
        ut = _dot_tn(vd, ks)
        dec = jnp.exp2(jnp.concatenate([tot_f, tot_b], axis=1))
        for g in range(HG_GROUP):
            ut_scr[gi * HG_GROUP + g] = ut[g * dh:(g + 1) * dh]
            dec_scr[gi * HG_GROUP + g] = dec[g * c:g * c + 8]
        return carry

    for gi in range(ng):
        gates(gi, 0)
        intra(gi, 0)

    def states(i, carry):
        st_f, st_b = carry
        cf = i
        cb = nc - 1 - i
        st_scr[cf, :, :dh] = st_f.astype(BF16)
        st_scr[cb, :, dh:] = st_b.astype(BF16)
        st_f = st_f * dec_scr[cf, 0:1, :dh] + ut_scr[cf, :, :dh]
        st_b = st_b * dec_scr[cb, 0:1, dh:] + ut_scr[cb, :, dh:]
        return st_f, st_b

    carry = (jnp.zeros((dh, dh), F32),) * 2
    for i in range(nc):
        carry = states(i, carry)

    ow = ow_ref[...].astype(F32)

    def finish(ci):
        r0 = ci * c
        o = o_scr[pl.ds(r0, c), :] + _dot_nt(qi_scr[pl.ds(r0, c), :], st_scr[ci])
        y = o * lax.rsqrt(jnp.mean(o * o, axis=-1, keepdims=True) + EPS) * ow
        out_ref[pl.ds(r0, c), :] = (y * _silu(hg_ref[pl.ds(r0, c), :].astype(F32))).astype(out_ref.dtype)

    for ci in range(nc):
        finish(ci)


def _hgrn(main, gates, lb_logits, onorm_w, *, batch, seq, heads, layer, col_q, col_i, col_g):
    dh, c = HG_HEAD_DIM, HG_CHUNK
    n_slots = lb_logits.shape[1]
    lg = lb_logits.reshape(2 * n_slots, heads * dh)
    tri, masks = _hgrn_constants()
    nc = seq // c
    kern = functools.partial(_hgrn_kernel, layer=layer, n_slots=n_slots)
    return pl.pallas_call(
        kern,
        grid=(batch, heads),
        in_specs=[
            pl.BlockSpec((2 * n_slots, dh), lambda b, h: (0, h)),
            pl.BlockSpec((None, seq, dh), lambda b, h: (col_q + h, b, 0)),
            pl.BlockSpec((None, seq, dh), lambda b, h: (col_i + h, b, 0)),
            pl.BlockSpec((None, seq, dh), lambda b, h: (col_g + h, b, 0)),
            pl.BlockSpec((None, seq, dh), lambda b, h: (h, b, 0)),
            pl.BlockSpec((None, seq, dh), lambda b, h: (heads + h, b, 0)),
            pl.BlockSpec((1, dh), lambda b, h: (0, 0)),
            pl.BlockSpec(tri.shape, lambda b, h: (0, 0)),
            pl.BlockSpec(masks.shape, lambda b, h: (0, 0, 0)),
        ],
        out_specs=pl.BlockSpec((None, seq, dh), lambda b, h: (h, b, 0)),
        out_shape=jax.ShapeDtypeStruct((heads, batch * seq, dh), BF16),
        scratch_shapes=[
            pltpu.VMEM((seq, dh), F32),
            pltpu.VMEM((seq, 2 * dh), BF16),
            pltpu.VMEM((nc, dh, 2 * dh), F32),
            pltpu.VMEM((nc, 8, 2 * dh), F32),
            pltpu.VMEM((nc, dh, 2 * dh), BF16),
            pltpu.VMEM((4, seq, dh), F32),
        ],
        compiler_params=_cparams(("parallel", "parallel")),
        name="hgrn2",
    )(lg, main, main, main, gates, gates, onorm_w.reshape(1, dh),
      jnp.asarray(tri, BF16), jnp.asarray(masks, F32))


def _rel_bucket_index(rel):
    nb = N_BUCKETS // 2
    max_exact = nb // 2
    ret = jnp.where(rel > 0, nb, 0)
    n = jnp.abs(rel)
    nf = jnp.maximum(n, 1).astype(jnp.float32)
    large = max_exact + (jnp.log(nf / max_exact) / math.log(MAX_DISTANCE / max_exact)
                         * (nb - max_exact)).astype(jnp.int32)
    large = jnp.minimum(large, nb - 1)
    return ret + jnp.where(n < max_exact, n, large)


def _bias_tiles_kernel(tbl_ref, bkt_ref, out_ref):
    h = pl.program_id(0)
    bkt = bkt_ref[...]
    acc = jnp.zeros(bkt.shape, F32)
    for cidx in range(N_BUCKETS):
        acc = jnp.where(bkt == cidx, tbl_ref[cidx, h], acc)
    out_ref[0] = acc * LOG2E


def _bias_tiles(rel_bias):
    t = BIAS_TILE
    assert t >= MAX_DISTANCE
    heads = rel_bias.shape[1]
    key = jnp.arange(t, dtype=jnp.int32)[:, None]
    qry = jnp.arange(t, dtype=jnp.int32)[None, :]
    rel = jnp.stack([t * d + key - qry for d in range(-2, 3)])
    bkt = _rel_bucket_index(rel).astype(jnp.int32)
    return pl.pallas_call(
        _bias_tiles_kernel,
        grid=(heads,),
        in_specs=[
            pl.BlockSpec(memory_space=pltpu.SMEM),
            pl.BlockSpec((5, t, t), lambda h: (0, 0, 0)),
        ],
        out_specs=pl.BlockSpec((1, 5, t, t), lambda h: (h, 0, 0, 0)),
        out_shape=jax.ShapeDtypeStruct((heads, 5, t, t), F32),
        compiler_params=_cparams(("arbitrary",)),
        name="bias_tiles",
    )(rel_bias.astype(F32), bkt)


def _diff_attn_kernel(lam_ref, q_ref, k_ref, v_ref, tiles_ref, sw_ref, out_ref, vaug_scr, *slot_scr, lam_init, tq):
    rows_all, d2 = q_ref.shape
    nb = ATTN_BATCH
    seq = rows_all // nb
    t = BIAS_TILE
    ns = ATTN_SLOTS
    s_scr, m_scr, e_scr = slot_scr[:ns], slot_scr[ns:2 * ns], slot_scr[2 * ns:]

    for e in range(nb):
        vaug_scr[e, :d2, :] = v_ref[e * seq:(e + 1) * seq, :].T
        vaug_scr[e, d2:, :] = jnp.ones((vaug_scr.shape[1] - d2, seq), BF16)

    lv = lam_ref[...].astype(F32)
    lam = (jnp.exp(jnp.sum(lv[0:1] * lv[1:2], axis=-1, keepdims=True))
           - jnp.exp(jnp.sum(lv[2:3] * lv[3:4], axis=-1, keepdims=True)) + lam_init)

    c = DA_HEAD_DIM ** -0.5 * LOG2E
    lane = lax.broadcasted_iota(jnp.int32, (tq, d2), 1)
    first = lane < DA_HEAD_DIM
    sw = sw_ref[...].astype(F32) * (1.0 - lam_init)

    def scores(n, slot):
        r0 = n * tq
        e, n = divmod(n, seq // tq)
        q = q_ref[pl.ds(r0, tq), :]
        zero = jnp.zeros_like(q)
        cols = []
        for ib in range(tq // t):
            d = [min(max(jb - (n * (tq // t) + ib), -2), 2) + 2 for jb in range(seq // t)]
            cols.append(jnp.concatenate([tiles_ref[0, dj] for dj in d], axis=0))
        bias = jnp.concatenate(cols, axis=1)
        qq = jnp.concatenate([jnp.where(first, q, zero), jnp.where(first, zero, q)], axis=0)
        st = _dot_nt(k_ref[e * seq:(e + 1) * seq, :], qq) * c
        st = jnp.concatenate([st[:, :tq] + bias, st[:, tq:] + bias], axis=1)
        s_scr[slot][...] = st
        m_scr[slot][...] = jnp.max(st, axis=0, keepdims=True)

    def numerators(slot):
        e_scr[slot][...] = jnp.exp2(s_scr[slot][...] - m_scr[slot][...]).astype(BF16)

    def values(n, slot):
        r0 = n * tq
        ot = _dot(vaug_scr[n // (seq // tq)], e_scr[slot][...])
        on = ot[:d2] / ot[d2:d2 + 1]
        o = on[:, :tq] - lam * on[:, tq:]
        yt = o * lax.rsqrt(jnp.mean(o * o, axis=0, keepdims=True) + EPS)
        out_ref[pl.ds(r0, tq), :] = (yt.T * sw).astype(out_ref.dtype)

    nq = rows_all // tq
    scores(0, 0)
    scores(1, 1 % ns)
    numerators(0)
    for j in range(nq):
        if j + 2 < nq:
            scores(j + 2, (j + 2) % ns)
        if j + 1 < nq:
            numerators((j + 1) % ns)
        values(j, j % ns)


def _diff_attn(main, lam_vecs, tiles, subln_w, *, batch, seq, heads, lam_init, col_q, col_k, col_v, tq):
    d2 = 2 * DA_HEAD_DIM
    t = BIAS_TILE
    assert batch % ATTN_BATCH == 0
    kern = functools.partial(_diff_attn_kernel, lam_init=lam_init, tq=tq)
    return pl.pallas_call(
        kern,
        grid=(heads, batch // ATTN_BATCH),
        in_specs=[
            pl.BlockSpec(lam_vecs.shape, lambda h, b: (0, 0)),
            pl.BlockSpec((None, ATTN_BATCH * seq, d2), lambda h, b: (col_q + h, b, 0)),
            pl.BlockSpec((None, ATTN_BATCH * seq, d2), lambda h, b: (col_k + h, b, 0)),
            pl.BlockSpec((None, ATTN_BATCH * seq, d2), lambda h, b: (col_v + h, b, 0)),
            pl.BlockSpec((1, 5, t, t), lambda h, b: (h, 0, 0, 0)),
            pl.BlockSpec((1, d2), lambda h, b: (0, 0)),
        ],
        out_specs=pl.BlockSpec((None, ATTN_BATCH * seq, d2), lambda h, b: (h, b, 0)),
        out_shape=jax.ShapeDtypeStruct((heads, batch * seq, d2), BF16),
        scratch_shapes=([pltpu.VMEM((ATTN_BATCH, d2 + 2 * SUBLANES, seq), BF16)]
                        + [pltpu.VMEM((seq, 2 * tq), F32)] * ATTN_SLOTS
                        + [pltpu.VMEM((1, 2 * tq), F32)] * ATTN_SLOTS
                        + [pltpu.VMEM((seq, 2 * tq), BF16)] * ATTN_SLOTS),
        compiler_params=_cparams(("parallel", "parallel")),
        name="diff_attn",
    )(lam_vecs, main, main, main, tiles, subln_w.reshape(1, d2))


def _out_proj_kernel(x_ref, a_ref, b_ref, wa_ref, wb_ref, nw_ref, wd_ref, h_ref, u_ref, wd_out_ref):
    def heads_to_lanes(ref):
        return jnp.concatenate([ref[c] for c in range(ref.shape[0])], axis=1)

    h = x_ref[...] + _dot(heads_to_lanes(a_ref), wa_ref[...]) + _dot(heads_to_lanes(b_ref), wb_ref[...])
    h_ref[...] = h
    ms = jnp.mean(h * h, axis=-1, keepdims=True)
    u_ref[...] = (h * lax.rsqrt(ms + EPS) * nw_ref[...]).astype(u_ref.dtype)
    wd_out_ref[...] = wd_ref[...].astype(wd_out_ref.dtype)


def _out_proj(x, a, b, w, nw, wd, layer, tm):
    m, d = x.shape
    ka, kb = a.shape[0] * a.shape[2], b.shape[0] * b.shape[2]
    assert ka == kb and w.shape == (ka + kb, d)
    steps = m // tm
    f = wd.shape[1]
    rows = f // steps
    assert rows * steps == f and rows % (2 * SUBLANES) == 0
    resident = pl.Buffered(1)
    return pl.pallas_call(
        _out_proj_kernel,
        grid=(steps,),
        in_specs=[
            pl.BlockSpec((tm, d), lambda i: (i, 0)),
            pl.BlockSpec((a.shape[0], tm, a.shape[2]), lambda i: (0, i, 0)),
            pl.BlockSpec((b.shape[0], tm, b.shape[2]), lambda i: (0, i, 0)),
            pl.BlockSpec((ka, d), lambda i: (0, 0), pipeline_mode=resident),
            pl.BlockSpec((kb, d), lambda i: (1, 0), pipeline_mode=resident),
            pl.BlockSpec((1, d), lambda i: (0, 0)),
            pl.BlockSpec((None, rows, wd.shape[2]), lambda i: (layer, i, 0)),
        ],
        out_specs=[pl.BlockSpec((tm, d), lambda i: (i, 0)), pl.BlockSpec((tm, d), lambda i: (i, 0)),
                   pl.BlockSpec((rows, wd.shape[2]), lambda i: (i, 0))],
        out_shape=[jax.ShapeDtypeStruct((m, d), F32), jax.ShapeDtypeStruct((m, d), BF16),
                   jax.ShapeDtypeStruct(wd.shape[1:], BF16)],
        compiler_params=_cparams(("parallel",)),
        name="out_proj",
    )(x, a, b, w, w, nw.reshape(1, d), wd)


def _ffn_up_kernel(u_ref, wg_ref, wu_ref, o_ref, wg_scr, wu_scr):
    @pl.when(pl.program_id(1) == 0)
    def _():
        wg_scr[...] = wg_ref[...].astype(BF16)
        wu_scr[...] = wu_ref[...].astype(BF16)

    sub = min(512, u_ref.shape[0])
    for r in range(u_ref.shape[0] // sub):
        rows = slice(r * sub, (r + 1) * sub)
        u = u_ref[rows, :]
        g = _dot(u, wg_scr[...])
        up = _dot(u, wu_scr[...])
        o_ref[rows, :] = (_silu(g) * up).astype(o_ref.dtype)


def _ffn_up(u, wg, wu, layer, tm, tn):
    m, d = u.shape
    n = wg.shape[2]
    return pl.pallas_call(
        _ffn_up_kernel,
        grid=(n // tn, m // tm),
        in_specs=[
            pl.BlockSpec((tm, d), lambda j, i: (i, 0)),
            pl.BlockSpec((None, d, tn), lambda j, i: (layer, 0, j)),
            pl.BlockSpec((None, d, tn), lambda j, i: (layer, 0, j)),
        ],
        out_specs=pl.BlockSpec((tm, tn), lambda j, i: (i, j)),
        out_shape=jax.ShapeDtypeStruct((m, n), BF16),
        scratch_shapes=[pltpu.VMEM((d, tn), BF16), pltpu.VMEM((d, tn), BF16)],
        compiler_params=_cparams(("parallel", "arbitrary")),
        name="ffn_up",
    )(u, wg, wu)


def _ffn_down_kernel(a_ref, w_ref, h_ref, fw_ref, o_ref):
    y = h_ref[...] + _dot(a_ref[...], w_ref[...])
    ms = jnp.mean(y * y, axis=-1, keepdims=True)
    o_ref[...] = y * lax.rsqrt(ms + EPS) * fw_ref[...]


def _ffn_down(a, w, h, fw, tm):
    m, f = a.shape
    d = w.shape[1]
    return pl.pallas_call(
        _ffn_down_kernel,
        grid=(m // tm,),
        in_specs=[
            pl.BlockSpec((tm, f), lambda i: (i, 0)),
            pl.BlockSpec((f, d), lambda i: (0, 0), pipeline_mode=pl.Buffered(1)),
            pl.BlockSpec((tm, d), lambda i: (i, 0)),
            pl.BlockSpec((1, d), lambda i: (0, 0)),
        ],
        out_specs=pl.BlockSpec((tm, d), lambda i: (i, 0)),
        out_shape=jax.ShapeDtypeStruct((m, d), F32),
        compiler_params=_cparams(("parallel",)),
        name="ffn_down",
    )(a, w, h, fw.reshape(1, d))


def kernel(x, norm1_w, w_in, hg_lb_logits, hg_onorm_w, lambda_q1, lambda_k1, lambda_q2, lambda_k2,
           da_subln_w, rel_bias, w_out, norm2_w, w_gate, w_up, w_down, final_norm_w):
    batch, seq, d_model = x.shape
    depth = w_in.shape[0]
    hg_width = hg_lb_logits.shape[-1]
    da_width = d_model - hg_width
    hg_heads = hg_width // HG_HEAD_DIM
    da_heads = da_width // (2 * DA_HEAD_DIM)
    assert w_in.shape[2] == 5 * hg_width + 3 * da_width
    assert seq % HG_CHUNK == 0 and seq % BIAS_TILE == 0
    m = batch * seq
    blk = LANES

    tiles = _bias_tiles(rel_bias)
    h = x.reshape(m, d_model)
    for l in range(depth):
        main, gates = _in_proj(h, norm1_w[l], w_in[l].astype(BF16), 2 * hg_width, 4 * hg_width,
                               TILES["in_proj_m"], TILES["in_proj_n"])

        o_hg = _hgrn(main, gates, hg_lb_logits, hg_onorm_w[l], batch=batch, seq=seq, heads=hg_heads,
                     layer=l, col_q=0, col_i=hg_width // blk, col_g=2 * hg_width // blk)

        lam_init = 0.8 - 0.6 * math.exp(-0.3 * l)
        lam_vecs = jnp.stack([lambda_q1[l], lambda_k1[l], lambda_q2[l], lambda_k2[l]]).astype(F32)
        base = 3 * hg_width // blk
        o_da = _diff_attn(main, lam_vecs, tiles, da_subln_w[l], batch=batch, seq=seq, heads=da_heads,
                          lam_init=lam_init, col_q=base, col_k=base + da_width // blk,
                          col_v=base + 2 * da_width // blk, tq=TILES["attn_q"])

        h, u2, w_down_bf16 = _out_proj(h, o_hg, o_da, w_out[l].astype(BF16), norm2_w[l], w_down, l,
                                       TILES["out_proj_m"])

        act = _ffn_up(u2, w_gate, w_up, l, TILES["ffn_up_m"], TILES["ffn_up_n"])
        last = l == depth - 1
        assert last, "final norm is fused into the last layer's down projection"
        h = _ffn_down(act, w_down_bf16, h, final_norm_w, TILES["ffn_down_m"])
    return h.reshape(batch, seq, d_model)
```

```python
import functools
import math

import numpy as np
import jax
import jax.numpy as jnp
from jax import lax
from jax.experimental import pallas as pl
from jax.experimental.pallas import tpu as pltpu

F32 = jnp.float32
BF16 = jnp.bfloat16

EPS = 1e-6
LOG2E = math.log2(math.e)
HG_HEAD_DIM = 128
DA_HEAD_DIM = 64
N_BUCKETS = 32
MAX_DISTANCE = 128

LANES = 128
SUBLANES = 8
HG_CHUNK = 64
HG_LEVELS = 6
HG_BATCH = 2
HG_GROUP = 2
BIAS_TILE = 128
ATTN_BATCH = 2
ATTN_SLOTS = 2
VMEM_LIMIT = 56 * 1024 * 1024

TILES = {
    "in_proj_m": 256, "in_proj_n": 1024,
    "attn_q": 256,
    "out_proj_m": 512,
    "ffn_up_m": 2048, "ffn_up_n": 512,
    "ffn_down_m": 512,
}


def _cparams(sem, flags=None):
    return pltpu.CompilerParams(dimension_semantics=sem, vmem_limit_bytes=VMEM_LIMIT, flags=flags)


def _dot(a, b):
    return jnp.dot(a, b, preferred_element_type=F32)


def _dot_nt(a, b):
    return lax.dot_general(a, b, (((1,), (1,)), ((), ())), preferred_element_type=F32)


def _dot_tn(a, b):
    return lax.dot_general(a, b, (((0,), (0,)), ((), ())), preferred_element_type=F32)


def _sigmoid(x):
    return 0.5 * jnp.tanh(0.5 * x) + 0.5


def _silu(x):
    h = 0.5 * x
    return h + h * jnp.tanh(h)


def _in_proj_kernel(x_ref, nw_ref, w_ref, main_ref, gates_ref, *, gate_lo, gate_hi, tn):
    x = x_ref[...].astype(F32)
    ms = jnp.mean(x * x, axis=-1, keepdims=True)
    u = (x * lax.rsqrt(ms + EPS) * nw_ref[...]).astype(BF16)

    n = w_ref.shape[1]
    slab = {id(main_ref): 0, id(gates_ref): 0}
    for c0 in range(0, n, tn):
        o_ref = gates_ref if gate_lo <= c0 < gate_hi else main_ref
        r = _dot(u, w_ref[:, c0:c0 + tn]).astype(o_ref.dtype)
        for c in range(tn // LANES):
            o_ref[slab[id(o_ref)]] = r[:, c * LANES:(c + 1) * LANES]
            slab[id(o_ref)] += 1


def _in_proj(x, nw, w, gate_lo, gate_hi, tm, tn):
    m, d = x.shape
    n = w.shape[1]
    assert gate_lo % tn == 0 and gate_hi % tn == 0 and n % tn == 0
    n_gate_cols = gate_hi - gate_lo
    kern = functools.partial(_in_proj_kernel, gate_lo=gate_lo, gate_hi=gate_hi, tn=tn)
    return pl.pallas_call(
        kern,
        grid=(m // tm,),
        in_specs=[
            pl.BlockSpec((tm, d), lambda i: (i, 0)),
            pl.BlockSpec((1, d), lambda i: (0, 0)),
            pl.BlockSpec((d, n), lambda i: (0, 0), pipeline_mode=pl.Buffered(1)),
        ],
        out_specs=[
            pl.BlockSpec(((n - n_gate_cols) // LANES, tm, LANES), lambda i: (0, i, 0)),
            pl.BlockSpec((n_gate_cols // LANES, tm, LANES), lambda i: (0, i, 0)),
        ],
        out_shape=[
            jax.ShapeDtypeStruct(((n - n_gate_cols) // LANES, m, LANES), BF16),
            jax.ShapeDtypeStruct((n_gate_cols // LANES, m, LANES), F32),
        ],
        compiler_params=_cparams(("parallel",)),
        name="in_proj",
    )(x, nw.reshape(1, d), w)


def _hgrn_constants():
    c, nl, g = HG_CHUNK, HG_LEVELS, HG_GROUP
    idx = np.arange(g * c)
    same_chunk = (idx[:, None] // c) == (idx[None, :] // c)
    tri = (same_chunk & (idx[:, None] >= idx[None, :])).astype(np.float32)
    masks = np.zeros((nl + 1, g * c, g * c), np.float32)
    masks[0] = np.eye(g * c, dtype=np.float32)
    for lvl in range(nl):
        m = 1 << lvl
        same_pair = (idx[:, None] // (2 * m)) == (idx[None, :] // (2 * m))
        other_sibling = (idx[:, None] // m) != (idx[None, :] // m)
        masks[lvl + 1] = (same_pair & other_sibling).astype(np.float32)
    return tri, masks


def _level_operands(lvl, j, q, f_f, f_b, k_f, k_b, beta_f, bx_b, odd, hi2, hi4):
    sl = slice(j * SUBLANES, (j + 1) * SUBLANES)
    qj, ffj, fbj, kfj, kbj = q[sl], f_f[sl], f_b[sl], k_f[sl], k_b[sl]
    if lvl == 0:
        return qj * jnp.where(odd, ffj, fbj), jnp.where(odd, kbj, kfj)
    if lvl == 1:
        up_f, dn_f = pltpu.roll(ffj, SUBLANES - 1, axis=0), pltpu.roll(ffj, 1, axis=0)
        up_b, dn_b = pltpu.roll(fbj, SUBLANES - 1, axis=0), pltpu.roll(fbj, 1, axis=0)
        fq = jnp.where(hi2, jnp.where(odd, ffj * dn_f, ffj), jnp.where(odd, fbj, fbj * up_b))
        ks = jnp.where(hi2, jnp.where(odd, kbj * dn_b, kbj), jnp.where(odd, kfj, kfj * up_f))
        return qj * fq, ks
    bfj, bbj = beta_f[sl], bx_b[sl]
    if lvl == 2:
        r = j * SUBLANES + SUBLANES // 2
        ef = jnp.exp2(-jnp.abs(bfj - beta_f[r - 1:r]))
        eb = jnp.exp2(-jnp.abs(bbj - bx_b[r:r + 1]))
        return qj * jnp.where(hi4, ef, eb), jnp.where(hi4, kbj * eb, kfj * ef)
    mb = (1 << lvl) // SUBLANES
    r = ((j // (2 * mb)) * 2 * mb + mb) * SUBLANES
    ref_f, ref_b = beta_f[r - 1:r], bx_b[r:r + 1]
    if (j // mb) & 1:
        return qj * jnp.exp2(bfj - ref_f), kbj * jnp.exp2(bbj - ref_b)
    return qj * jnp.exp2(ref_b - bbj), kfj * jnp.exp2(ref_f - bfj)


def _hgrn_kernel(lg_ref, hq_ref, hi_ref, hg_ref, gf_ref, gb_ref, ow_ref, tri_ref, mk_ref,
                 out_ref, o_scr, qi_scr, ut_scr, dec_scr, st_scr, g_scr, *, layer, n_slots):
    c, nl, dh = HG_CHUNK, HG_LEVELS, HG_HEAD_DIM
    seq = hq_ref.shape[0]
    nc = seq // c

    lg = lg_ref[...].astype(F32)
    lbs = []
    for d in range(2):
        rows = lg[d * n_slots:(d + 1) * n_slots]
        e = jnp.exp(rows - jnp.max(rows, axis=0, keepdims=True))
        lbs.append(jnp.sum(e[:layer + 1], axis=0, keepdims=True) / jnp.sum(e, axis=0, keepdims=True))
    lb_f, lb_b = lbs

    pos = lax.broadcasted_iota(jnp.int32, (SUBLANES, dh), 0)
    odd = (pos & 1) != 0
    hi2 = (pos & 2) != 0
    hi4 = (pos & 4) != 0

    def split(x):
        hi = x.astype(BF16)
        return hi, (x - hi.astype(F32)).astype(BF16)

    gc = HG_GROUP * c
    ng = nc // HG_GROUP

    def gates(gi, carry):
        r0 = gi * gc
        f_f = lb_f + (1.0 - lb_f) * _sigmoid(gf_ref[pl.ds(r0, gc), :])
        f_b = lb_b + (1.0 - lb_b) * _sigmoid(gb_ref[pl.ds(r0, gc), :])
        lf_f = jnp.log2(f_f)
        lf_b = jnp.log2(f_b)
        pre = _dot(tri_ref[...], jnp.concatenate(split(lf_f) + split(lf_b), axis=1))
        g_scr[0, pl.ds(r0, gc), :] = f_f
        g_scr[1, pl.ds(r0, gc), :] = f_b
        g_scr[2, pl.ds(r0, gc), :] = pre[:, :dh] + pre[:, dh:2 * dh]
        g_scr[3, pl.ds(r0, gc), :] = pre[:, 2 * dh:3 * dh] + pre[:, 3 * dh:] - lf_b
        return carry


    def per_chunk_rows(x, row):
        return jnp.concatenate([jnp.broadcast_to(x[g * c + row:g * c + row + 1], (c, dh))
                                for g in range(HG_GROUP)], axis=0)

    def intra(gi, carry):
        r0 = gi * gc
        q = _silu(hq_ref[pl.ds(r0, gc), :].astype(F32))
        v = hi_ref[pl.ds(r0, gc), :]
        f_f = g_scr[0, pl.ds(r0, gc), :]
        f_b = g_scr[1, pl.ds(r0, gc), :]
        beta_f = g_scr[2, pl.ds(r0, gc), :]
        bx_b = g_scr[3, pl.ds(r0, gc), :]
        k_f = 1.0 - f_f
        k_b = 1.0 - f_b
        tot_f = per_chunk_rows(beta_f, c - 1)
        tot_b = per_chunk_rows(bx_b + jnp.log2(f_b), c - 1)

        a = mk_ref[0] * _dot_nt(q.astype(BF16), (k_f + k_b).astype(BF16))
        for lvl in range(nl):
            ops = [_level_operands(lvl, j, q, f_f, f_b, k_f, k_b, beta_f, bx_b, odd, hi2, hi4)
                   for j in range(gc // SUBLANES)]
            qs = jnp.concatenate([o[0] for o in ops], axis=0)
            ks = jnp.concatenate([o[1] for o in ops], axis=0)
            a = a + mk_ref[lvl + 1] * _dot_nt(qs.astype(BF16), ks.astype(BF16))
        o_scr[pl.ds(r0, gc), :] = _dot(a.astype(BF16), v)

        qi = jnp.concatenate([q * jnp.exp2(beta_f), q * jnp.exp2(tot_b - bx_b)], axis=1)
        qi_scr[pl.ds(r0, gc), :] = qi.astype(BF16)
        ks = jnp.concatenate([k_f * jnp.exp2(tot_f - beta_f), k_b * jnp.exp2(bx_b)], axis=1).astype(BF16)
        zero = jnp.zeros((c, dh), BF16)
        vd = jnp.concatenate([jnp.concatenate([v[g * c:(g + 1) * c] if h == g else zero
                                               for h in range(HG_GROUP)], axis=1)
                              for g in range(HG_GROUP)], axis=0)
        ut = _dot_tn(vd, ks)
        dec = jnp.exp2(jnp.concatenate([tot_f, tot_b], axis=1))
        for g in range(HG_GROUP):
            ut_scr[gi * HG_GROUP + g] = ut[g * dh:(g + 1) * dh]
            dec_scr[gi * HG_GROUP + g] = dec[g * c:g * c + 8]
        return carry

    for gi in range(ng):
        gates(gi, 0)
        intra(gi, 0)

    nce = nc // HG_BATCH

    def states(e, i, carry):
        st_f, st_b = carry
        cf = e * nce + i
        cb = e * nce + nce - 1 - i
        st_scr[cf, :, :dh] = st_f.astype(BF16)
        st_scr[cb, :, dh:] = st_b.astype(BF16)
        st_f = st_f * dec_scr[cf, 0:1, :dh] + ut_scr[cf, :, :dh]
        st_b = st_b * dec_scr[cb, 0:1, dh:] + ut_scr[cb, :, dh:]
        return st_f, st_b

    carry = [(jnp.zeros((dh, dh), F32),) * 2 for _ in range(HG_BATCH)]
    for i in range(nce):
        for e in range(HG_BATCH):
            carry[e] = states(e, i, carry[e])

    ow = ow_ref[...].astype(F32)

    def finish(ci):
        r0 = ci * c
        o = o_scr[pl.ds(r0, c), :] + _dot_nt(qi_scr[pl.ds(r0, c), :], st_scr[ci])
        y = o * lax.rsqrt(jnp.mean(o * o, axis=-1, keepdims=True) + EPS) * ow
        out_ref[pl.ds(r0, c), :] = (y * _silu(hg_ref[pl.ds(r0, c), :].astype(F32))).astype(out_ref.dtype)

    for ci in range(nc):
        finish(ci)


def _hgrn(main, gates, lb_logits, onorm_w, *, batch, seq, heads, layer, col_q, col_i, col_g):
    dh, c = HG_HEAD_DIM, HG_CHUNK
    n_slots = lb_logits.shape[1]
    lg = lb_logits.reshape(2 * n_slots, heads * dh)
    tri, masks = _hgrn_constants()
    assert batch % HG_BATCH == 0
    seq = HG_BATCH * seq
    nc = seq // c
    kern = functools.partial(_hgrn_kernel, layer=layer, n_slots=n_slots)
    return pl.pallas_call(
        kern,
        grid=(batch // HG_BATCH, heads),
        in_specs=[
            pl.BlockSpec((2 * n_slots, dh), lambda b, h: (0, h)),
            pl.BlockSpec((None, seq, dh), lambda b, h: (col_q + h, b, 0)),
            pl.BlockSpec((None, seq, dh), lambda b, h: (col_i + h, b, 0)),
            pl.BlockSpec((None, seq, dh), lambda b, h: (col_g + h, b, 0)),
            pl.BlockSpec((None, seq, dh), lambda b, h: (h, b, 0)),
            pl.BlockSpec((None, seq, dh), lambda b, h: (heads + h, b, 0)),
            pl.BlockSpec((1, dh), lambda b, h: (0, 0)),
            pl.BlockSpec(tri.shape, lambda b, h: (0, 0)),
            pl.BlockSpec(masks.shape, lambda b, h: (0, 0, 0)),
        ],
        out_specs=pl.BlockSpec((None, seq, dh), lambda b, h: (h, b, 0)),
        out_shape=jax.ShapeDtypeStruct((heads, batch // HG_BATCH * seq, dh), BF16),
        scratch_shapes=[
            pltpu.VMEM((seq, dh), F32),
            pltpu.VMEM((seq, 2 * dh), BF16),
            pltpu.VMEM((nc, dh, 2 * dh), F32),
            pltpu.VMEM((nc, 8, 2 * dh), F32),
            pltpu.VMEM((nc, dh, 2 * dh), BF16),
            pltpu.VMEM((4, seq, dh), F32),
        ],
        compiler_params=_cparams(("parallel", "parallel")),
        name="hgrn2",
    )(lg, main, main, main, gates, gates, onorm_w.reshape(1, dh),
      jnp.asarray(tri, BF16), jnp.asarray(masks, F32))


def _rel_bucket_index(rel):
    nb = N_BUCKETS // 2
    max_exact = nb // 2
    ret = jnp.where(rel > 0, nb, 0)
    n = jnp.abs(rel)
    nf = jnp.maximum(n, 1).astype(jnp.float32)
    large = max_exact + (jnp.log(nf / max_exact) / math.log(MAX_DISTANCE / max_exact)
                         * (nb - max_exact)).astype(jnp.int32)
    large = jnp.minimum(large, nb - 1)
    return ret + jnp.where(n < max_exact, n, large)


def _bias_tiles_kernel(tbl_ref, bkt_ref, out_ref):
    h = pl.program_id(0)
    bkt = bkt_ref[...]
    acc = jnp.zeros(bkt.shape, F32)
    for cidx in range(N_BUCKETS):
        acc = jnp.where(bkt == cidx, tbl_ref[cidx, h], acc)
    out_ref[0] = acc * LOG2E


def _bias_tiles(rel_bias):
    t = BIAS_TILE
    assert t >= MAX_DISTANCE
    heads = rel_bias.shape[1]
    key = jnp.arange(t, dtype=jnp.int32)[:, None]
    qry = jnp.arange(t, dtype=jnp.int32)[None, :]
    rel = jnp.stack([t * d + key - qry for d in range(-2, 3)])
    bkt = _rel_bucket_index(rel).astype(jnp.int32)
    return pl.pallas_call(
        _bias_tiles_kernel,
        grid=(heads,),
        in_specs=[
            pl.BlockSpec(memory_space=pltpu.SMEM),
            pl.BlockSpec((5, t, t), lambda h: (0, 0, 0)),
        ],
        out_specs=pl.BlockSpec((1, 5, t, t), lambda h: (h, 0, 0, 0)),
        out_shape=jax.ShapeDtypeStruct((heads, 5, t, t), F32),
        compiler_params=_cparams(("arbitrary",)),
        name="bias_tiles",
    )(rel_bias.astype(F32), bkt)


def _diff_attn_kernel(lam_ref, q_ref, k_ref, v_ref, tiles_ref, sw_ref, out_ref, vaug_scr, *slot_scr, lam_init, tq):
    rows_all, d2 = q_ref.shape
    nb = ATTN_BATCH
    seq = rows_all // nb
    t = BIAS_TILE
    ns = ATTN_SLOTS
    s_scr, m_scr, e_scr = slot_scr[:ns], slot_scr[ns:2 * ns], slot_scr[2 * ns:]

    for e in range(nb):
        vaug_scr[e, :d2, :] = v_ref[e * seq:(e + 1) * seq, :].T
        vaug_scr[e, d2:, :] = jnp.ones((vaug_scr.shape[1] - d2, seq), BF16)

    lv = lam_ref[...].astype(F32)
    lam = (jnp.exp(jnp.sum(lv[0:1] * lv[1:2], axis=-1, keepdims=True))
           - jnp.exp(jnp.sum(lv[2:3] * lv[3:4], axis=-1, keepdims=True)) + lam_init)

    c = DA_HEAD_DIM ** -0.5 * LOG2E
    lane = lax.broadcasted_iota(jnp.int32, (tq, d2), 1)
    first = lane < DA_HEAD_DIM
    sw = sw_ref[...].astype(F32) * (1.0 - lam_init)

    def scores(n, slot):
        r0 = n * tq
        e, n = divmod(n, seq // tq)
        q = q_ref[pl.ds(r0, tq), :]
        zero = jnp.zeros_like(q)
        cols = []
        for ib in range(tq // t):
            d = [min(max(jb - (n * (tq // t) + ib), -2), 2) + 2 for jb in range(seq // t)]
            cols.append(jnp.concatenate([tiles_ref[0, dj] for dj in d], axis=0))
        bias = jnp.concatenate(cols, axis=1)
        qq = jnp.concatenate([jnp.where(first, q, zero), jnp.where(first, zero, q)], axis=0)
        st = _dot_nt(k_ref[e * seq:(e + 1) * seq, :], qq) * c
        st = jnp.concatenate([st[:, :tq] + bias, st[:, tq:] + bias], axis=1)
        s_scr[slot][...] = st
        m_scr[slot][...] = jnp.max(st, axis=0, keepdims=True)

    def numerators(slot):
        e_scr[slot][...] = jnp.exp2(s_scr[slot][...] - m_scr[slot][...]).astype(BF16)

    def values(n, slot):
        r0 = n * tq
        ot = _dot(vaug_scr[n // (seq // tq)], e_scr[slot][...])
        on = ot[:d2] / ot[d2:d2 + 1]
        o = on[:, :tq] - lam * on[:, tq:]
        yt = o * lax.rsqrt(jnp.mean(o * o, axis=0, keepdims=True) + EPS)
        out_ref[pl.ds(r0, tq), :] = (yt.T * sw).astype(out_ref.dtype)

    nq = rows_all // tq
    scores(0, 0)
    scores(1, 1 % ns)
    numerators(0)
    for j in range(nq):
        if j + 2 < nq:
            scores(j + 2, (j + 2) % ns)
        if j + 1 < nq:
            numerators((j + 1) % ns)
        values(j, j % ns)


def _diff_attn(main, lam_vecs, tiles, subln_w, *, batch, seq, heads, lam_init, col_q, col_k, col_v, tq):
    d2 = 2 * DA_HEAD_DIM
    t = BIAS_TILE
    assert batch % ATTN_BATCH == 0
    kern = functools.partial(_diff_attn_kernel, lam_init=lam_init, tq=tq)
    return pl.pallas_call(
        kern,
        grid=(heads, batch // ATTN_BATCH),
        in_specs=[
            pl.BlockSpec(lam_vecs.shape, lambda h, b: (0, 0)),
            pl.BlockSpec((None, ATTN_BATCH * seq, d2), lambda h, b: (col_q + h, b, 0)),
            pl.BlockSpec((None, ATTN_BATCH * seq, d2), lambda h, b: (col_k + h, b, 0)),
            pl.BlockSpec((None, ATTN_BATCH * seq, d2), lambda h, b: (col_v + h, b, 0)),
            pl.BlockSpec((1, 5, t, t), lambda h, b: (h, 0, 0, 0)),
            pl.BlockSpec((1, d2), lambda h, b: (0, 0)),
        ],
        out_specs=pl.BlockSpec((None, ATTN_BATCH * seq, d2), lambda h, b: (h, b, 0)),
        out_shape=jax.ShapeDtypeStruct((heads, batch * seq, d2), BF16),
        scratch_shapes=([pltpu.VMEM((ATTN_BATCH, d2 + 2 * SUBLANES, seq), BF16)]
                        + [pltpu.VMEM((seq, 2 * tq), F32)] * ATTN_SLOTS
                        + [pltpu.VMEM((1, 2 * tq), F32)] * ATTN_SLOTS
                        + [pltpu.VMEM((seq, 2 * tq), BF16)] * ATTN_SLOTS),
        compiler_params=_cparams(("parallel", "parallel")),
        name="diff_attn",
    )(lam_vecs, main, main, main, tiles, subln_w.reshape(1, d2))


def _out_proj_kernel(x_ref, a_ref, b_ref, wa_ref, wb_ref, nw_ref, wd_ref, h_ref, u_ref, wd_out_ref):
    def heads_to_lanes(ref):
        return jnp.concatenate([ref[c] for c in range(ref.shape[0])], axis=1)

    h = x_ref[...] + _dot(heads_to_lanes(a_ref), wa_ref[...]) + _dot(heads_to_lanes(b_ref), wb_ref[...])
    h_ref[...] = h
    ms = jnp.mean(h * h, axis=-1, keepdims=True)
    u_ref[...] = (h * lax.rsqrt(ms + EPS) * nw_ref[...]).astype(u_ref.dtype)
    wd_out_ref[...] = wd_ref[...].astype(wd_out_ref.dtype)


def _out_proj(x, a, b, w, nw, wd, layer, tm):
    m, d = x.shape
    ka, kb = a.shape[0] * a.shape[2], b.shape[0] * b.shape[2]
    assert ka == kb and w.shape == (ka + kb, d)
    steps = m // tm
    f = wd.shape[1]
    rows = f // steps
    assert rows * steps == f and rows % (2 * SUBLANES) == 0
    resident = pl.Buffered(1)
    return pl.pallas_call(
        _out_proj_kernel,
        grid=(steps,),
        in_specs=[
            pl.BlockSpec((tm, d), lambda i: (i, 0)),
            pl.BlockSpec((a.shape[0], tm, a.shape[2]), lambda i: (0, i, 0)),
            pl.BlockSpec((b.shape[0], tm, b.shape[2]), lambda i: (0, i, 0)),
            pl.BlockSpec((ka, d), lambda i: (0, 0), pipeline_mode=resident),
            pl.BlockSpec((kb, d), lambda i: (1, 0), pipeline_mode=resident),
            pl.BlockSpec((1, d), lambda i: (0, 0)),
            pl.BlockSpec((None, rows, wd.shape[2]), lambda i: (layer, i, 0)),
        ],
        out_specs=[pl.BlockSpec((tm, d), lambda i: (i, 0)), pl.BlockSpec((tm, d), lambda i: (i, 0)),
                   pl.BlockSpec((rows, wd.shape[2]), lambda i: (i, 0))],
        out_shape=[jax.ShapeDtypeStruct((m, d), F32), jax.ShapeDtypeStruct((m, d), BF16),
                   jax.ShapeDtypeStruct(wd.shape[1:], BF16)],
        compiler_params=_cparams(("parallel",)),
        name="out_proj",
    )(x, a, b, w, w, nw.reshape(1, d), wd)


def _ffn_up_kernel(u_ref, wg_ref, wu_ref, o_ref, wg_scr, wu_scr):
    @pl.when(pl.program_id(1) == 0)
    def _():
        wg_scr[...] = wg_ref[...].astype(BF16)
        wu_scr[...] = wu_ref[...].astype(BF16)

    sub = min(512, u_ref.shape[0])
    for r in range(u_ref.shape[0] // sub):
        rows = slice(r * sub, (r + 1) * sub)
        u = u_ref[rows, :]
        g = _dot(u, wg_scr[...])
        up = _dot(u, wu_scr[...])
        o_ref[rows, :] = (_silu(g) * up).astype(o_ref.dtype)


def _ffn_up(u, wg, wu, layer, tm, tn):
    m, d = u.shape
    n = wg.shape[2]
    return pl.pallas_call(
        _ffn_up_kernel,
        grid=(n // tn, m // tm),
        in_specs=[
            pl.BlockSpec((tm, d), lambda j, i: (i, 0)),
            pl.BlockSpec((None, d, tn), lambda j, i: (layer, 0, j)),
            pl.BlockSpec((None, d, tn), lambda j, i: (layer, 0, j)),
        ],
        out_specs=pl.BlockSpec((tm, tn), lambda j, i: (i, j)),
        out_shape=jax.ShapeDtypeStruct((m, n), BF16),
        scratch_shapes=[pltpu.VMEM((d, tn), BF16), pltpu.VMEM((d, tn), BF16)],
        compiler_params=_cparams(("parallel", "arbitrary")),
        name="ffn_up",
    )(u, wg, wu)


def _ffn_down_kernel(a_ref, w_ref, h_ref, fw_ref, o_ref):
    y = h_ref[...] + _dot(a_ref[...], w_ref[...])
    ms = jnp.mean(y * y, axis=-1, keepdims=True)
    o_ref[...] = y * lax.rsqrt(ms + EPS) * fw_ref[...]


def _ffn_down(a, w, h, fw, tm):
    m, f = a.shape
    d = w.shape[1]
    return pl.pallas_call(
        _ffn_down_kernel,
        grid=(m // tm,),
        in_specs=[
            pl.BlockSpec((tm, f), lambda i: (i, 0)),
            pl.BlockSpec((f, d), lambda i: (0, 0), pipeline_mode=pl.Buffered(1)),
            pl.BlockSpec((tm, d), lambda i: (i, 0)),
            pl.BlockSpec((1, d), lambda i: (0, 0)),
        ],
        out_specs=pl.BlockSpec((tm, d), lambda i: (i, 0)),
        out_shape=jax.ShapeDtypeStruct((m, d), F32),
        compiler_params=_cparams(("parallel",)),
        name="ffn_down",
    )(a, w, h, fw.reshape(1, d))


def kernel(x, norm1_w, w_in, hg_lb_logits, hg_onorm_w, lambda_q1, lambda_k1, lambda_q2, lambda_k2,
           da_subln_w, rel_bias, w_out, norm2_w, w_gate, w_up, w_down, final_norm_w):
    batch, seq, d_model = x.shape
    depth = w_in.shape[0]
    hg_width = hg_lb_logits.shape[-1]
    da_width = d_model - hg_width
    hg_heads = hg_width // HG_HEAD_DIM
    da_heads = da_width // (2 * DA_HEAD_DIM)
    assert w_in.shape[2] == 5 * hg_width + 3 * da_width
    assert seq % HG_CHUNK == 0 and seq % BIAS_TILE == 0
    m = batch * seq
    blk = LANES

    tiles = _bias_tiles(rel_bias)
    h = x.reshape(m, d_model)
    for l in range(depth):
        main, gates = _in_proj(h, norm1_w[l], w_in[l].astype(BF16), 2 * hg_width, 4 * hg_width,
                               TILES["in_proj_m"], TILES["in_proj_n"])

        o_hg = _hgrn(main, gates, hg_lb_logits, hg_onorm_w[l], batch=batch, seq=seq, heads=hg_heads,
                     layer=l, col_q=0, col_i=hg_width // blk, col_g=2 * hg_width // blk)

        lam_init = 0.8 - 0.6 * math.exp(-0.3 * l)
        lam_vecs = jnp.stack([lambda_q1[l], lambda_k1[l], lambda_q2[l], lambda_k2[l]]).astype(F32)
        base = 3 * hg_width // blk
        o_da = _diff_attn(main, lam_vecs, tiles, da_subln_w[l], batch=batch, seq=seq, heads=da_heads,
                          lam_init=lam_init, col_q=base, col_k=base + da_width // blk,
                          col_v=base + 2 * da_width // blk, tq=TILES["attn_q"])

        h, u2, w_down_bf16 = _out_proj(h, o_hg, o_da, w_out[l].astype(BF16), norm2_w[l], w_down, l,
                                       TILES["out_proj_m"])

        act = _ffn_up(u2, w_gate, w_up, l, TILES["ffn_up_m"], TILES["ffn_up_n"])
        last = l == depth - 1
        assert last, "final norm is fused into the last layer's down projection"
        h = _ffn_down(act, w_down_bf16, h, final_norm_w, TILES["ffn_down_m"])
    return h.reshape(batch, seq, d_model)
```
